```python
import math
import jax, jax.numpy as jnp
from jax import lax
import numpy as np

D_MODEL = 1024
BATCH = 16
SEQ = 2048
DEPTH = 2

DIFF_HEADS = 4
DIFF_HEAD_DIM = 64
DIFF_V_DIM = 2 * DIFF_HEAD_DIM
DIFF_WIDTH = DIFF_HEADS * DIFF_V_DIM
WIN_HEADS = 8
WIN_KV_HEADS = 2
WIN_GROUP = WIN_HEADS // WIN_KV_HEADS
WIN_HEAD_DIM = 64
WIN_WIDTH = WIN_HEADS * WIN_HEAD_DIM
WINDOW = 128
BLOCK = 128
REL_BUCKETS = 32
REL_MAX_DIST = 128
TOTAL_HEADS = DIFF_HEADS + WIN_HEADS
C_DQ = DIFF_HEADS * 2 * DIFF_HEAD_DIM
C_DK = DIFF_HEADS * 2 * DIFF_HEAD_DIM
C_DV = DIFF_WIDTH
C_WQ = WIN_WIDTH
C_WK = WIN_KV_HEADS * WIN_HEAD_DIM
C_WV = WIN_KV_HEADS * WIN_HEAD_DIM
C_GATE = 2 * D_MODEL
IN_COLS = C_DQ + C_DK + C_DV + C_WQ + C_WK + C_WV + C_GATE
IN_SPLITS = (C_DQ, C_DQ + C_DK, C_DQ + C_DK + C_DV, C_DQ + C_DK + C_DV + C_WQ,
             C_DQ + C_DK + C_DV + C_WQ + C_WK, C_DQ + C_DK + C_DV + C_WQ + C_WK + C_WV)
N_GROUPS = 4
EXPERTS_PER_GROUP = 8
N_EXPERTS = N_GROUPS * EXPERTS_PER_GROUP
TOP_K = 2
EXPERT_HIDDEN = 512
MOE_BLOCK = 128
RMS_EPS = 1e-6
NEG_INF = -1e30

kernel_name = "hybrid_diffattn_wingqa_hiermoe_encoder"


def rms_norm(x, g):
    xf = x.astype(jnp.float32)
    y = xf * lax.rsqrt(jnp.mean(xf * xf, axis=-1, keepdims=True) + RMS_EPS)
    return (y * g.astype(jnp.float32)).astype(x.dtype)


def rel_bucket(rel):
    half = REL_BUCKETS // 2
    max_exact = half // 2
    n = jnp.abs(rel)
    nf = jnp.maximum(n, max_exact).astype(jnp.float32)
    large = max_exact + (jnp.log(nf / max_exact) / math.log(REL_MAX_DIST / max_exact)
                         * (half - max_exact)).astype(jnp.int32)
    large = jnp.minimum(large, half - 1)
    return jnp.where(rel > 0, half, 0) + jnp.where(n < max_exact, n, large)


def diff_attention(q, k, v, lam, rel_table, subln_g, lambda_init):
    B, S = q.shape[:2]
    nb = S // BLOCK
    scale = DIFF_HEAD_DIM ** -0.5
    k_pos = jnp.arange(S)
    qb = q.reshape(B, nb, BLOCK, DIFF_HEADS, 2, DIFF_HEAD_DIM).swapaxes(0, 1)

    def one_block(args):
        q_blk, start = args
        s = jnp.einsum('bqhcd,bkhcd->bhcqk', q_blk, k).astype(jnp.float32) * scale
        q_pos = start + jnp.arange(BLOCK)
        bias = rel_table[rel_bucket(k_pos[None, :] - q_pos[:, None])]
        s = s + bias.transpose(2, 0, 1)[None, :, None].astype(jnp.float32)
        p = jax.nn.softmax(s, axis=-1)
        a = p[:, :, 0] - lam * p[:, :, 1]
        return jnp.einsum('bhqk,bkhe->bqhe', a.astype(v.dtype), v)

    o = lax.map(one_block, (qb, jnp.arange(nb) * BLOCK))
    o = o.swapaxes(0, 1).reshape(B, S, DIFF_HEADS, DIFF_V_DIM)
    o = rms_norm(o, subln_g) * (1.0 - lambda_init)
    return o.reshape(B, S, DIFF_WIDTH)


def window_attention(q, k, v, rel_table, sink):
    B, S = q.shape[:2]
    nb = S // BLOCK
    scale = WIN_HEAD_DIM ** -0.5
    qb = q.reshape(B, nb, BLOCK, WIN_KV_HEADS, WIN_GROUP, WIN_HEAD_DIM).swapaxes(0, 1)
    pad = ((0, 0), (BLOCK, BLOCK), (0, 0), (0, 0))

    def band(t):
        tp = jnp.pad(t, pad)
        parts = [tp[:, o:o + S].reshape(B, nb, BLOCK, WIN_KV_HEADS, WIN_HEAD_DIM)
                 for o in (0, BLOCK, 2 * BLOCK)]
        return jnp.concatenate(parts, axis=2).swapaxes(0, 1)

    kb, vb = band(k), band(v)
    rel = jnp.arange(3 * BLOCK)[None, :] - BLOCK - jnp.arange(BLOCK)[:, None]
    bias = rel_table[rel_bucket(rel)].transpose(2, 0, 1)
    bias = bias.reshape(WIN_KV_HEADS, WIN_GROUP, BLOCK, 3 * BLOCK).astype(jnp.float32)
    k_pos = jnp.arange(nb)[:, None] * BLOCK - BLOCK + jnp.arange(3 * BLOCK)[None, :]
    valid = ((jnp.abs(rel) <= WINDOW)[None]
             & ((k_pos >= 0) & (k_pos < S))[:, None, :])
    sink_l = sink.reshape(WIN_KV_HEADS, WIN_GROUP)[None, :, :, None, None].astype(jnp.float32)

    def one_block(args):
        q_blk, k_blk, v_blk, ok = args
        s = jnp.einsum('bqhgd,bkhd->bhgqk', q_blk, k_blk).astype(jnp.float32) * scale + bias
        s = jnp.where(ok[None, None, None], s, NEG_INF)
        m = jnp.maximum(jnp.max(s, axis=-1, keepdims=True), sink_l)
        p = jnp.exp(s - m)
        p = p / (jnp.sum(p, axis=-1, keepdims=True) + jnp.exp(sink_l - m))
        return jnp.einsum('bhgqk,bkhd->bqhgd', p.astype(v_blk.dtype), v_blk)

    o = lax.map(one_block, (qb, kb, vb, valid))
    return o.swapaxes(0, 1).reshape(B, S, WIN_WIDTH)


def hier_moe(h, w_rg, b_rg, w_re, b_re, w_gate, w_up, w_down):
    B, S, D = h.shape
    N = B * S
    t = h.reshape(N, D)
    g_logits = (t @ w_rg).astype(jnp.float32) + b_rg.astype(jnp.float32)
    g_prob = jax.nn.softmax(g_logits, axis=-1)
    g_sel = jnp.argmax(g_logits, axis=-1).astype(jnp.int32)
    g_w = jnp.take_along_axis(g_prob, g_sel[:, None], axis=-1)
    e_logits = ((t @ w_re).astype(jnp.float32) + b_re.astype(jnp.float32))
    e_logits = e_logits.reshape(N, N_GROUPS, EXPERTS_PER_GROUP)
    e_logits = jnp.take_along_axis(e_logits, g_sel[:, None, None], axis=1)[:, 0]
    top_v, top_i = lax.top_k(e_logits, TOP_K)
    weights = jax.nn.softmax(top_v, axis=-1) * g_w
    expert = g_sel[:, None] * EXPERTS_PER_GROUP + top_i.astype(jnp.int32)

    A = N * TOP_K
    flat_e = expert.reshape(A)
    order = jnp.argsort(flat_e)
    sorted_e = flat_e[order]
    counts = jnp.bincount(flat_e, length=N_EXPERTS).astype(jnp.int32)
    padded = (counts + MOE_BLOCK - 1) // MOE_BLOCK * MOE_BLOCK
    pad_end = jnp.cumsum(padded)
    pad_start = pad_end - padded
    start = jnp.cumsum(counts) - counts
    dest_sorted = pad_start[sorted_e] + jnp.arange(A, dtype=jnp.int32) - start[sorted_e]
    dest = jnp.zeros((A,), jnp.int32).at[order].set(dest_sorted)
    n_blocks = -(-(A + N_EXPERTS * (MOE_BLOCK - 1)) // MOE_BLOCK)
    R = n_blocks * MOE_BLOCK
    row_token = jnp.full((R,), N, jnp.int32).at[dest].set(jnp.arange(A, dtype=jnp.int32) // TOP_K)
    t_pad = jnp.concatenate([t, jnp.zeros((1, D), t.dtype)], axis=0)
    xs = t_pad[row_token].reshape(n_blocks, MOE_BLOCK, D)
    block_e = jnp.minimum(jnp.searchsorted(pad_end, jnp.arange(n_blocks) * MOE_BLOCK, side='right'),
                          N_EXPERTS - 1).astype(jnp.int32)

    def expert_block(args):
        xb, e = args
        hid = jax.nn.silu(xb @ w_gate[e]) * (xb @ w_up[e])
        return hid @ w_down[e]

    ys = lax.map(expert_block, (xs, block_e)).reshape(R, D)
    y = ys[dest].reshape(N, TOP_K, D)
    out = jnp.einsum('nk,nkd->nd', weights.astype(y.dtype), y)
    return out.reshape(B, S, D)


def setup_inputs(seed: int = 0) -> dict:
    key = jax.random.key(seed)
    ks = jax.random.split(key, 20)

    def nrm(k, shape, s):
        return jax.random.normal(k, shape, jnp.float32) * s

    L, D, F, E = DEPTH, D_MODEL, EXPERT_HIDDEN, N_EXPERTS
    return {
        'x': nrm(ks[0], (BATCH, SEQ, D), 1.0),
        'w_in': nrm(ks[1], (L, D, IN_COLS), D ** -0.5),
        'diff_lambda': nrm(ks[2], (L, 4, DIFF_HEAD_DIM), 0.1),
        'diff_subln': 1.0 + nrm(ks[3], (L, DIFF_V_DIM), 0.02),
        'win_sink': nrm(ks[4], (L, WIN_HEADS), 0.5),
        'w_branch_diff': nrm(ks[5], (L, DIFF_WIDTH, D), DIFF_WIDTH ** -0.5),
        'w_branch_win': nrm(ks[6], (L, WIN_WIDTH, D), WIN_WIDTH ** -0.5),
        'w_out': nrm(ks[7], (L, D, D), D ** -0.5),
        'rel_bias': nrm(ks[8], (REL_BUCKETS, TOTAL_HEADS), 0.5),
        'norm_mix': 1.0 + nrm(ks[9], (L, D), 0.02),
        'norm_ffn': 1.0 + nrm(ks[10], (L, D), 0.02),
        'w_router_group': nrm(ks[11], (L, D, N_GROUPS), D ** -0.5),
        'b_router_group': nrm(ks[12], (L, N_GROUPS), 0.01),
        'w_router_expert': nrm(ks[13], (L, D, E), D ** -0.5),
        'b_router_expert': nrm(ks[14], (L, E), 0.01),
        'w_exp_gate': nrm(ks[15], (L, E, D, F), D ** -0.5),
        'w_exp_up': nrm(ks[16], (L, E, D, F), D ** -0.5),
        'w_exp_down': nrm(ks[17], (L, E, F, D), F ** -0.5),
        'norm_final': 1.0 + nrm(ks[18], (D,), 0.02),
    }


def reference(x, w_in, diff_lambda, diff_subln, win_sink, w_branch_diff, w_branch_win, w_out,
              rel_bias, norm_mix, norm_ffn, w_router_group, b_router_group, w_router_expert,
              b_router_expert, w_exp_gate, w_exp_up, w_exp_down, norm_final):
    B, S, _ = x.shape
    rel_diff = rel_bias[:, :DIFF_HEADS]
    rel_win = rel_bias[:, DIFF_HEADS:]
    for l in range(DEPTH):
        h = rms_norm(x, norm_mix[l])
        proj = h @ w_in[l]
        dq, dk, dv, wq, wk, wv, gates = jnp.split(proj, IN_SPLITS, axis=-1)

        lambda_init = 0.8 - 0.6 * math.exp(-0.3 * l)
        lam_p = diff_lambda[l].astype(jnp.float32)
        lam = (jnp.exp(jnp.sum(lam_p[0] * lam_p[1])) - jnp.exp(jnp.sum(lam_p[2] * lam_p[3]))
               + lambda_init)
        o_diff = diff_attention(dq.reshape(B, S, DIFF_HEADS, 2, DIFF_HEAD_DIM),
                                dk.reshape(B, S, DIFF_HEADS, 2, DIFF_HEAD_DIM),
                                dv.reshape(B, S, DIFF_HEADS, DIFF_V_DIM),
                                lam, rel_diff, diff_subln[l], lambda_init)
        o_win = window_attention(wq.reshape(B, S, WIN_HEADS, WIN_HEAD_DIM),
                                 wk.reshape(B, S, WIN_KV_HEADS, WIN_HEAD_DIM),
                                 wv.reshape(B, S, WIN_KV_HEADS, WIN_HEAD_DIM),
                                 rel_win, win_sink[l])

        gate_diff, gate_win = jnp.split(jax.nn.sigmoid(gates), 2, axis=-1)
        merged = gate_diff * (o_diff @ w_branch_diff[l]) + gate_win * (o_win @ w_branch_win[l])
        x = x + merged @ w_out[l]

        x = x + hier_moe(rms_norm(x, norm_ffn[l]), w_router_group[l], b_router_group[l],
                         w_router_expert[l], b_router_expert[l],
                         w_exp_gate[l], w_exp_up[l], w_exp_down[l])
    return rms_norm(x, norm_final)
```

```python
import functools
import math

import jax
import jax.numpy as jnp
from jax import lax
from jax.experimental import pallas as pl
from jax.experimental.pallas import tpu as pltpu

D_MODEL = 1024
DIFF_HEADS = 4
DIFF_HEAD_DIM = 64
DIFF_V_DIM = 2 * DIFF_HEAD_DIM
DIFF_WIDTH = DIFF_HEADS * DIFF_V_DIM
WIN_HEADS = 8
WIN_KV_HEADS = 2
WIN_GROUP = WIN_HEADS // WIN_KV_HEADS
WIN_HEAD_DIM = 64
WIN_WIDTH = WIN_HEADS * WIN_HEAD_DIM
WINDOW = 128
WIN_BLOCK = 128
REL_BUCKETS = 32
REL_MAX_DIST = 128
N_GROUPS = 4
EXPERTS_PER_GROUP = 8
N_EXPERTS = N_GROUPS * EXPERTS_PER_GROUP
EXPERT_HIDDEN = 512
RMS_EPS = 1e-6
NEG_INF = -1e30

LANES = 128
V7X_VMEM_BYTES = 64 * 1024 * 1024

ROW_TILE = 512
DIFF_TQ = 256
DIFF_TK = 512
POS_TILE = 256
MOE_TILE = 256
MOVE_TILE = 256

F32 = jnp.float32
BF16 = jnp.bfloat16


def _vmem_limit(nbytes):
    return int(min(max(2 * nbytes, 16 * 1024 * 1024), V7X_VMEM_BYTES - 8 * 1024 * 1024))


def _rel_bucket(rel):
    half = REL_BUCKETS // 2
    max_exact = half // 2
    n = jnp.abs(rel)
    nf = jnp.maximum(n, max_exact).astype(jnp.float32)
    large = max_exact + (jnp.log(nf / max_exact) / math.log(REL_MAX_DIST / max_exact)
                         * (half - max_exact)).astype(jnp.int32)
    large = jnp.minimum(large, half - 1)
    return jnp.where(rel > 0, half, 0) + jnp.where(n < max_exact, n, large)


C_DQ, C_DK, C_DV, C_WQ = 512, 512, 512, 512
C_WKD, C_WVD = 2 * LANES, 2 * LANES
C_GATE = 2 * D_MODEL
IN_COLS = C_DQ + C_DK + C_DV + C_WQ + C_WKD + C_WVD + C_GATE


def _inproj_kernel(x_ref, g_ref, w_ref, dq_ref, dk_ref, dv_ref, wq_ref, wk_ref, wv_ref, gate_ref):
    x = x_ref[...]
    ms = jnp.mean(x * x, axis=-1, keepdims=True)
    h = (x * lax.rsqrt(ms + RMS_EPS) * g_ref[...]).astype(BF16)

    col = 0

    def proj(width):
        nonlocal col
        out = jnp.dot(h, w_ref[:, col:col + width], preferred_element_type=F32)
        col += width
        return out

    dq_ref[...] = (proj(C_DQ) * (DIFF_HEAD_DIM ** -0.5)).astype(BF16)
    dk_ref[...] = proj(C_DK).astype(BF16)
    dv_ref[...] = proj(C_DV).astype(BF16)
    wq_ref[...] = (proj(C_WQ) * (WIN_HEAD_DIM ** -0.5)).astype(BF16)
    wk_ref[...] = proj(C_WKD).astype(BF16)
    wv_ref[...] = proj(C_WVD).astype(BF16)
    gate_ref[...] = jax.nn.sigmoid(proj(C_GATE)).astype(BF16)


def _inproj(x2, g, w):
    n = x2.shape[0]
    tm = ROW_TILE
    widths = (C_DQ, C_DK, C_DV, C_WQ, C_WKD, C_WVD, C_GATE)
    est = 2 * tm * D_MODEL * 4 + D_MODEL * IN_COLS * 2 + 2 * tm * IN_COLS * 2 + tm * IN_COLS * 4
    return pl.pallas_call(
        _inproj_kernel,
        grid=(n // tm,),
        in_specs=[
            pl.BlockSpec((tm, D_MODEL), lambda i: (i, 0)),
            pl.BlockSpec((1, D_MODEL), lambda i: (0, 0)),
            pl.BlockSpec((D_MODEL, IN_COLS), lambda i: (0, 0), pipeline_mode=pl.Buffered(1)),
        ],
        out_specs=[pl.BlockSpec((tm, c), lambda i: (i, 0)) for c in widths],
        out_shape=[jax.ShapeDtypeStruct((n, c), BF16) for c in widths],
        compiler_params=pltpu.CompilerParams(
            dimension_semantics=("parallel",), vmem_limit_bytes=_vmem_limit(est)),
        name="inproj",
    )(x2, g, w)


def _diff_attn_kernel(lam_ref, q_ref, k_ref, v_ref, bias_ref, g_ref, o_ref, *, seq, lambda_init):
    qi = pl.program_id(2)
    lp = lam_ref[...]
    lam = (jnp.exp(jnp.sum(lp[0:1] * lp[1:2], axis=-1, keepdims=True))
           - jnp.exp(jnp.sum(lp[2:3] * lp[3:4], axis=-1, keepdims=True)) + lambda_init)

    q = q_ref[...]
    lane = lax.broadcasted_iota(jnp.int32, (1, LANES), 1)
    zero = jnp.zeros_like(q)
    qs = (jnp.where(lane < DIFF_HEAD_DIM, q, zero), jnp.where(lane >= DIFF_HEAD_DIM, q, zero))

    tq = q.shape[0]
    m = [jnp.full((tq, 1), NEG_INF, F32) for _ in range(2)]
    l = [jnp.zeros((tq, 1), F32) for _ in range(2)]
    acc = [jnp.zeros((tq, DIFF_V_DIM), F32) for _ in range(2)]
    off = seq - tq
    for kc in range(seq // DIFF_TK):
        k = k_ref[kc * DIFF_TK:(kc + 1) * DIFF_TK, :]
        v = v_ref[kc * DIFF_TK:(kc + 1) * DIFF_TK, :]
        start = pl.multiple_of(kc * DIFF_TK + off - qi * tq, LANES)
        bias = bias_ref[0, :, pl.ds(start, DIFF_TK)]
        for c in range(2):
            s = lax.dot_general(qs[c], k, (((1,), (1,)), ((), ())),
                                preferred_element_type=F32) + bias
            m_new = jnp.maximum(m[c], jnp.max(s, axis=-1, keepdims=True))
            alpha = jnp.exp(m[c] - m_new)
            p = jnp.exp(s - m_new)
            l[c] = alpha * l[c] + jnp.sum(p, axis=-1, keepdims=True)
            acc[c] = alpha * acc[c] + jnp.dot(p.astype(BF16), v, preferred_element_type=F32)
            m[c] = m_new
    o = acc[0] / l[0] - lam * (acc[1] / l[1])
    ms = jnp.mean(o * o, axis=-1, keepdims=True)
    o = o * lax.rsqrt(ms + RMS_EPS) * g_ref[...] * (1.0 - lambda_init)
    o_ref[...] = o.astype(o_ref.dtype)


def _diff_bias_strip(rel_diff, seq, tq):
    width = 2 * seq - tq
    i = jnp.arange(tq)[:, None]
    u = jnp.arange(width)[None, :]
    return rel_diff[_rel_bucket(u - i - (seq - tq))].transpose(2, 0, 1).astype(F32)


def _diff_attention(dq, dk, dv, lam_p, strip, subln_g, batch, seq, lambda_init):
    tq = DIFF_TQ
    nq = seq // tq
    width = strip.shape[-1]
    est = (2 * tq * LANES * 2 * 2 + 2 * 2 * seq * LANES * 2 + 2 * tq * width * 4
           + 8 * tq * DIFF_TK * 4)
    return pl.pallas_call(
        functools.partial(_diff_attn_kernel, seq=seq, lambda_init=lambda_init),
        grid=(batch, DIFF_HEADS, nq),
        in_specs=[
            pl.BlockSpec((4, DIFF_HEAD_DIM), lambda b, h, i: (0, 0)),
            pl.BlockSpec((tq, LANES), lambda b, h, i: (b * nq + i, h)),
            pl.BlockSpec((seq, LANES), lambda b, h, i: (b, h)),
            pl.BlockSpec((seq, LANES), lambda b, h, i: (b, h)),
            pl.BlockSpec((1, tq, width), lambda b, h, i: (h, 0, 0)),
            pl.BlockSpec((1, DIFF_V_DIM), lambda b, h, i: (0, 0)),
        ],
        out_specs=pl.BlockSpec((tq, LANES), lambda b, h, i: (b * nq + i, h)),
        out_shape=jax.ShapeDtypeStruct((batch * seq, DIFF_WIDTH), BF16),
        compiler_params=pltpu.CompilerParams(
            dimension_semantics=("parallel", "parallel", "parallel"),
            vmem_limit_bytes=_vmem_limit(est)),
        name="diff_attn",
    )(lam_p, dq, dk, dv, strip, subln_g)


WIN_BAND = 3 * WIN_BLOCK
WIN_STRIP = 5 * WIN_BLOCK


def _win_attn_kernel(sink_ref, q_ref, k_ref, v_ref, strip_ref, o_ref, *, seq):
    n = pl.program_id(1)
    start = pl.multiple_of(jnp.clip(n * WIN_BLOCK - WIN_BLOCK, 0, seq - WIN_BAND), WIN_BLOCK)
    ustart = pl.multiple_of(start - n * WIN_BLOCK + 2 * WIN_BLOCK, WIN_BLOCK)
    lane = lax.broadcasted_iota(jnp.int32, (1, LANES), 1)
    low = lane < WIN_HEAD_DIM
    for kv in range(WIN_KV_HEADS):
        kb = k_ref[pl.ds(start, WIN_BAND), kv * LANES:(kv + 1) * LANES]
        vb = v_ref[pl.ds(start, WIN_BAND), kv * LANES:(kv + 1) * LANES]
        for pair in range(WIN_GROUP // 2):
            c0 = kv * WIN_GROUP * WIN_HEAD_DIM + pair * LANES
            qp = q_ref[:, c0:c0 + LANES]
            zero = jnp.zeros_like(qp)
            outs = []
            for half in range(2):
                hd = kv * WIN_GROUP + 2 * pair + half
                qm = jnp.where(low if half == 0 else jnp.logical_not(low), qp, zero)
                s = lax.dot_general(qm, kb, (((1,), (1,)), ((), ())), preferred_element_type=F32)
                s = s + strip_ref[hd, :, pl.ds(ustart, WIN_BAND)]
                sink = sink_ref[hd]
                m = jnp.maximum(jnp.max(s, axis=-1, keepdims=True), sink)
                p = jnp.exp(s - m)
                den = jnp.sum(p, axis=-1, keepdims=True) + jnp.exp(sink - m)
                outs.append(jnp.dot(p.astype(BF16), vb, preferred_element_type=F32) / den)
            o_ref[:, c0:c0 + LANES] = jnp.where(low, outs[0], outs[1]).astype(o_ref.dtype)


def _win_bias_strip(rel_win):
    i = jnp.arange(WIN_BLOCK)[:, None]
    u = jnp.arange(WIN_STRIP)[None, :]
    rel = u - i - 2 * WIN_BLOCK
    bias = rel_win[_rel_bucket(rel)].transpose(2, 0, 1).astype(F32)
    return jnp.where((jnp.abs(rel) <= WINDOW)[None], bias, NEG_INF)


def _win_attention(wq, wk, wv, sink, strip, batch, seq):
    nb = seq // WIN_BLOCK
    est = (2 * WIN_BLOCK * WIN_WIDTH * 2 * 2 + 2 * 2 * seq * 2 * LANES * 2
           + WIN_HEADS * WIN_BLOCK * WIN_STRIP * 4 * 2 + 16 * WIN_BLOCK * WIN_BAND * 4)
    return pl.pallas_call(
        functools.partial(_win_attn_kernel, seq=seq),
        grid=(batch, nb),
        in_specs=[
            pl.BlockSpec(memory_space=pltpu.SMEM),
            pl.BlockSpec((WIN_BLOCK, WIN_WIDTH), lambda b, n: (b * nb + n, 0)),
            pl.BlockSpec((seq, 2 * LANES), lambda b, n: (b, 0)),
            pl.BlockSpec((seq, 2 * LANES), lambda b, n: (b, 0)),
            pl.BlockSpec((WIN_HEADS, WIN_BLOCK, WIN_STRIP), lambda b, n: (0, 0, 0)),
        ],
        out_specs=pl.BlockSpec((WIN_BLOCK, WIN_WIDTH), lambda b, n: (b * nb + n, 0)),
        out_shape=jax.ShapeDtypeStruct((batch * seq, WIN_WIDTH), BF16),
        compiler_params=pltpu.CompilerParams(
            dimension_semantics=("parallel", "parallel"), vmem_limit_bytes=_vmem_limit(est)),
        name="win_attn",
    )(sink, wq, wk, wv, strip)


ROUTE_E1, ROUTE_E2, ROUTE_W1, ROUTE_W2 = 0, 1, 2, 3


def _merge_kernel(x_ref, od_ref, ow_ref, gate_ref, pd_ref, pw_ref, wo_ref, g_ref, wr_ref, br_ref,
                  x1_ref, hn_ref, route_ref):
    md = jnp.dot(od_ref[...], pd_ref[...], preferred_element_type=F32)
    mw = jnp.dot(ow_ref[...], pw_ref[...], preferred_element_type=F32)
    merged = (gate_ref[:, :D_MODEL].astype(F32) * md + gate_ref[:, D_MODEL:].astype(F32) * mw)
    x1 = x_ref[...] + jnp.dot(merged.astype(BF16), wo_ref[...], preferred_element_type=F32)
    x1_ref[...] = x1
    ms = jnp.mean(x1 * x1, axis=-1, keepdims=True)
    hn = x1 * lax.rsqrt(ms + RMS_EPS) * g_ref[...]
    hn_ref[...] = hn

    logits = jnp.dot(hn, wr_ref[...], preferred_element_type=F32,
                     precision=lax.Precision.HIGHEST) + br_ref[...]
    lane = lax.broadcasted_iota(jnp.int32, logits.shape, 1)
    big = jnp.int32(LANES)
    is_group = lane < N_GROUPS
    gl = jnp.where(is_group, logits, NEG_INF)
    gmax = jnp.max(gl, axis=-1, keepdims=True)
    gsel = jnp.min(jnp.where(gl == gmax, lane, big), axis=-1, keepdims=True)
    gden = jnp.sum(jnp.where(is_group, jnp.exp(gl - gmax), 0.0), axis=-1, keepdims=True)
    gw = 1.0 / gden
    lo = N_GROUPS + EXPERTS_PER_GROUP * gsel
    in_group = jnp.logical_and(lane >= lo, lane < lo + EXPERTS_PER_GROUP)
    el = jnp.where(in_group, logits, NEG_INF)
    t1 = jnp.max(el, axis=-1, keepdims=True)
    i1 = jnp.min(jnp.where(el == t1, lane, big), axis=-1, keepdims=True)
    el2 = jnp.where(lane == i1, NEG_INF, el)
    t2 = jnp.max(el2, axis=-1, keepdims=True)
    i2 = jnp.min(jnp.where(el2 == t2, lane, big), axis=-1, keepdims=True)
    z = jnp.exp(t2 - t1)
    w1 = gw / (1.0 + z)
    w2 = gw * z / (1.0 + z)
    e1 = (i1 - N_GROUPS).astype(F32)
    e2 = (i2 - N_GROUPS).astype(F32)
    route = jnp.where(lane == ROUTE_E1, e1,
                      jnp.where(lane == ROUTE_E2, e2,
                                jnp.where(lane == ROUTE_W1, w1,
                                          jnp.where(lane == ROUTE_W2, w2, 0.0))))
    route_ref[...] = route


def _merge(x2, od, ow, gates, pd, pw, wo, g, wr, br):
    n = x2.shape[0]
    tm = ROW_TILE
    est = (2 * tm * D_MODEL * 4 * 3 + 2 * tm * (DIFF_WIDTH + WIN_WIDTH + C_GATE) * 2
           + 2 * (DIFF_WIDTH + WIN_WIDTH + D_MODEL) * D_MODEL * 2 + D_MODEL * LANES * 4 * 2
           + 6 * tm * D_MODEL * 4)
    row = lambda c: pl.BlockSpec((tm, c), lambda i: (i, 0))
    full = lambda r, c: pl.BlockSpec((r, c), lambda i: (0, 0))
    return pl.pallas_call(
        _merge_kernel,
        grid=(n // tm,),
        in_specs=[row(D_MODEL), row(DIFF_WIDTH), row(WIN_WIDTH), row(C_GATE),
                  full(DIFF_WIDTH, D_MODEL), full(WIN_WIDTH, D_MODEL), full(D_MODEL, D_MODEL),
                  full(1, D_MODEL), full(D_MODEL, LANES), full(1, LANES)],
        out_specs=[row(D_MODEL), row(D_MODEL), row(LANES)],
        out_shape=[jax.ShapeDtypeStruct((n, D_MODEL), F32), jax.ShapeDtypeStruct((n, D_MODEL), F32),
                   jax.ShapeDtypeStruct((n, LANES), F32)],
        compiler_params=pltpu.CompilerParams(
            dimension_semantics=("parallel",), vmem_limit_bytes=_vmem_limit(est)),
        name="merge_route",
    )(x2, od, ow, gates, pd, pw, wo, g, wr, br)


def _positions_kernel(route_ref, pos_ref, counts_ref, carry_ref):
    i = pl.program_id(0)

    @pl.when(i == 0)
    def _():
        carry_ref[...] = jnp.zeros_like(carry_ref)

    r = route_ref[...]
    tb = r.shape[0]
    lane = lax.broadcasted_iota(jnp.int32, r.shape, 1)
    lane_f = lane.astype(F32)
    oh1 = lane_f == r[:, ROUTE_E1:ROUTE_E1 + 1]
    oh2 = lane_f == r[:, ROUTE_E2:ROUTE_E2 + 1]
    cnt = jnp.where(oh1, 1.0, 0.0) + jnp.where(oh2, 1.0, 0.0)
    rows = lax.broadcasted_iota(jnp.int32, (tb, tb), 0)
    cols = lax.broadcasted_iota(jnp.int32, (tb, tb), 1)
    tri = jnp.where(rows >= cols, 1.0, 0.0).astype(BF16)
    incl = jnp.dot(tri, cnt.astype(BF16), preferred_element_type=F32)
    excl = incl - cnt + carry_ref[...]
    rank1 = jnp.sum(jnp.where(oh1, excl, 0.0), axis=-1, keepdims=True)
    rank2 = jnp.sum(jnp.where(oh2, excl, 0.0), axis=-1, keepdims=True)
    pos_ref[...] = jnp.where(lane == ROUTE_E1, rank1, jnp.where(lane == ROUTE_E2, rank2, 0.0))
    carry_ref[...] = carry_ref[...] + incl[tb - 1:tb, :]
    counts_ref[...] = carry_ref[...]


def _positions(route):
    n = route.shape[0]
    tb = POS_TILE
    return pl.pallas_call(
        _positions_kernel,
        grid=(n // tb,),
        in_specs=[pl.BlockSpec((tb, LANES), lambda i: (i, 0))],
        out_specs=[pl.BlockSpec((tb, LANES), lambda i: (i, 0)),
                   pl.BlockSpec((1, LANES), lambda i: (0, 0))],
        out_shape=[jax.ShapeDtypeStruct((n, LANES), F32), jax.ShapeDtypeStruct((1, LANES), F32)],
        scratch_shapes=[pltpu.VMEM((1, LANES), F32)],
        compiler_params=pltpu.CompilerParams(dimension_semantics=("arbitrary",)),
        name="positions",
    )(route)


def _dispatch_kernel(d1_ref, d2_ref, hn_hbm, xs_init_hbm, xs_hbm, sem):
    del xs_init_hbm
    i = pl.program_id(0)
    ts = d1_ref.shape[0]

    def row_copy(tok, dst):
        return pltpu.make_async_copy(hn_hbm.at[pl.ds(tok, 1)], xs_hbm.at[pl.ds(dst, 1)], sem)

    def issue(t, carry):
        tok = i * ts + t
        row_copy(tok, d1_ref[t]).start()
        row_copy(tok, d2_ref[t]).start()
        return carry

    lax.fori_loop(0, ts, issue, 0)

    def drain(t, carry):
        row_copy(0, 0).wait()
        row_copy(0, 0).wait()
        return carry

    lax.fori_loop(0, ts, drain, 0)


def _dispatch(dest1, dest2, hn, n_rows):
    n = hn.shape[0]
    ts = MOVE_TILE
    xs0 = jnp.zeros((n_rows, D_MODEL), F32)
    smem_blk = pl.BlockSpec((ts,), lambda i: (i,), memory_space=pltpu.SMEM)
    return pl.pallas_call(
        _dispatch_kernel,
        grid=(n // ts,),
        in_specs=[smem_blk, smem_blk,
                  pl.BlockSpec(memory_space=pl.ANY), pl.BlockSpec(memory_space=pl.ANY)],
        out_specs=pl.BlockSpec(memory_space=pl.ANY),
        out_shape=jax.ShapeDtypeStruct((n_rows, D_MODEL), F32),
        scratch_shapes=[pltpu.SemaphoreType.DMA(())],
        input_output_aliases={3: 0},
        compiler_params=pltpu.CompilerParams(dimension_semantics=("arbitrary",)),
        name="dispatch",
    )(dest1, dest2, hn, xs0)


def _experts_kernel(be_ref, na_ref, xs_ref, wg_ref, wu_ref, wd_ref, ys_ref):
    del be_ref
    j = pl.program_id(0)

    @pl.when(j < na_ref[0])
    def _():
        x = xs_ref[...].astype(BF16)
        g = jnp.dot(x, wg_ref[0], preferred_element_type=F32)
        u = jnp.dot(x, wu_ref[0], preferred_element_type=F32)
        hid = (g * jax.nn.sigmoid(g) * u).astype(BF16)
        ys_ref[...] = jnp.dot(hid, wd_ref[0], preferred_element_type=F32)

    @pl.when(j >= na_ref[0])
    def _():
        ys_ref[...] = jnp.zeros_like(ys_ref)


def _experts(block_e, n_active, xs, wg, wu, wd):
    n_rows = xs.shape[0]
    tmb = MOE_TILE
    nblk = n_rows // tmb
    rows = pl.BlockSpec((tmb, D_MODEL), lambda j, be, na: (j, 0))
    est = (4 * tmb * D_MODEL * 4 + 2 * 3 * D_MODEL * EXPERT_HIDDEN * 2
           + 4 * tmb * EXPERT_HIDDEN * 4 + tmb * D_MODEL * 4)
    return pl.pallas_call(
        _experts_kernel,
        grid_spec=pltpu.PrefetchScalarGridSpec(
            num_scalar_prefetch=2,
            grid=(nblk,),
            in_specs=[
                rows,
                pl.BlockSpec((1, D_MODEL, EXPERT_HIDDEN), lambda j, be, na: (be[j], 0, 0)),
                pl.BlockSpec((1, D_MODEL, EXPERT_HIDDEN), lambda j, be, na: (be[j], 0, 0)),
                pl.BlockSpec((1, EXPERT_HIDDEN, D_MODEL), lambda j, be, na: (be[j], 0, 0)),
            ],
            out_specs=rows,
        ),
        out_shape=jax.ShapeDtypeStruct((n_rows, D_MODEL), F32),
        compiler_params=pltpu.CompilerParams(
            dimension_semantics=("arbitrary",), vmem_limit_bytes=_vmem_limit(est)),
        name="experts",
    )(block_e, n_active, xs, wg, wu, wd)


def _combine_kernel(d1_ref, d2_ref, route_ref, x_ref, g_ref, ys_hbm, o_ref, buf, sem, *, final_norm):
    tc = d1_ref.shape[0]

    def row_copy(src, slot):
        return pltpu.make_async_copy(ys_hbm.at[pl.ds(src, 1)], buf.at[pl.ds(slot, 1)], sem)

    def issue(t, carry):
        row_copy(d1_ref[t], t).start()
        row_copy(d2_ref[t], tc + t).start()
        return carry

    lax.fori_loop(0, tc, issue, 0)

    def drain(t, carry):
        row_copy(0, 0).wait()
        row_copy(0, 0).wait()
        return carry

    lax.fori_loop(0, tc, drain, 0)

    r = route_ref[...]
    w1 = r[:, ROUTE_W1:ROUTE_W1 + 1]
    w2 = r[:, ROUTE_W2:ROUTE_W2 + 1]
    out = x_ref[...] + w1 * buf[0:tc, :] + w2 * buf[tc:2 * tc, :]
    if final_norm:
        ms = jnp.mean(out * out, axis=-1, keepdims=True)
        out = out * lax.rsqrt(ms + RMS_EPS) * g_ref[...]
    o_ref[...] = out


def _combine(dest1, dest2, route, x1, g, ys, final_norm):
    n = x1.shape[0]
    tc = MOVE_TILE
    smem_blk = pl.BlockSpec((tc,), lambda i: (i,), memory_space=pltpu.SMEM)
    est = 2 * tc * D_MODEL * 4 * 2 + 2 * tc * D_MODEL * 4 + 2 * tc * LANES * 4 + 4 * tc * D_MODEL * 4
    return pl.pallas_call(
        functools.partial(_combine_kernel, final_norm=final_norm),
        grid=(n // tc,),
        in_specs=[smem_blk, smem_blk,
                  pl.BlockSpec((tc, LANES), lambda i: (i, 0)),
                  pl.BlockSpec((tc, D_MODEL), lambda i: (i, 0)),
                  pl.BlockSpec((1, D_MODEL), lambda i: (0, 0)),
                  pl.BlockSpec(memory_space=pl.ANY)],
        out_specs=pl.BlockSpec((tc, D_MODEL), lambda i: (i, 0)),
        out_shape=jax.ShapeDtypeStruct((n, D_MODEL), F32),
        scratch_shapes=[pltpu.VMEM((2 * tc, D_MODEL), F32), pltpu.SemaphoreType.DMA(())],
        compiler_params=pltpu.CompilerParams(
            dimension_semantics=("arbitrary",), vmem_limit_bytes=_vmem_limit(est)),
        name="combine",
    )(dest1, dest2, route, x1, g, ys)


def _plan_rows(route, pos, counts, n_rows):
    counts = counts[0, :N_EXPERTS].astype(jnp.int32)
    padded = (counts + MOE_TILE - 1) // MOE_TILE * MOE_TILE
    pad_end = jnp.cumsum(padded)
    pad_start = pad_end - padded
    e1 = route[:, ROUTE_E1].astype(jnp.int32)
    e2 = route[:, ROUTE_E2].astype(jnp.int32)
    dest1 = pad_start[e1] + pos[:, ROUTE_E1].astype(jnp.int32)
    dest2 = pad_start[e2] + pos[:, ROUTE_E2].astype(jnp.int32)
    nblk = n_rows // MOE_TILE
    block_e = jnp.minimum(
        jnp.searchsorted(pad_end, jnp.arange(nblk, dtype=jnp.int32) * MOE_TILE, side='right'),
        N_EXPERTS - 1).astype(jnp.int32)
    n_active = (pad_end[-1:] // MOE_TILE).astype(jnp.int32)
    return dest1, dest2, block_e, n_active


def _forward(x, w_in, diff_lambda, diff_subln, win_sink, w_branch_diff, w_branch_win, w_out,
             rel_bias, norm_mix, norm_ffn, w_router_group, b_router_group, w_router_expert,
             b_router_expert, w_exp_gate, w_exp_up, w_exp_down, norm_final):
    batch, seq, _ = x.shape
    n = batch * seq
    depth = w_in.shape[0]
    assert seq % DIFF_TK == 0 and seq % DIFF_TQ == 0 and seq >= WIN_BAND
    assert n % ROW_TILE == 0 and n % POS_TILE == 0 and n % MOVE_TILE == 0

    diff_strip = _diff_bias_strip(rel_bias[:, :DIFF_HEADS], seq, DIFF_TQ)
    win_strip = _win_bias_strip(rel_bias[:, DIFF_HEADS:])
    n_rows = -(-(2 * n + N_EXPERTS * (MOE_TILE - 1)) // MOE_TILE) * MOE_TILE

    x2 = x.reshape(n, D_MODEL)
    for l in range(depth):
        splits = (512, 1024, 1536, 2048, 2176, 2304)
        wdq, wdk, wdv, wwq, wwk, wwv, wgt = jnp.split(w_in[l], splits, axis=-1)
        dup = lambda w: jnp.concatenate(
            [w[:, :WIN_HEAD_DIM], w[:, :WIN_HEAD_DIM], w[:, WIN_HEAD_DIM:], w[:, WIN_HEAD_DIM:]],
            axis=-1)
        w_all = jnp.concatenate([wdq, wdk, wdv, wwq, dup(wwk), dup(wwv), wgt], axis=-1).astype(BF16)

        dq, dk, dv, wq, wk, wv, gates = _inproj(x2, norm_mix[l][None], w_all)

        lambda_init = 0.8 - 0.6 * math.exp(-0.3 * l)
        o_diff = _diff_attention(dq, dk, dv, diff_lambda[l], diff_strip, diff_subln[l][None],
                                 batch, seq, lambda_init)
        o_win = _win_attention(wq, wk, wv, win_sink[l], win_strip, batch, seq)

        wr = jnp.zeros((D_MODEL, LANES), F32)
        wr = wr.at[:, :N_GROUPS].set(w_router_group[l])
        wr = wr.at[:, N_GROUPS:N_GROUPS + N_EXPERTS].set(w_router_expert[l])
        br = jnp.zeros((1, LANES), F32)
        br = br.at[0, :N_GROUPS].set(b_router_group[l])
        br = br.at[0, N_GROUPS:N_GROUPS + N_EXPERTS].set(b_router_expert[l])
        x1, hn, route = _merge(x2, o_diff, o_win, gates,
                               w_branch_diff[l].astype(BF16), w_branch_win[l].astype(BF16),
                               w_out[l].astype(BF16), norm_ffn[l][None], wr, br)

        pos, counts = _positions(route)
        dest1, dest2, block_e, n_active = _plan_rows(route, pos, counts, n_rows)
        xs = _dispatch(dest1, dest2, hn, n_rows)
        ys = _experts(block_e, n_active, xs, w_exp_gate[l].astype(BF16), w_exp_up[l].astype(BF16),
                      w_exp_down[l].astype(BF16))
        x2 = _combine(dest1, dest2, route, x1, norm_final[None], ys, final_norm=(l == depth - 1))
    return x2.reshape(batch, seq, D_MODEL)


def kernel(x, w_in, diff_lambda, diff_subln, win_sink, w_branch_diff, w_branch_win, w_out, rel_bias, norm_mix, norm_ffn, w_router_group, b_router_group, w_router_expert, b_router_expert, w_exp_gate, w_exp_up, w_exp_down, norm_final):
    return _forward(x, w_in, diff_lambda, diff_subln, win_sink, w_branch_diff, w_branch_win, w_out,
                    rel_bias, norm_mix, norm_ffn, w_router_group, b_router_group, w_router_expert,
                    b_router_expert, w_exp_gate, w_exp_up, w_exp_down, norm_final)
```

```python
import functools
import math

import jax
import jax.numpy as jnp
from jax import lax
from jax.experimental import pallas as pl
from jax.experimental.pallas import tpu as pltpu

D_MODEL = 1024
DIFF_HEADS = 4
DIFF_HEAD_DIM = 64
DIFF_V_DIM = 2 * DIFF_HEAD_DIM
DIFF_WIDTH = DIFF_HEADS * DIFF_V_DIM
WIN_HEADS = 8
WIN_KV_HEADS = 2
WIN_GROUP = WIN_HEADS // WIN_KV_HEADS
WIN_HEAD_DIM = 64
WIN_WIDTH = WIN_HEADS * WIN_HEAD_DIM
WINDOW = 128
WIN_BLOCK = 128
REL_BUCKETS = 32
REL_MAX_DIST = 128
N_GROUPS = 4
EXPERTS_PER_GROUP = 8
N_EXPERTS = N_GROUPS * EXPERTS_PER_GROUP
EXPERT_HIDDEN = 512
RMS_EPS = 1e-6
NEG_INF = -1e30

LANES = 128
SUBLANES = 8
V7X_VMEM_BYTES = 64 * 1024 * 1024

ROW_TILE = 512
DIFF_TQ = 256
DIFF_KEY_BLOCK = 128
DIFF_KEY_CHUNK = 512
DIFF_NEAR_BLOCKS = 3
DIFF_NEAR_BLOCKS_NEG = 2
LOG2E = math.log2(math.e)
POS_TILE = 256
MOE_TILE = 256
MOVE_TILE = 256
DMA_UNROLL = 8

F32 = jnp.float32
BF16 = jnp.bfloat16


def _vmem_limit(nbytes):
    return int(min(max(2 * nbytes, 16 * 1024 * 1024), V7X_VMEM_BYTES - 8 * 1024 * 1024))


def _rel_bucket(rel):
    half = REL_BUCKETS // 2
    max_exact = half // 2
    n = jnp.abs(rel)
    nf = jnp.maximum(n, max_exact).astype(jnp.float32)
    large = max_exact + (jnp.log(nf / max_exact) / math.log(REL_MAX_DIST / max_exact)
                         * (half - max_exact)).astype(jnp.int32)
    large = jnp.minimum(large, half - 1)
    return jnp.where(rel > 0, half, 0) + jnp.where(n < max_exact, n, large)


C_DQ, C_DK, C_DV, C_WQ = 512, 512, 512, 512
C_WKD, C_WVD = 2 * LANES, 2 * LANES
C_GATE = 2 * D_MODEL
IN_COLS = C_DQ + C_DK + C_DV + C_WQ + C_WKD + C_WVD + C_GATE


def _inproj_kernel(x_ref, g_ref, w_ref, dq_ref, dk_ref, dv_ref, wq_ref, wk_ref, wv_ref, gate_ref):
    x = x_ref[...]
    ms = jnp.mean(x * x, axis=-1, keepdims=True)
    h = (x * lax.rsqrt(ms + RMS_EPS) * g_ref[...]).astype(BF16)

    col = 0

    def proj(width):
        nonlocal col
        out = jnp.dot(h, w_ref[:, col:col + width], preferred_element_type=F32)
        col += width
        return out

    dq_ref[...] = (proj(C_DQ) * (DIFF_HEAD_DIM ** -0.5 * LOG2E)).astype(BF16)
    dk_ref[...] = proj(C_DK).astype(BF16)
    dv_ref[...] = proj(C_DV).astype(BF16)
    wq_ref[...] = (proj(C_WQ) * (WIN_HEAD_DIM ** -0.5)).astype(BF16)
    wk_ref[...] = proj(C_WKD).astype(BF16)
    wv_ref[...] = proj(C_WVD).astype(BF16)
    gate_ref[...] = jax.nn.sigmoid(proj(C_GATE)).astype(BF16)


def _inproj(x2, g, w):
    n = x2.shape[0]
    tm = ROW_TILE
    widths = (C_DQ, C_DK, C_DV, C_WQ, C_WKD, C_WVD, C_GATE)
    est = 2 * tm * D_MODEL * 4 + D_MODEL * IN_COLS * 2 + 2 * tm * IN_COLS * 2 + tm * IN_COLS * 4
    return pl.pallas_call(
        _inproj_kernel,
        grid=(n // tm,),
        in_specs=[
            pl.BlockSpec((tm, D_MODEL), lambda i: (i, 0)),
            pl.BlockSpec((1, D_MODEL), lambda i: (0, 0)),
            pl.BlockSpec((D_MODEL, IN_COLS), lambda i: (0, 0), pipeline_mode=pl.Buffered(1)),
        ],
        out_specs=[pl.BlockSpec((tm, c), lambda i: (i, 0)) for c in widths],
        out_shape=[jax.ShapeDtypeStruct((n, c), BF16) for c in widths],
        compiler_params=pltpu.CompilerParams(
            dimension_semantics=("parallel",), vmem_limit_bytes=_vmem_limit(est)),
        name="inproj",
    )(x2, g, w)


def _diff_attn_kernel(lam_ref, q_ref, k_ref, v_ref, bias_ref, g_ref, o_ref, vt_ref, s_ref, p_ref, *,
                      seq, lambda_init):
    qi = pl.program_id(2)

    @pl.when(qi == 0)
    def _():
        vt_ref[...] = v_ref[...].astype(F32).T.astype(BF16)

    lp = lam_ref[...]
    lam = (jnp.exp(jnp.sum(lp[0:1] * lp[1:2], axis=-1, keepdims=True))
           - jnp.exp(jnp.sum(lp[2:3] * lp[3:4], axis=-1, keepdims=True)) + lambda_init)

    q = q_ref[...]
    tq = q.shape[0]
    qt = q.astype(F32).T
    row = lax.broadcasted_iota(jnp.int32, (LANES, 1), 0)
    qcat = jnp.concatenate([jnp.where(row < DIFF_HEAD_DIM, qt, 0.0),
                            jnp.where(row >= DIFF_HEAD_DIM, qt, 0.0)], axis=1).astype(BF16)

    kb = DIFF_KEY_BLOCK
    groups = kb // SUBLANES
    nkb = seq // kb
    per_chunk = DIFF_KEY_CHUNK // kb
    width = bias_ref.shape[-1]
    mx = jnp.full((SUBLANES, 2 * tq), NEG_INF, F32)
    for c in range(seq // DIFF_KEY_CHUNK):
        s = jnp.dot(k_ref[c * DIFF_KEY_CHUNK:(c + 1) * DIFF_KEY_CHUNK, :], qcat,
                    preferred_element_type=F32)
        for rr in range(per_chunk):
            r = c * per_chunk + rr
            start = pl.multiple_of(
                jnp.clip(DIFF_NEAR_BLOCKS * kb - r * kb + qi * tq, 0, width - tq), LANES)
            b = bias_ref[0, :, pl.ds(start, tq)]
            blk = s[rr * kb:(rr + 1) * kb, :]
            blk = jnp.concatenate([blk[:, :tq] + b, blk[:, tq:] + b], axis=1)
            mx = jnp.maximum(mx, jnp.max(blk.reshape(groups, SUBLANES, 2 * tq), axis=0))
            s_ref[r * kb:(r + 1) * kb, :] = blk
    m = jnp.max(mx, axis=0, keepdims=True)
    sm = jnp.zeros((SUBLANES, 2 * tq), F32)
    for r in range(nkb):
        p = jnp.exp2(s_ref[r * kb:(r + 1) * kb, :] - m)
        sm = sm + jnp.sum(p.reshape(groups, SUBLANES, 2 * tq), axis=0)
        p_ref[r * kb:(r + 1) * kb, :] = p.astype(BF16)
    l = jnp.sum(sm, axis=0, keepdims=True)
    vt = vt_ref[...]
    acc1 = jnp.dot(vt, p_ref[:, :tq], preferred_element_type=F32)
    acc2 = jnp.dot(vt, p_ref[:, tq:], preferred_element_type=F32)
    o = (acc1 / l[:, :tq] - lam * (acc2 / l[:, tq:])).T
    ms = jnp.mean(o * o, axis=-1, keepdims=True)
    o = o * lax.rsqrt(ms + RMS_EPS) * g_ref[...] * (1.0 - lambda_init)
    o_ref[...] = o.astype(o_ref.dtype)


def _toeplitz(t, rows, width):
    h, length = t.shape
    assert length == rows + width - 1
    flat = jnp.tile(jnp.pad(t, ((0, 0), (0, 1))), (1, rows))[:, :rows * length]
    return flat.reshape(h, rows, length)[:, :, rows - 1:]


def _diff_bias_strip(rel_diff, tq):
    kb = DIFF_KEY_BLOCK
    assert DIFF_NEAR_BLOCKS * kb - (tq - 1) > REL_MAX_DIST
    assert -DIFF_NEAR_BLOCKS_NEG * kb + kb - 1 < -REL_MAX_DIST
    width = tq + (DIFF_NEAR_BLOCKS + DIFF_NEAR_BLOCKS_NEG) * kb
    rel = jnp.arange(kb + width - 1) - (width - 1) + DIFF_NEAR_BLOCKS * kb
    t = rel_diff[_rel_bucket(rel)].T.astype(F32) * LOG2E
    return _toeplitz(t, kb, width)[:, ::-1, ::-1]


def _diff_attention(dq, dk, dv, lam_p, strip, subln_g, batch, seq, lambda_init):
    tq = DIFF_TQ
    nq = seq // tq
    width = strip.shape[-1]
    est = (2 * tq * LANES * 2 * 2 + 2 * 2 * seq * LANES * 2 + 2 * DIFF_KEY_BLOCK * width * 4
           + seq * LANES * 2 + seq * 2 * tq * (4 + 2) + seq * 2 * tq * 4)
    return pl.pallas_call(
        functools.partial(_diff_attn_kernel, seq=seq, lambda_init=lambda_init),
        grid=(batch, DIFF_HEADS, nq),
        in_specs=[
            pl.BlockSpec((4, DIFF_HEAD_DIM), lambda b, h, i: (0, 0)),
            pl.BlockSpec((tq, LANES), lambda b, h, i: (b * nq + i, h)),
            pl.BlockSpec((seq, LANES), lambda b, h, i: (b, h)),
            pl.BlockSpec((seq, LANES), lambda b, h, i: (b, h)),
            pl.BlockSpec((1, DIFF_KEY_BLOCK, width), lambda b, h, i: (h, 0, 0)),
            pl.BlockSpec((1, DIFF_V_DIM), lambda b, h, i: (0, 0)),
        ],
        out_specs=pl.BlockSpec((tq, LANES), lambda b, h, i: (b * nq + i, h)),
        out_shape=jax.ShapeDtypeStruct((batch * seq, DIFF_WIDTH), BF16),
        scratch_shapes=[pltpu.VMEM((DIFF_V_DIM, seq), BF16),
                        pltpu.VMEM((seq, 2 * tq), F32),
                        pltpu.VMEM((seq, 2 * tq), BF16)],
        compiler_params=pltpu.CompilerParams(
            dimension_semantics=("parallel", "parallel", "arbitrary"),
            vmem_limit_bytes=_vmem_limit(est)),
        name="diff_attn",
    )(lam_p, dq, dk, dv, strip, subln_g)


WIN_BAND = 3 * WIN_BLOCK
WIN_STRIP = 5 * WIN_BLOCK


def _win_attn_kernel(sink_ref, q_ref, k_ref, v_ref, strip_ref, o_ref, *, seq):
    n = pl.program_id(1)
    start = pl.multiple_of(jnp.clip(n * WIN_BLOCK - WIN_BLOCK, 0, seq - WIN_BAND), WIN_BLOCK)
    ustart = pl.multiple_of(start - n * WIN_BLOCK + 2 * WIN_BLOCK, WIN_BLOCK)
    lane = lax.broadcasted_iota(jnp.int32, (1, LANES), 1)
    low = lane < WIN_HEAD_DIM
    for kv in range(WIN_KV_HEADS):
        kb = k_ref[pl.ds(start, WIN_BAND), kv * LANES:(kv + 1) * LANES]
        vb = v_ref[pl.ds(start, WIN_BAND), kv * LANES:(kv + 1) * LANES]
        for pair in range(WIN_GROUP // 2):
            c0 = kv * WIN_GROUP * WIN_HEAD_DIM + pair * LANES
            qp = q_ref[:, c0:c0 + LANES]
            zero = jnp.zeros_like(qp)
            outs = []
            for half in range(2):
                hd = kv * WIN_GROUP + 2 * pair + half
                qm = jnp.where(low if half == 0 else jnp.logical_not(low), qp, zero)
                s = lax.dot_general(qm, kb, (((1,), (1,)), ((), ())), preferred_element_type=F32)
                s = s + strip_ref[hd, :, pl.ds(ustart, WIN_BAND)]
                sink = sink_ref[hd]
                m = jnp.maximum(jnp.max(s, axis=-1, keepdims=True), sink)
                p = jnp.exp(s - m)
                den = jnp.sum(p, axis=-1, keepdims=True) + jnp.exp(sink - m)
                outs.append(jnp.dot(p.astype(BF16), vb, preferred_element_type=F32) / den)
            o_ref[:, c0:c0 + LANES] = jnp.where(low, outs[0], outs[1]).astype(o_ref.dtype)


def _win_bias_strip(rel_win):
    rel = jnp.arange(WIN_BLOCK + WIN_STRIP - 1) - (WIN_BLOCK - 1) - 2 * WIN_BLOCK
    t = jnp.where((jnp.abs(rel) <= WINDOW)[None], rel_win[_rel_bucket(rel)].T.astype(F32), NEG_INF)
    return _toeplitz(t, WIN_BLOCK, WIN_STRIP)


def _win_attention(wq, wk, wv, sink, strip, batch, seq):
    nb = seq // WIN_BLOCK
    est = (2 * WIN_BLOCK * WIN_WIDTH * 2 * 2 + 2 * 2 * seq * 2 * LANES * 2
           + WIN_HEADS * WIN_BLOCK * WIN_STRIP * 4 * 2 + 16 * WIN_BLOCK * WIN_BAND * 4)
    return pl.pallas_call(
        functools.partial(_win_attn_kernel, seq=seq),
        grid=(batch, nb),
        in_specs=[
            pl.BlockSpec(memory_space=pltpu.SMEM),
            pl.BlockSpec((WIN_BLOCK, WIN_WIDTH), lambda b, n: (b * nb + n, 0)),
            pl.BlockSpec((seq, 2 * LANES), lambda b, n: (b, 0)),
            pl.BlockSpec((seq, 2 * LANES), lambda b, n: (b, 0)),
            pl.BlockSpec((WIN_HEADS, WIN_BLOCK, WIN_STRIP), lambda b, n: (0, 0, 0)),
        ],
        out_specs=pl.BlockSpec((WIN_BLOCK, WIN_WIDTH), lambda b, n: (b * nb + n, 0)),
        out_shape=jax.ShapeDtypeStruct((batch * seq, WIN_WIDTH), BF16),
        compiler_params=pltpu.CompilerParams(
            dimension_semantics=("parallel", "parallel"), vmem_limit_bytes=_vmem_limit(est)),
        name="win_attn",
    )(sink, wq, wk, wv, strip)


ROUTE_E1, ROUTE_E2, ROUTE_W1, ROUTE_W2 = 0, 1, 2, 3


def _merge_kernel(x_ref, od_ref, ow_ref, gate_ref, pd_ref, pw_ref, wo_ref, g_ref, wr_ref, br_ref,
                  x1_ref, hn_ref, route_ref):
    md = jnp.dot(od_ref[...], pd_ref[...], preferred_element_type=F32)
    mw = jnp.dot(ow_ref[...], pw_ref[...], preferred_element_type=F32)
    merged = (gate_ref[:, :D_MODEL].astype(F32) * md + gate_ref[:, D_MODEL:].astype(F32) * mw)
    x1 = x_ref[...] + jnp.dot(merged.astype(BF16), wo_ref[...], preferred_element_type=F32)
    x1_ref[...] = x1
    ms = jnp.mean(x1 * x1, axis=-1, keepdims=True)
    hn = x1 * lax.rsqrt(ms + RMS_EPS) * g_ref[...]
    hn_ref[...] = hn

    logits = jnp.dot(hn, wr_ref[...], preferred_element_type=F32,
                     precision=lax.Precision.HIGHEST) + br_ref[...]
    lane = lax.broadcasted_iota(jnp.int32, logits.shape, 1)
    big = jnp.int32(LANES)
    is_group = lane < N_GROUPS
    gl = jnp.where(is_group, logits, NEG_INF)
    gmax = jnp.max(gl, axis=-1, keepdims=True)
    gsel = jnp.min(jnp.where(gl == gmax, lane, big), axis=-1, keepdims=True)
    gden = jnp.sum(jnp.where(is_group, jnp.exp(gl - gmax), 0.0), axis=-1, keepdims=True)
    gw = 1.0 / gden
    lo = N_GROUPS + EXPERTS_PER_GROUP * gsel
    in_group = jnp.logical_and(lane >= lo, lane < lo + EXPERTS_PER_GROUP)
    el = jnp.where(in_group, logits, NEG_INF)
    t1 = jnp.max(el, axis=-1, keepdims=True)
    i1 = jnp.min(jnp.where(el == t1, lane, big), axis=-1, keepdims=True)
    el2 = jnp.where(lane == i1, NEG_INF, el)
    t2 = jnp.max(el2, axis=-1, keepdims=True)
    i2 = jnp.min(jnp.where(el2 == t2, lane, big), axis=-1, keepdims=True)
    z = jnp.exp(t2 - t1)
    w1 = gw / (1.0 + z)
    w2 = gw * z / (1.0 + z)
    e1 = (i1 - N_GROUPS).astype(F32)
    e2 = (i2 - N_GROUPS).astype(F32)
    route = jnp.where(lane == ROUTE_E1, e1,
                      jnp.where(lane == ROUTE_E2, e2,
                                jnp.where(lane == ROUTE_W1, w1,
                                          jnp.where(lane == ROUTE_W2, w2, 0.0))))
    route_ref[...] = route


def _merge(x2, od, ow, gates, pd, pw, wo, g, wr, br):
    n = x2.shape[0]
    tm = ROW_TILE
    est = (2 * tm * D_MODEL * 4 * 3 + 2 * tm * (DIFF_WIDTH + WIN_WIDTH + C_GATE) * 2
           + 2 * (DIFF_WIDTH + WIN_WIDTH + D_MODEL) * D_MODEL * 2 + D_MODEL * LANES * 4 * 2
           + 6 * tm * D_MODEL * 4)
    row = lambda c: pl.BlockSpec((tm, c), lambda i: (i, 0))
    full = lambda r, c: pl.BlockSpec((r, c), lambda i: (0, 0))
    return pl.pallas_call(
        _merge_kernel,
        grid=(n // tm,),
        in_specs=[row(D_MODEL), row(DIFF_WIDTH), row(WIN_WIDTH), row(C_GATE),
                  full(DIFF_WIDTH, D_MODEL), full(WIN_WIDTH, D_MODEL), full(D_MODEL, D_MODEL),
                  full(1, D_MODEL), full(D_MODEL, LANES), full(1, LANES)],
        out_specs=[row(D_MODEL), row(D_MODEL), row(LANES)],
        out_shape=[jax.ShapeDtypeStruct((n, D_MODEL), F32), jax.ShapeDtypeStruct((n, D_MODEL), F32),
                   jax.ShapeDtypeStruct((n, LANES), F32)],
        compiler_params=pltpu.CompilerParams(
            dimension_semantics=("parallel",), vmem_limit_bytes=_vmem_limit(est)),
        name="merge_route",
    )(x2, od, ow, gates, pd, pw, wo, g, wr, br)


def _positions_kernel(route_ref, pos_ref, counts_ref, carry_ref):
    i = pl.program_id(0)

    @pl.when(i == 0)
    def _():
        carry_ref[...] = jnp.zeros_like(carry_ref)

    r = route_ref[...]
    tb = r.shape[0]
    lane = lax.broadcasted_iota(jnp.int32, r.shape, 1)
    lane_f = lane.astype(F32)
    oh1 = lane_f == r[:, ROUTE_E1:ROUTE_E1 + 1]
    oh2 = lane_f == r[:, ROUTE_E2:ROUTE_E2 + 1]
    cnt = jnp.where(oh1, 1.0, 0.0) + jnp.where(oh2, 1.0, 0.0)
    rows = lax.broadcasted_iota(jnp.int32, (tb, tb), 0)
    cols = lax.broadcasted_iota(jnp.int32, (tb, tb), 1)
    tri = jnp.where(rows >= cols, 1.0, 0.0).astype(BF16)
    incl = jnp.dot(tri, cnt.astype(BF16), preferred_element_type=F32)
    excl = incl - cnt + carry_ref[...]
    rank1 = jnp.sum(jnp.where(oh1, excl, 0.0), axis=-1, keepdims=True)
    rank2 = jnp.sum(jnp.where(oh2, excl, 0.0), axis=-1, keepdims=True)
    pos_ref[...] = jnp.where(lane == ROUTE_E1, rank1, jnp.where(lane == ROUTE_E2, rank2, 0.0))
    carry_ref[...] = carry_ref[...] + incl[tb - 1:tb, :]
    counts_ref[...] = carry_ref[...]


def _positions(route):
    n = route.shape[0]
    tb = POS_TILE
    return pl.pallas_call(
        _positions_kernel,
        grid=(n // tb,),
        in_specs=[pl.BlockSpec((tb, LANES), lambda i: (i, 0))],
        out_specs=[pl.BlockSpec((tb, LANES), lambda i: (i, 0)),
                   pl.BlockSpec((1, LANES), lambda i: (0, 0))],
        out_shape=[jax.ShapeDtypeStruct((n, LANES), F32), jax.ShapeDtypeStruct((1, LANES), F32)],
        scratch_shapes=[pltpu.VMEM((1, LANES), F32)],
        compiler_params=pltpu.CompilerParams(dimension_semantics=("arbitrary",)),
        name="positions",
    )(route)


def _dispatch_kernel(d1_ref, d2_ref, hn_ref, xs_init_hbm, xs_hbm, sem):
    del xs_init_hbm
    ts = d1_ref.shape[0]

    def row_copy(t, dst):
        return pltpu.make_async_copy(hn_ref.at[pl.ds(t, 1)], xs_hbm.at[pl.ds(dst, 1)], sem)

    def issue(t, carry):
        row_copy(t, d1_ref[t]).start()
        row_copy(t, d2_ref[t]).start()
        return carry

    lax.fori_loop(0, ts, issue, 0, unroll=DMA_UNROLL)

    def drain(t, carry):
        row_copy(0, 0).wait()
        row_copy(0, 0).wait()
        return carry

    lax.fori_loop(0, ts, drain, 0, unroll=DMA_UNROLL)


def _dispatch(dest1, dest2, hn, n_rows):
    n = hn.shape[0]
    ts = MOVE_TILE
    xs0 = jnp.zeros((n_rows, D_MODEL), F32)
    smem_blk = pl.BlockSpec((ts,), lambda i: (i,), memory_space=pltpu.SMEM)
    return pl.pallas_call(
        _dispatch_kernel,
        grid=(n // ts,),
        in_specs=[smem_blk, smem_blk,
                  pl.BlockSpec((ts, D_MODEL), lambda i: (i, 0)), pl.BlockSpec(memory_space=pl.ANY)],
        out_specs=pl.BlockSpec(memory_space=pl.ANY),
        out_shape=jax.ShapeDtypeStruct((n_rows, D_MODEL), F32),
        scratch_shapes=[pltpu.SemaphoreType.DMA(())],
        input_output_aliases={3: 0},
        compiler_params=pltpu.CompilerParams(dimension_semantics=("arbitrary",)),
        name="dispatch",
    )(dest1, dest2, hn, xs0)


def _experts_kernel(be_ref, na_ref, xs_ref, wg_ref, wu_ref, wd_ref, ys_ref):
    del be_ref
    j = pl.program_id(0)

    @pl.when(j < na_ref[0])
    def _():
        x = xs_ref[...].astype(BF16)
        g = jnp.dot(x, wg_ref[0], preferred_element_type=F32)
        u = jnp.dot(x, wu_ref[0], preferred_element_type=F32)
        hid = (g * jax.nn.sigmoid(g) * u).astype(BF16)
        ys_ref[...] = jnp.dot(hid, wd_ref[0], preferred_element_type=F32)

    @pl.when(j >= na_ref[0])
    def _():
        ys_ref[...] = jnp.zeros_like(ys_ref)


def _experts(block_e, n_active, xs, wg, wu, wd):
    n_rows = xs.shape[0]
    tmb = MOE_TILE
    nblk = n_rows // tmb
    rows = pl.BlockSpec((tmb, D_MODEL), lambda j, be, na: (j, 0))
    est = (4 * tmb * D_MODEL * 4 + 2 * 3 * D_MODEL * EXPERT_HIDDEN * 2
           + 4 * tmb * EXPERT_HIDDEN * 4 + tmb * D_MODEL * 4)
    return pl.pallas_call(
        _experts_kernel,
        grid_spec=pltpu.PrefetchScalarGridSpec(
            num_scalar_prefetch=2,
            grid=(nblk,),
            in_specs=[
                rows,
                pl.BlockSpec((1, D_MODEL, EXPERT_HIDDEN), lambda j, be, na: (be[j], 0, 0)),
                pl.BlockSpec((1, D_MODEL, EXPERT_HIDDEN), lambda j, be, na: (be[j], 0, 0)),
                pl.BlockSpec((1, EXPERT_HIDDEN, D_MODEL), lambda j, be, na: (be[j], 0, 0)),
            ],
            out_specs=rows,
        ),
        out_shape=jax.ShapeDtypeStruct((n_rows, D_MODEL), F32),
        compiler_params=pltpu.CompilerParams(
            dimension_semantics=("arbitrary",), vmem_limit_bytes=_vmem_limit(est)),
        name="experts",
    )(block_e, n_active, xs, wg, wu, wd)


def _combine_kernel(d1_ref, d2_ref, route_ref, x_ref, g_ref, ys_hbm, o_ref, buf, sem, *, final_norm):
    tc = d1_ref.shape[0]

    def row_copy(src, slot):
        return pltpu.make_async_copy(ys_hbm.at[pl.ds(src, 1)], buf.at[pl.ds(slot, 1)], sem)

    def issue(t, carry):
        row_copy(d1_ref[t], t).start()
        row_copy(d2_ref[t], tc + t).start()
        return carry

    lax.fori_loop(0, tc, issue, 0, unroll=DMA_UNROLL)

    def drain(t, carry):
        row_copy(0, 0).wait()
        row_copy(0, 0).wait()
        return carry

    lax.fori_loop(0, tc, drain, 0, unroll=DMA_UNROLL)

    r = route_ref[...]
    w1 = r[:, ROUTE_W1:ROUTE_W1 + 1]
    w2 = r[:, ROUTE_W2:ROUTE_W2 + 1]
    out = x_ref[...] + w1 * buf[0:tc, :] + w2 * buf[tc:2 * tc, :]
    if final_norm:
        ms = jnp.mean(out * out, axis=-1, keepdims=True)
        out = out * lax.rsqrt(ms + RMS_EPS) * g_ref[...]
    o_ref[...] = out


def _combine(dest1, dest2, route, x1, g, ys, final_norm):
    n = x1.shape[0]
    tc = MOVE_TILE
    smem_blk = pl.BlockSpec((tc,), lambda i: (i,), memory_space=pltpu.SMEM)
    est = 2 * tc * D_MODEL * 4 * 2 + 2 * tc * D_MODEL * 4 + 2 * tc * LANES * 4 + 4 * tc * D_MODEL * 4
    return pl.pallas_call(
        functools.partial(_combine_kernel, final_norm=final_norm),
        grid=(n // tc,),
        in_specs=[smem_blk, smem_blk,
                  pl.BlockSpec((tc, LANES), lambda i: (i, 0)),
                  pl.BlockSpec((tc, D_MODEL), lambda i: (i, 0)),
                  pl.BlockSpec((1, D_MODEL), lambda i: (0, 0)),
                  pl.BlockSpec(memory_space=pl.ANY)],
        out_specs=pl.BlockSpec((tc, D_MODEL), lambda i: (i, 0)),
        out_shape=jax.ShapeDtypeStruct((n, D_MODEL), F32),
        scratch_shapes=[pltpu.VMEM((2 * tc, D_MODEL), F32), pltpu.SemaphoreType.DMA(())],
        compiler_params=pltpu.CompilerParams(
            dimension_semantics=("arbitrary",), vmem_limit_bytes=_vmem_limit(est)),
        name="combine",
    )(dest1, dest2, route, x1, g, ys)


def _plan_rows(route, pos, counts, n_rows):
    counts = counts[0, :N_EXPERTS].astype(jnp.int32)
    padded = (counts + MOE_TILE - 1) // MOE_TILE * MOE_TILE
    pad_end = jnp.cumsum(padded)
    pad_start = pad_end - padded
    e1 = route[:, ROUTE_E1].astype(jnp.int32)
    e2 = route[:, ROUTE_E2].astype(jnp.int32)
    dest1 = pad_start[e1] + pos[:, ROUTE_E1].astype(jnp.int32)
    dest2 = pad_start[e2] + pos[:, ROUTE_E2].astype(jnp.int32)
    nblk = n_rows // MOE_TILE
    block_e = jnp.minimum(
        jnp.searchsorted(pad_end, jnp.arange(nblk, dtype=jnp.int32) * MOE_TILE, side='right'),
        N_EXPERTS - 1).astype(jnp.int32)
    n_active = (pad_end[-1:] // MOE_TILE).astype(jnp.int32)
    return dest1, dest2, block_e, n_active


def _forward(x, w_in, diff_lambda, diff_subln, win_sink, w_branch_diff, w_branch_win, w_out,
             rel_bias, norm_mix, norm_ffn, w_router_group, b_router_group, w_router_expert,
             b_router_expert, w_exp_gate, w_exp_up, w_exp_down, norm_final):
    batch, seq, _ = x.shape
    n = batch * seq
    depth = w_in.shape[0]
    assert seq % DIFF_KEY_CHUNK == 0 and seq % DIFF_TQ == 0 and seq >= WIN_BAND
    assert n % ROW_TILE == 0 and n % POS_TILE == 0 and n % MOVE_TILE == 0

    diff_strip = _diff_bias_strip(rel_bias[:, :DIFF_HEADS], DIFF_TQ)
    win_strip = _win_bias_strip(rel_bias[:, DIFF_HEADS:])
    n_rows = -(-(2 * n + N_EXPERTS * (MOE_TILE - 1)) // MOE_TILE) * MOE_TILE

    x2 = x.reshape(n, D_MODEL)
    for l in range(depth):
        splits = (512, 1024, 1536, 2048, 2176, 2304)
        wdq, wdk, wdv, wwq, wwk, wwv, wgt = jnp.split(w_in[l], splits, axis=-1)
        dup = lambda w: jnp.concatenate(
            [w[:, :WIN_HEAD_DIM], w[:, :WIN_HEAD_DIM], w[:, WIN_HEAD_DIM:], w[:, WIN_HEAD_DIM:]],
            axis=-1)
        w_all = jnp.concatenate([wdq, wdk, wdv, wwq, dup(wwk), dup(wwv), wgt], axis=-1).astype(BF16)

        dq, dk, dv, wq, wk, wv, gates = _inproj(x2, norm_mix[l][None], w_all)

        lambda_init = 0.8 - 0.6 * math.exp(-0.3 * l)
        o_diff = _diff_attention(dq, dk, dv, diff_lambda[l], diff_strip, diff_subln[l][None],
                                 batch, seq, lambda_init)
        o_win = _win_attention(wq, wk, wv, win_sink[l], win_strip, batch, seq)

        wr = jnp.zeros((D_MODEL, LANES), F32)
        wr = wr.at[:, :N_GROUPS].set(w_router_group[l])
        wr = wr.at[:, N_GROUPS:N_GROUPS + N_EXPERTS].set(w_router_expert[l])
        br = jnp.zeros((1, LANES), F32)
        br = br.at[0, :N_GROUPS].set(b_router_group[l])
        br = br.at[0, N_GROUPS:N_GROUPS + N_EXPERTS].set(b_router_expert[l])
        x1, hn, route = _merge(x2, o_diff, o_win, gates,
                               w_branch_diff[l].astype(BF16), w_branch_win[l].astype(BF16),
                               w_out[l].astype(BF16), norm_ffn[l][None], wr, br)

        pos, counts = _positions(route)
        dest1, dest2, block_e, n_active = _plan_rows(route, pos, counts, n_rows)
        xs = _dispatch(dest1, dest2, hn, n_rows)
        ys = _experts(block_e, n_active, xs, w_exp_gate[l].astype(BF16), w_exp_up[l].astype(BF16),
                      w_exp_down[l].astype(BF16))
        x2 = _combine(dest1, dest2, route, x1, norm_final[None], ys, final_norm=(l == depth - 1))
    return x2.reshape(batch, seq, D_MODEL)


def kernel(x, w_in, diff_lambda, diff_subln, win_sink, w_branch_diff, w_branch_win, w_out, rel_bias, norm_mix, norm_ffn, w_router_group, b_router_group, w_router_expert, b_router_expert, w_exp_gate, w_exp_up, w_exp_down, norm_final):
    return _forward(x, w_in, diff_lambda, diff_subln, win_sink, w_branch_diff, w_branch_win, w_out,
                    rel_bias, norm_mix, norm_ffn, w_router_group, b_router_group, w_router_expert,
                    b_router_expert, w_exp_gate, w_exp_up, w_exp_down, norm_final)
```

```python
import functools
import math

import jax
import jax.numpy as jnp
from jax import lax
from jax.experimental import pallas as pl
from jax.experimental.pallas import tpu as pltpu

D_MODEL = 1024
DIFF_HEADS = 4
DIFF_HEAD_DIM = 64
DIFF_V_DIM = 2 * DIFF_HEAD_DIM
DIFF_WIDTH = DIFF_HEADS * DIFF_V_DIM
WIN_HEADS = 8
WIN_KV_HEADS = 2
WIN_GROUP = WIN_HEADS // WIN_KV_HEADS
WIN_HEAD_DIM = 64
WIN_WIDTH = WIN_HEADS * WIN_HEAD_DIM
WINDOW = 128
WIN_BLOCK = 128
REL_BUCKETS = 32
REL_MAX_DIST = 128
N_GROUPS = 4
EXPERTS_PER_GROUP = 8
N_EXPERTS = N_GROUPS * EXPERTS_PER_GROUP
EXPERT_HIDDEN = 512
RMS_EPS = 1e-6
NEG_INF = -1e30

LANES = 128
SUBLANES = 8
V7X_VMEM_BYTES = 64 * 1024 * 1024

ROW_TILE = 512
DIFF_TQ = 256
DIFF_KEY_BLOCK = 128
DIFF_KEY_CHUNK = 512
DIFF_NEAR_BLOCKS = 3
DIFF_NEAR_BLOCKS_NEG = 2
LOG2E = math.log2(math.e)
POS_TILE = 256
MOE_TILE = 256

F32 = jnp.float32
BF16 = jnp.bfloat16


def _vmem_limit(nbytes):
    return int(min(max(2 * nbytes, 16 * 1024 * 1024), V7X_VMEM_BYTES - 8 * 1024 * 1024))


def _rel_bucket(rel):
    half = REL_BUCKETS // 2
    max_exact = half // 2
    n = jnp.abs(rel)
    nf = jnp.maximum(n, max_exact).astype(jnp.float32)
    large = max_exact + (jnp.log(nf / max_exact) / math.log(REL_MAX_DIST / max_exact)
                         * (half - max_exact)).astype(jnp.int32)
    large = jnp.minimum(large, half - 1)
    return jnp.where(rel > 0, half, 0) + jnp.where(n < max_exact, n, large)


C_DQ, C_DK, C_DV, C_WQ = 512, 512, 512, 512
C_WKD, C_WVD = 2 * LANES, 2 * LANES
C_GATE = 2 * D_MODEL
IN_COLS = C_DQ + C_DK + C_DV + C_WQ + C_WKD + C_WVD + C_GATE


def _inproj_kernel(x_ref, g_ref, w_ref, dq_ref, dk_ref, dv_ref, wq_ref, wk_ref, wv_ref, gate_ref):
    x = x_ref[...]
    ms = jnp.mean(x * x, axis=-1, keepdims=True)
    h = (x * lax.rsqrt(ms + RMS_EPS) * g_ref[...]).astype(BF16)

    col = 0

    def proj(width):
        nonlocal col
        out = jnp.dot(h, w_ref[:, col:col + width], preferred_element_type=F32)
        col += width
        return out

    dq_ref[...] = (proj(C_DQ) * (DIFF_HEAD_DIM ** -0.5 * LOG2E)).astype(BF16)
    dk_ref[...] = proj(C_DK).astype(BF16)
    dv_ref[...] = proj(C_DV).astype(BF16)
    wq_ref[...] = (proj(C_WQ) * (WIN_HEAD_DIM ** -0.5)).astype(BF16)
    wk_ref[...] = proj(C_WKD).astype(BF16)
    wv_ref[...] = proj(C_WVD).astype(BF16)
    gate_ref[...] = jax.nn.sigmoid(proj(C_GATE)).astype(BF16)


def _inproj(x2, g, w):
    n = x2.shape[0]
    tm = ROW_TILE
    widths = (C_DQ, C_DK, C_DV, C_WQ, C_WKD, C_WVD, C_GATE)
    est = 2 * tm * D_MODEL * 4 + D_MODEL * IN_COLS * 2 + 2 * tm * IN_COLS * 2 + tm * IN_COLS * 4
    return pl.pallas_call(
        _inproj_kernel,
        grid=(n // tm,),
        in_specs=[
            pl.BlockSpec((tm, D_MODEL), lambda i: (i, 0)),
            pl.BlockSpec((1, D_MODEL), lambda i: (0, 0)),
            pl.BlockSpec((D_MODEL, IN_COLS), lambda i: (0, 0), pipeline_mode=pl.Buffered(1)),
        ],
        out_specs=[pl.BlockSpec((tm, c), lambda i: (i, 0)) for c in widths],
        out_shape=[jax.ShapeDtypeStruct((n, c), BF16) for c in widths],
        compiler_params=pltpu.CompilerParams(
            dimension_semantics=("parallel",), vmem_limit_bytes=_vmem_limit(est)),
        name="inproj",
    )(x2, g, w)


def _diff_attn_kernel(lam_ref, q_ref, k_ref, v_ref, bias_ref, g_ref, o_ref, vt_ref, s_ref, p_ref, *,
                      seq, lambda_init):
    qi = pl.program_id(2)

    @pl.when(qi == 0)
    def _():
        vt_ref[...] = v_ref[...].astype(F32).T.astype(BF16)

    lp = lam_ref[...]
    lam = (jnp.exp(jnp.sum(lp[0:1] * lp[1:2], axis=-1, keepdims=True))
           - jnp.exp(jnp.sum(lp[2:3] * lp[3:4], axis=-1, keepdims=True)) + lambda_init)

    q = q_ref[...]
    tq = q.shape[0]
    qt = q.astype(F32).T
    row = lax.broadcasted_iota(jnp.int32, (LANES, 1), 0)
    qcat = jnp.concatenate([jnp.where(row < DIFF_HEAD_DIM, qt, 0.0),
                            jnp.where(row >= DIFF_HEAD_DIM, qt, 0.0)], axis=1).astype(BF16)

    kb = DIFF_KEY_BLOCK
    groups = kb // SUBLANES
    nkb = seq // kb
    per_chunk = DIFF_KEY_CHUNK // kb
    width = bias_ref.shape[-1]
    mx = jnp.full((SUBLANES, 2 * tq), NEG_INF, F32)
    for c in range(seq // DIFF_KEY_CHUNK):
        s = jnp.dot(k_ref[c * DIFF_KEY_CHUNK:(c + 1) * DIFF_KEY_CHUNK, :], qcat,
                    preferred_element_type=F32)
        for rr in range(per_chunk):
            r = c * per_chunk + rr
            start = pl.multiple_of(
                jnp.clip(DIFF_NEAR_BLOCKS * kb - r * kb + qi * tq, 0, width - tq), LANES)
            b = bias_ref[0, :, pl.ds(start, tq)]
            blk = s[rr * kb:(rr + 1) * kb, :]
            blk = jnp.concatenate([blk[:, :tq] + b, blk[:, tq:] + b], axis=1)
            mx = jnp.maximum(mx, jnp.max(blk.reshape(groups, SUBLANES, 2 * tq), axis=0))
            s_ref[r * kb:(r + 1) * kb, :] = blk
    m = jnp.max(mx, axis=0, keepdims=True)
    sm = jnp.zeros((SUBLANES, 2 * tq), F32)
    for r in range(nkb):
        p = jnp.exp2(s_ref[r * kb:(r + 1) * kb, :] - m)
        sm = sm + jnp.sum(p.reshape(groups, SUBLANES, 2 * tq), axis=0)
        p_ref[r * kb:(r + 1) * kb, :] = p.astype(BF16)
    l = jnp.sum(sm, axis=0, keepdims=True)
    vt = vt_ref[...]
    acc1 = jnp.dot(vt, p_ref[:, :tq], preferred_element_type=F32)
    acc2 = jnp.dot(vt, p_ref[:, tq:], preferred_element_type=F32)
    o = (acc1 / l[:, :tq] - lam * (acc2 / l[:, tq:])).T
    ms = jnp.mean(o * o, axis=-1, keepdims=True)
    o = o * lax.rsqrt(ms + RMS_EPS) * g_ref[...] * (1.0 - lambda_init)
    o_ref[...] = o.astype(o_ref.dtype)


def _toeplitz(t, rows, width):
    h, length = t.shape
    assert length == rows + width - 1
    flat = jnp.tile(jnp.pad(t, ((0, 0), (0, 1))), (1, rows))[:, :rows * length]
    return flat.reshape(h, rows, length)[:, :, rows - 1:]


def _diff_bias_strip(rel_diff, tq):
    kb = DIFF_KEY_BLOCK
    assert DIFF_NEAR_BLOCKS * kb - (tq - 1) > REL_MAX_DIST
    assert -DIFF_NEAR_BLOCKS_NEG * kb + kb - 1 < -REL_MAX_DIST
    width = tq + (DIFF_NEAR_BLOCKS + DIFF_NEAR_BLOCKS_NEG) * kb
    rel = jnp.arange(kb + width - 1) - (width - 1) + DIFF_NEAR_BLOCKS * kb
    t = rel_diff[_rel_bucket(rel)].T.astype(F32) * LOG2E
    return _toeplitz(t, kb, width)[:, ::-1, ::-1]


def _diff_attention(dq, dk, dv, lam_p, strip, subln_g, batch, seq, lambda_init):
    tq = DIFF_TQ
    nq = seq // tq
    width = strip.shape[-1]
    est = (2 * tq * LANES * 2 * 2 + 2 * 2 * seq * LANES * 2 + 2 * DIFF_KEY_BLOCK * width * 4
           + seq * LANES * 2 + seq * 2 * tq * (4 + 2) + seq * 2 * tq * 4)
    return pl.pallas_call(
        functools.partial(_diff_attn_kernel, seq=seq, lambda_init=lambda_init),
        grid=(batch, DIFF_HEADS, nq),
        in_specs=[
            pl.BlockSpec((4, DIFF_HEAD_DIM), lambda b, h, i: (0, 0)),
            pl.BlockSpec((tq, LANES), lambda b, h, i: (b * nq + i, h)),
            pl.BlockSpec((seq, LANES), lambda b, h, i: (b, h)),
            pl.BlockSpec((seq, LANES), lambda b, h, i: (b, h)),
            pl.BlockSpec((1, DIFF_KEY_BLOCK, width), lambda b, h, i: (h, 0, 0)),
            pl.BlockSpec((1, DIFF_V_DIM), lambda b, h, i: (0, 0)),
        ],
        out_specs=pl.BlockSpec((tq, LANES), lambda b, h, i: (b * nq + i, h)),
        out_shape=jax.ShapeDtypeStruct((batch * seq, DIFF_WIDTH), BF16),
        scratch_shapes=[pltpu.VMEM((DIFF_V_DIM, seq), BF16),
                        pltpu.VMEM((seq, 2 * tq), F32),
                        pltpu.VMEM((seq, 2 * tq), BF16)],
        compiler_params=pltpu.CompilerParams(
            dimension_semantics=("parallel", "parallel", "arbitrary"),
            vmem_limit_bytes=_vmem_limit(est)),
        name="diff_attn",
    )(lam_p, dq, dk, dv, strip, subln_g)


WIN_BAND = 3 * WIN_BLOCK
WIN_STRIP = 5 * WIN_BLOCK


def _win_attn_kernel(sink_ref, q_ref, k_ref, v_ref, strip_ref, o_ref, *, seq):
    n = pl.program_id(1)
    start = pl.multiple_of(jnp.clip(n * WIN_BLOCK - WIN_BLOCK, 0, seq - WIN_BAND), WIN_BLOCK)
    ustart = pl.multiple_of(start - n * WIN_BLOCK + 2 * WIN_BLOCK, WIN_BLOCK)
    lane = lax.broadcasted_iota(jnp.int32, (1, LANES), 1)
    low = lane < WIN_HEAD_DIM
    for kv in range(WIN_KV_HEADS):
        kb = k_ref[pl.ds(start, WIN_BAND), kv * LANES:(kv + 1) * LANES]
        vb = v_ref[pl.ds(start, WIN_BAND), kv * LANES:(kv + 1) * LANES]
        for pair in range(WIN_GROUP // 2):
            c0 = kv * WIN_GROUP * WIN_HEAD_DIM + pair * LANES
            qp = q_ref[:, c0:c0 + LANES]
            zero = jnp.zeros_like(qp)
            outs = []
            for half in range(2):
                hd = kv * WIN_GROUP + 2 * pair + half
                qm = jnp.where(low if half == 0 else jnp.logical_not(low), qp, zero)
                s = lax.dot_general(qm, kb, (((1,), (1,)), ((), ())), preferred_element_type=F32)
                s = s + strip_ref[hd, :, pl.ds(ustart, WIN_BAND)]
                sink = sink_ref[hd]
                m = jnp.maximum(jnp.max(s, axis=-1, keepdims=True), sink)
                p = jnp.exp(s - m)
                den = jnp.sum(p, axis=-1, keepdims=True) + jnp.exp(sink - m)
                outs.append(jnp.dot(p.astype(BF16), vb, preferred_element_type=F32) / den)
            o_ref[:, c0:c0 + LANES] = jnp.where(low, outs[0], outs[1]).astype(o_ref.dtype)


def _win_bias_strip(rel_win):
    rel = jnp.arange(WIN_BLOCK + WIN_STRIP - 1) - (WIN_BLOCK - 1) - 2 * WIN_BLOCK
    t = jnp.where((jnp.abs(rel) <= WINDOW)[None], rel_win[_rel_bucket(rel)].T.astype(F32), NEG_INF)
    return _toeplitz(t, WIN_BLOCK, WIN_STRIP)


def _win_attention(wq, wk, wv, sink, strip, batch, seq):
    nb = seq // WIN_BLOCK
    est = (2 * WIN_BLOCK * WIN_WIDTH * 2 * 2 + 2 * 2 * seq * 2 * LANES * 2
           + WIN_HEADS * WIN_BLOCK * WIN_STRIP * 4 * 2 + 16 * WIN_BLOCK * WIN_BAND * 4)
    return pl.pallas_call(
        functools.partial(_win_attn_kernel, seq=seq),
        grid=(batch, nb),
        in_specs=[
            pl.BlockSpec(memory_space=pltpu.SMEM),
            pl.BlockSpec((WIN_BLOCK, WIN_WIDTH), lambda b, n: (b * nb + n, 0)),
            pl.BlockSpec((seq, 2 * LANES), lambda b, n: (b, 0)),
            pl.BlockSpec((seq, 2 * LANES), lambda b, n: (b, 0)),
            pl.BlockSpec((WIN_HEADS, WIN_BLOCK, WIN_STRIP), lambda b, n: (0, 0, 0)),
        ],
        out_specs=pl.BlockSpec((WIN_BLOCK, WIN_WIDTH), lambda b, n: (b * nb + n, 0)),
        out_shape=jax.ShapeDtypeStruct((batch * seq, WIN_WIDTH), BF16),
        compiler_params=pltpu.CompilerParams(
            dimension_semantics=("parallel", "parallel"), vmem_limit_bytes=_vmem_limit(est)),
        name="win_attn",
    )(sink, wq, wk, wv, strip)


ROUTE_E1, ROUTE_E2, ROUTE_W1, ROUTE_W2 = 0, 1, 2, 3


def _merge_kernel(x_ref, od_ref, ow_ref, gate_ref, pd_ref, pw_ref, wo_ref, g_ref, wr_ref, br_ref,
                  x1_ref, hn_ref, route_ref, counts_ref):
    @pl.when(pl.program_id(0) == 0)
    def _():
        counts_ref[...] = jnp.zeros_like(counts_ref)

    md = jnp.dot(od_ref[...], pd_ref[...], preferred_element_type=F32)
    mw = jnp.dot(ow_ref[...], pw_ref[...], preferred_element_type=F32)
    merged = (gate_ref[:, :D_MODEL].astype(F32) * md + gate_ref[:, D_MODEL:].astype(F32) * mw)
    x1 = x_ref[...] + jnp.dot(merged.astype(BF16), wo_ref[...], preferred_element_type=F32)
    x1_ref[...] = x1
    ms = jnp.mean(x1 * x1, axis=-1, keepdims=True)
    hn = x1 * lax.rsqrt(ms + RMS_EPS) * g_ref[...]
    hn_ref[...] = hn

    logits = jnp.dot(hn, wr_ref[...], preferred_element_type=F32,
                     precision=lax.Precision.HIGHEST) + br_ref[...]
    lane = lax.broadcasted_iota(jnp.int32, logits.shape, 1)
    big = jnp.int32(LANES)
    is_group = lane < N_GROUPS
    gl = jnp.where(is_group, logits, NEG_INF)
    gmax = jnp.max(gl, axis=-1, keepdims=True)
    gsel = jnp.min(jnp.where(gl == gmax, lane, big), axis=-1, keepdims=True)
    gden = jnp.sum(jnp.where(is_group, jnp.exp(gl - gmax), 0.0), axis=-1, keepdims=True)
    gw = 1.0 / gden
    lo = N_GROUPS + EXPERTS_PER_GROUP * gsel
    in_group = jnp.logical_and(lane >= lo, lane < lo + EXPERTS_PER_GROUP)
    el = jnp.where(in_group, logits, NEG_INF)
    t1 = jnp.max(el, axis=-1, keepdims=True)
    i1 = jnp.min(jnp.where(el == t1, lane, big), axis=-1, keepdims=True)
    el2 = jnp.where(lane == i1, NEG_INF, el)
    t2 = jnp.max(el2, axis=-1, keepdims=True)
    i2 = jnp.min(jnp.where(el2 == t2, lane, big), axis=-1, keepdims=True)
    z = jnp.exp(t2 - t1)
    w1 = gw / (1.0 + z)
    w2 = gw * z / (1.0 + z)
    e1 = (i1 - N_GROUPS).astype(F32)
    e2 = (i2 - N_GROUPS).astype(F32)
    route = jnp.where(lane == ROUTE_E1, e1,
                      jnp.where(lane == ROUTE_E2, e2,
                                jnp.where(lane == ROUTE_W1, w1,
                                          jnp.where(lane == ROUTE_W2, w2, 0.0))))
    route_ref[...] = route
    hits = (jnp.where(lane == i1 - N_GROUPS, 1.0, 0.0) + jnp.where(lane == i2 - N_GROUPS, 1.0, 0.0))
    counts_ref[...] += jnp.sum(hits, axis=0, keepdims=True)


def _merge(x2, od, ow, gates, pd, pw, wo, g, wr, br):
    n = x2.shape[0]
    tm = ROW_TILE
    est = (2 * tm * D_MODEL * 4 * 3 + 2 * tm * (DIFF_WIDTH + WIN_WIDTH + C_GATE) * 2
           + 2 * (DIFF_WIDTH + WIN_WIDTH + D_MODEL) * D_MODEL * 2 + D_MODEL * LANES * 4 * 2
           + 6 * tm * D_MODEL * 4)
    row = lambda c: pl.BlockSpec((tm, c), lambda i: (i, 0))
    full = lambda r, c: pl.BlockSpec((r, c), lambda i: (0, 0))
    return pl.pallas_call(
        _merge_kernel,
        grid=(n // tm,),
        in_specs=[row(D_MODEL), row(DIFF_WIDTH), row(WIN_WIDTH), row(C_GATE),
                  full(DIFF_WIDTH, D_MODEL), full(WIN_WIDTH, D_MODEL), full(D_MODEL, D_MODEL),
                  full(1, D_MODEL), full(D_MODEL, LANES), full(1, LANES)],
        out_specs=[row(D_MODEL), row(D_MODEL), row(LANES), full(1, LANES)],
        out_shape=[jax.ShapeDtypeStruct((n, D_MODEL), F32), jax.ShapeDtypeStruct((n, D_MODEL), F32),
                   jax.ShapeDtypeStruct((n, LANES), F32), jax.ShapeDtypeStruct((1, LANES), F32)],
        compiler_params=pltpu.CompilerParams(
            dimension_semantics=("arbitrary",), vmem_limit_bytes=_vmem_limit(est)),
        name="merge_route",
    )(x2, od, ow, gates, pd, pw, wo, g, wr, br)


def _positions_kernel(route_ref, pstart_ref, dest_ref, carry_ref):
    i = pl.program_id(0)

    @pl.when(i == 0)
    def _():
        carry_ref[...] = jnp.zeros_like(carry_ref)

    r = route_ref[...]
    tb = r.shape[0]
    lane = lax.broadcasted_iota(jnp.int32, r.shape, 1)
    lane_f = lane.astype(F32)
    oh1 = lane_f == r[:, ROUTE_E1:ROUTE_E1 + 1]
    oh2 = lane_f == r[:, ROUTE_E2:ROUTE_E2 + 1]
    cnt = jnp.where(oh1, 1.0, 0.0) + jnp.where(oh2, 1.0, 0.0)
    rows = lax.broadcasted_iota(jnp.int32, (tb, tb), 0)
    cols = lax.broadcasted_iota(jnp.int32, (tb, tb), 1)
    tri = jnp.where(rows >= cols, 1.0, 0.0).astype(BF16)
    incl = jnp.dot(tri, cnt.astype(BF16), preferred_element_type=F32)
    row_of = incl - cnt + carry_ref[...] + pstart_ref[...]
    dest1 = jnp.sum(jnp.where(oh1, row_of, 0.0), axis=-1, keepdims=True)
    dest2 = jnp.sum(jnp.where(oh2, row_of, 0.0), axis=-1, keepdims=True)
    dest_ref[...] = jnp.where(lane == ROUTE_E1, dest1, jnp.where(lane == ROUTE_E2, dest2, 0.0))
    carry_ref[...] = carry_ref[...] + incl[tb - 1:tb, :]


def _positions(route, pad_start):
    n = route.shape[0]
    tb = POS_TILE
    return pl.pallas_call(
        _positions_kernel,
        grid=(n // tb,),
        in_specs=[pl.BlockSpec((tb, LANES), lambda i: (i, 0)),
                  pl.BlockSpec((1, LANES), lambda i: (0, 0))],
        out_specs=pl.BlockSpec((tb, LANES), lambda i: (i, 0)),
        out_shape=jax.ShapeDtypeStruct((n, LANES), F32),
        scratch_shapes=[pltpu.VMEM((1, LANES), F32)],
        compiler_params=pltpu.CompilerParams(dimension_semantics=("arbitrary",)),
        name="positions",
    )(route, pad_start)


def _experts_kernel(be_ref, ns_ref, rin0_ref, rinb_ref, rina_next_ref, routa_ref, routb_prev_ref,
                    routb_ref, hn_hbm, wga_ref, wua_ref, wda_ref, wgb_ref, wub_ref, wdb_ref, y_hbm,
                    xa, xb, ya, yb, gsem_a, gsem_b, ssem_a, ssem_b):
    del be_ref
    s = pl.program_id(0)
    n_steps = ns_ref[0]
    tmb = xa.shape[0]

    def gather(idx_ref, buf, sem):
        for r in range(tmb):
            pltpu.make_async_copy(hn_hbm.at[pl.ds(idx_ref[r], 1)], buf.at[pl.ds(r, 1)], sem).start()

    def scatter(idx_ref, buf, sem):
        for r in range(tmb):
            pltpu.make_async_copy(buf.at[pl.ds(r, 1)], y_hbm.at[pl.ds(idx_ref[r], 1)], sem).start()

    def wait_gather(buf, sem):
        pltpu.make_async_copy(hn_hbm.at[pl.ds(0, tmb)], buf, sem).wait()

    def wait_scatter(buf, sem):
        pltpu.make_async_copy(buf, y_hbm.at[pl.ds(0, tmb)], sem).wait()

    def swiglu(x_ref, wg_ref, wu_ref, wd_ref):
        x = x_ref[...].astype(BF16)
        g = jnp.dot(x, wg_ref[0], preferred_element_type=F32)
        u = jnp.dot(x, wu_ref[0], preferred_element_type=F32)
        hid = (g * jax.nn.sigmoid(g) * u).astype(BF16)
        return jnp.dot(hid, wd_ref[0], preferred_element_type=F32)

    @pl.when(s == 0)
    def _():
        gather(rin0_ref, xa, gsem_a)
        ya[...] = jnp.zeros_like(ya)
        yb[...] = jnp.zeros_like(yb)
        n_assign = y_hbm.shape[0] - 2 * tmb
        spare_a = pltpu.make_async_copy(ya, y_hbm.at[pl.ds(n_assign, tmb)], ssem_a)
        spare_b = pltpu.make_async_copy(yb, y_hbm.at[pl.ds(n_assign + tmb, tmb)], ssem_b)
        spare_a.start()
        spare_b.start()
        spare_a.wait()
        spare_b.wait()

    @pl.when(jnp.logical_and(s > 0, s < n_steps))
    def _():
        wait_scatter(ya, ssem_a)

    @pl.when(s < n_steps)
    def _():
        wait_gather(xa, gsem_a)
        scatter(routb_prev_ref, yb, ssem_b)
        gather(rinb_ref, xb, gsem_b)
        ya[...] = swiglu(xa, wga_ref, wua_ref, wda_ref)
        wait_gather(xb, gsem_b)
        wait_scatter(yb, ssem_b)
        scatter(routa_ref, ya, ssem_a)
        gather(rina_next_ref, xa, gsem_a)
        yb[...] = swiglu(xb, wgb_ref, wub_ref, wdb_ref)

    @pl.when(s == n_steps - 1)
    def _():
        scatter(routb_ref, yb, ssem_b)
        wait_scatter(yb, ssem_b)
        wait_scatter(ya, ssem_a)
        wait_gather(xa, gsem_a)


def _experts(block_e, n_steps, row_in, row_out, row_out_prev, hn, wg, wu, wd, n_out_rows):
    tmb = MOE_TILE
    nblk = row_in.shape[0] // tmb
    assert nblk % 2 == 0
    idx = lambda f: pl.BlockSpec((tmb,), lambda s, be, ns: (f(s),), memory_space=pltpu.SMEM)
    wspec = lambda shape, blk: pl.BlockSpec(
        (1,) + shape, lambda s, be, ns: (be[2 * s + blk], 0, 0))
    up = (D_MODEL, EXPERT_HIDDEN)
    down = (EXPERT_HIDDEN, D_MODEL)
    est = 4 * tmb * D_MODEL * 4 + 2 * 2 * 3 * D_MODEL * EXPERT_HIDDEN * 2 + 6 * tmb * D_MODEL * 4
    return pl.pallas_call(
        _experts_kernel,
        grid_spec=pltpu.PrefetchScalarGridSpec(
            num_scalar_prefetch=2,
            grid=(nblk // 2,),
            in_specs=[
                idx(lambda s: 0),
                idx(lambda s: 2 * s + 1),
                idx(lambda s: jnp.minimum(2 * s + 2, nblk - 1)),
                idx(lambda s: 2 * s),
                idx(lambda s: 2 * s),
                idx(lambda s: 2 * s + 1),
                pl.BlockSpec(memory_space=pl.ANY),
                wspec(up, 0), wspec(up, 0), wspec(down, 0),
                wspec(up, 1), wspec(up, 1), wspec(down, 1),
            ],
            out_specs=pl.BlockSpec(memory_space=pl.ANY),
            scratch_shapes=[pltpu.VMEM((tmb, D_MODEL), F32)] * 4 + [pltpu.SemaphoreType.DMA(())] * 4,
        ),
        out_shape=jax.ShapeDtypeStruct((n_out_rows, D_MODEL), F32),
        compiler_params=pltpu.CompilerParams(
            dimension_semantics=("arbitrary",), vmem_limit_bytes=_vmem_limit(est)),
        name="experts",
    )(block_e, n_steps, row_in, row_in, row_in, row_out, row_out_prev, row_out, hn, wg, wu, wd,
      wg, wu, wd)


def _combine_kernel(route_ref, x_ref, y1_ref, y2_ref, g_ref, o_ref, *, final_norm):
    r = route_ref[...]
    w1 = r[:, ROUTE_W1:ROUTE_W1 + 1]
    w2 = r[:, ROUTE_W2:ROUTE_W2 + 1]
    out = x_ref[...] + w1 * y1_ref[...] + w2 * y2_ref[...]
    if final_norm:
        ms = jnp.mean(out * out, axis=-1, keepdims=True)
        out = out * lax.rsqrt(ms + RMS_EPS) * g_ref[...]
    o_ref[...] = out


def _combine(route, x1, y, g, final_norm):
    n = x1.shape[0]
    tc = ROW_TILE
    nt = n // tc
    est = 2 * 4 * tc * D_MODEL * 4 + 2 * tc * LANES * 4 + 3 * tc * D_MODEL * 4
    return pl.pallas_call(
        functools.partial(_combine_kernel, final_norm=final_norm),
        grid=(nt,),
        in_specs=[pl.BlockSpec((tc, LANES), lambda i: (i, 0)),
                  pl.BlockSpec((tc, D_MODEL), lambda i: (i, 0)),
                  pl.BlockSpec((tc, D_MODEL), lambda i: (i, 0)),
                  pl.BlockSpec((tc, D_MODEL), lambda i: (nt + i, 0)),
                  pl.BlockSpec((1, D_MODEL), lambda i: (0, 0))],
        out_specs=pl.BlockSpec((tc, D_MODEL), lambda i: (i, 0)),
        out_shape=jax.ShapeDtypeStruct((n, D_MODEL), F32),
        compiler_params=pltpu.CompilerParams(
            dimension_semantics=("parallel",), vmem_limit_bytes=_vmem_limit(est)),
        name="combine",
    )(route, x1, y, y, g)


def _plan_blocks(counts, n_rows):
    counts = counts[0, :N_EXPERTS].astype(jnp.int32)
    padded = (counts + MOE_TILE - 1) // MOE_TILE * MOE_TILE
    pad_end = jnp.cumsum(padded)
    pad_start = pad_end - padded
    blk_start = jnp.arange(n_rows // MOE_TILE, dtype=jnp.int32) * MOE_TILE
    block_e = jnp.minimum(jnp.sum((pad_end[None, :] <= blk_start[:, None]).astype(jnp.int32), axis=1),
                          N_EXPERTS - 1)
    n_steps = (pad_end[-1:] // MOE_TILE + 1) // 2
    pad_start_lanes = jnp.zeros((1, LANES), F32).at[0, :N_EXPERTS].set(pad_start.astype(F32))
    return pad_start_lanes, block_e, n_steps.astype(jnp.int32)


def _plan_rows(dest, n, n_rows):
    n_assign = 2 * n
    dest_flat = jnp.concatenate([dest[:, ROUTE_E1], dest[:, ROUTE_E2]]).astype(jnp.int32)
    row_assign = jnp.full((n_rows,), -1, jnp.int32).at[dest_flat].set(
        jnp.arange(n_assign, dtype=jnp.int32), unique_indices=True)
    rows = jnp.arange(n_rows, dtype=jnp.int32)
    valid = row_assign >= 0
    row_in = jnp.where(valid, row_assign % n, 0)
    row_out = jnp.where(valid, row_assign, n_assign + rows % (2 * MOE_TILE))
    spare_b = n_assign + MOE_TILE + jnp.arange(MOE_TILE, dtype=jnp.int32)
    row_out_prev = jnp.concatenate([spare_b, row_out[:-MOE_TILE]])
    return row_in, row_out, row_out_prev


def _forward(x, w_in, diff_lambda, diff_subln, win_sink, w_branch_diff, w_branch_win, w_out,
             rel_bias, norm_mix, norm_ffn, w_router_group, b_router_group, w_router_expert,
             b_router_expert, w_exp_gate, w_exp_up, w_exp_down, norm_final):
    batch, seq, _ = x.shape
    n = batch * seq
    depth = w_in.shape[0]
    assert seq % DIFF_KEY_CHUNK == 0 and seq % DIFF_TQ == 0 and seq >= WIN_BAND
    assert n % ROW_TILE == 0 and n % POS_TILE == 0

    diff_strip = _diff_bias_strip(rel_bias[:, :DIFF_HEADS], DIFF_TQ)
    win_strip = _win_bias_strip(rel_bias[:, DIFF_HEADS:])
    n_rows = -(-(2 * n + N_EXPERTS * (MOE_TILE - 1)) // (2 * MOE_TILE)) * (2 * MOE_TILE)
    n_out_rows = 2 * n + 2 * MOE_TILE

    x2 = x.reshape(n, D_MODEL)
    for l in range(depth):
        splits = (512, 1024, 1536, 2048, 2176, 2304)
        wdq, wdk, wdv, wwq, wwk, wwv, wgt = jnp.split(w_in[l], splits, axis=-1)
        dup = lambda w: jnp.concatenate(
            [w[:, :WIN_HEAD_DIM], w[:, :WIN_HEAD_DIM], w[:, WIN_HEAD_DIM:], w[:, WIN_HEAD_DIM:]],
            axis=-1)
        w_all = jnp.concatenate([wdq, wdk, wdv, wwq, dup(wwk), dup(wwv), wgt], axis=-1).astype(BF16)

        dq, dk, dv, wq, wk, wv, gates = _inproj(x2, norm_mix[l][None], w_all)

        lambda_init = 0.8 - 0.6 * math.exp(-0.3 * l)
        o_diff = _diff_attention(dq, dk, dv, diff_lambda[l], diff_strip, diff_subln[l][None],
                                 batch, seq, lambda_init)
        o_win = _win_attention(wq, wk, wv, win_sink[l], win_strip, batch, seq)

        wr = jnp.zeros((D_MODEL, LANES), F32)
        wr = wr.at[:, :N_GROUPS].set(w_router_group[l])
        wr = wr.at[:, N_GROUPS:N_GROUPS + N_EXPERTS].set(w_router_expert[l])
        br = jnp.zeros((1, LANES), F32)
        br = br.at[0, :N_GROUPS].set(b_router_group[l])
        br = br.at[0, N_GROUPS:N_GROUPS + N_EXPERTS].set(b_router_expert[l])
        x1, hn, route, counts = _merge(x2, o_diff, o_win, gates,
                                       w_branch_diff[l].astype(BF16), w_branch_win[l].astype(BF16),
                                       w_out[l].astype(BF16), norm_ffn[l][None], wr, br)

        pad_start, block_e, n_steps = _plan_blocks(counts, n_rows)
        dest = _positions(route, pad_start)
        row_in, row_out, row_out_prev = _plan_rows(dest, n, n_rows)
        y = _experts(block_e, n_steps, row_in, row_out, row_out_prev, hn,
                     w_exp_gate[l].astype(BF16), w_exp_up[l].astype(BF16),
                     w_exp_down[l].astype(BF16), n_out_rows)
        x2 = _combine(route, x1, y, norm_final[None], final_norm=(l == depth - 1))
    return x2.reshape(batch, seq, D_MODEL)


def kernel(x, w_in, diff_lambda, diff_subln, win_sink, w_branch_diff, w_branch_win, w_out, rel_bias, norm_mix, norm_ffn, w_router_group, b_router_group, w_router_expert, b_router_expert, w_exp_gate, w_exp_up, w_exp_down, norm_final):
    return _forward(x, w_in, diff_lambda, diff_subln, win_sink, w_branch_diff, w_branch_win, w_out,
                    rel_bias, norm_mix, norm_ffn, w_router_group, b_router_group, w_router_expert,
                    b_router_expert, w_exp_gate, w_exp_up, w_exp_down, norm_final)
```

```python
import functools
import math

import jax
import jax.numpy as jnp
from jax import lax
from jax.experimental import pallas as pl
from jax.experimental.pallas import tpu as pltpu

D_MODEL = 1024
DIFF_HEADS = 4
DIFF_HEAD_DIM = 64
DIFF_V_DIM = 2 * DIFF_HEAD_DIM
DIFF_WIDTH = DIFF_HEADS * DIFF_V_DIM
WIN_HEADS = 8
WIN_KV_HEADS = 2
WIN_GROUP = WIN_HEADS // WIN_KV_HEADS
WIN_HEAD_DIM = 64
WIN_WIDTH = WIN_HEADS * WIN_HEAD_DIM
WINDOW = 128
WIN_BLOCK = 128
REL_BUCKETS = 32
REL_MAX_DIST = 128
N_GROUPS = 4
EXPERTS_PER_GROUP = 8
N_EXPERTS = N_GROUPS * EXPERTS_PER_GROUP
EXPERT_HIDDEN = 512
RMS_EPS = 1e-6
NEG_INF = -1e30

LANES = 128
SUBLANES = 8
V7X_VMEM_BYTES = 64 * 1024 * 1024

ROW_TILE = 512
DIFF_TQ = 256
DIFF_TILES_PER_STEP = 4
DIFF_KEY_BLOCK = 128
DIFF_KEY_CHUNK = 512
DIFF_NEAR_BLOCKS = 3
DIFF_NEAR_BLOCKS_NEG = 2
LOG2E = math.log2(math.e)
POS_TILE = 256
MOE_TILE = 256

F32 = jnp.float32
BF16 = jnp.bfloat16


def _vmem_limit(nbytes):
    return int(min(max(2 * nbytes, 16 * 1024 * 1024), V7X_VMEM_BYTES - 8 * 1024 * 1024))


def _rel_bucket(rel):
    half = REL_BUCKETS // 2
    max_exact = half // 2
    n = jnp.abs(rel)
    nf = jnp.maximum(n, max_exact).astype(jnp.float32)
    large = max_exact + (jnp.log(nf / max_exact) / math.log(REL_MAX_DIST / max_exact)
                         * (half - max_exact)).astype(jnp.int32)
    large = jnp.minimum(large, half - 1)
    return jnp.where(rel > 0, half, 0) + jnp.where(n < max_exact, n, large)


C_DQ, C_DK, C_DV, C_WQ = 512, 512, 512, 512
C_WKD, C_WVD = 2 * LANES, 2 * LANES
C_GATE = 2 * D_MODEL
IN_COLS = C_DQ + C_DK + C_DV + C_WQ + C_WKD + C_WVD + C_GATE


def _inproj_kernel(x_ref, g_ref, w_ref, dq_ref, dk_ref, dv_ref, wq_ref, wk_ref, wv_ref, gate_ref):
    x = x_ref[...]
    ms = jnp.mean(x * x, axis=-1, keepdims=True)
    h = (x * lax.rsqrt(ms + RMS_EPS) * g_ref[...]).astype(BF16)

    col = 0

    def proj(width):
        nonlocal col
        out = jnp.dot(h, w_ref[:, col:col + width], preferred_element_type=F32)
        col += width
        return out

    dq_ref[...] = (proj(C_DQ) * (DIFF_HEAD_DIM ** -0.5 * LOG2E)).astype(BF16)
    dk_ref[...] = proj(C_DK).astype(BF16)
    dv_ref[...] = proj(C_DV).astype(BF16)
    wq_ref[...] = (proj(C_WQ) * (WIN_HEAD_DIM ** -0.5 * LOG2E)).astype(BF16)
    wk_ref[...] = proj(C_WKD).astype(BF16)
    wv_ref[...] = proj(C_WVD).astype(BF16)
    gate_ref[...] = jax.nn.sigmoid(proj(C_GATE)).astype(BF16)


def _inproj(x2, g, w):
    n = x2.shape[0]
    tm = ROW_TILE
    widths = (C_DQ, C_DK, C_DV, C_WQ, C_WKD, C_WVD, C_GATE)
    est = 2 * tm * D_MODEL * 4 + D_MODEL * IN_COLS * 2 + 2 * tm * IN_COLS * 2 + tm * IN_COLS * 4
    return pl.pallas_call(
        _inproj_kernel,
        grid=(n // tm,),
        in_specs=[
            pl.BlockSpec((tm, D_MODEL), lambda i: (i, 0)),
            pl.BlockSpec((1, D_MODEL), lambda i: (0, 0)),
            pl.BlockSpec((D_MODEL, IN_COLS), lambda i: (0, 0), pipeline_mode=pl.Buffered(1)),
        ],
        out_specs=[pl.BlockSpec((tm, c), lambda i: (i, 0)) for c in widths],
        out_shape=[jax.ShapeDtypeStruct((n, c), BF16) for c in widths],
        compiler_params=pltpu.CompilerParams(
            dimension_semantics=("parallel",), vmem_limit_bytes=_vmem_limit(est)),
        name="inproj",
    )(x2, g, w)


def _diff_attn_kernel(lam_ref, q_ref, k_ref, v_ref, bias_ref, g_ref, o_ref, vt_ref, s_ref, p_ref, *,
                      seq, lambda_init):
    qi = pl.program_id(2)

    @pl.when(qi == 0)
    def _():
        vt_ref[...] = v_ref[...].astype(F32).T.astype(BF16)

    lp = lam_ref[...]
    lam = (jnp.exp(jnp.sum(lp[0:1] * lp[1:2], axis=-1, keepdims=True))
           - jnp.exp(jnp.sum(lp[2:3] * lp[3:4], axis=-1, keepdims=True)) + lambda_init)

    tq = DIFF_TQ
    tiles = q_ref.shape[0] // tq
    kb = DIFF_KEY_BLOCK
    groups = kb // SUBLANES
    nkb = seq // kb
    per_chunk = DIFF_KEY_CHUNK // kb
    width = bias_ref.shape[-1]
    row = lax.broadcasted_iota(jnp.int32, (LANES, 1), 0)

    def scores(t):
        qt = q_ref[t * tq:(t + 1) * tq, :].astype(F32).T
        qcat = jnp.concatenate([jnp.where(row < DIFF_HEAD_DIM, qt, 0.0),
                                jnp.where(row >= DIFF_HEAD_DIM, qt, 0.0)], axis=1).astype(BF16)
        q0 = (qi * tiles + t) * tq
        mx = jnp.full((SUBLANES, 2 * tq), NEG_INF, F32)
        for c in range(seq // DIFF_KEY_CHUNK):
            s = jnp.dot(k_ref[c * DIFF_KEY_CHUNK:(c + 1) * DIFF_KEY_CHUNK, :], qcat,
                        preferred_element_type=F32)
            for rr in range(per_chunk):
                r = c * per_chunk + rr
                start = pl.multiple_of(
                    jnp.clip(DIFF_NEAR_BLOCKS * kb - r * kb + q0, 0, width - tq), LANES)
                b = bias_ref[0, :, pl.ds(start, tq)]
                blk = s[rr * kb:(rr + 1) * kb, :]
                blk = jnp.concatenate([blk[:, :tq] + b, blk[:, tq:] + b], axis=1)
                mx = jnp.maximum(mx, jnp.max(blk.reshape(groups, SUBLANES, 2 * tq), axis=0))
                s_ref[t, r * kb:(r + 1) * kb, :] = blk
        return jnp.max(mx, axis=0, keepdims=True)

    def probs(t, m):
        sm = jnp.zeros((SUBLANES, 2 * tq), F32)
        for r in range(nkb):
            p = jnp.exp2(s_ref[t, r * kb:(r + 1) * kb, :] - m)
            sm = sm + jnp.sum(p.reshape(groups, SUBLANES, 2 * tq), axis=0)
            p_ref[t, r * kb:(r + 1) * kb, :] = p.astype(BF16)
        return jnp.sum(sm, axis=0, keepdims=True)

    def values(t, l):
        vt = vt_ref[...]
        acc1 = jnp.dot(vt, p_ref[t, :, :tq], preferred_element_type=F32)
        acc2 = jnp.dot(vt, p_ref[t, :, tq:], preferred_element_type=F32)
        o = (acc1 / l[:, :tq] - lam * (acc2 / l[:, tq:])).T
        ms = jnp.mean(o * o, axis=-1, keepdims=True)
        o = o * lax.rsqrt(ms + RMS_EPS) * g_ref[...] * (1.0 - lambda_init)
        o_ref[t * tq:(t + 1) * tq, :] = o.astype(o_ref.dtype)

    m_next = scores(0)
    for t in range(tiles):
        m = m_next
        if t + 1 < tiles:
            m_next = scores(t + 1)
        values(t, probs(t, m))


def _toeplitz(t, rows, width):
    h, length = t.shape
    assert length == rows + width - 1
    flat = jnp.tile(jnp.pad(t, ((0, 0), (0, 1))), (1, rows))[:, :rows * length]
    return flat.reshape(h, rows, length)[:, :, rows - 1:]


def _diff_bias_strip(rel_diff, tq):
    kb = DIFF_KEY_BLOCK
    assert DIFF_NEAR_BLOCKS * kb - (tq - 1) > REL_MAX_DIST
    assert -DIFF_NEAR_BLOCKS_NEG * kb + kb - 1 < -REL_MAX_DIST
    width = tq + (DIFF_NEAR_BLOCKS + DIFF_NEAR_BLOCKS_NEG) * kb
    rel = jnp.arange(kb + width - 1) - (width - 1) + DIFF_NEAR_BLOCKS * kb
    t = rel_diff[_rel_bucket(rel)].T.astype(F32) * LOG2E
    return _toeplitz(t, kb, width)[:, ::-1, ::-1]


def _diff_attention(dq, dk, dv, lam_p, strip, subln_g, batch, seq, lambda_init):
    tq = DIFF_TQ
    tiles = min(DIFF_TILES_PER_STEP, seq // tq)
    ts = tiles * tq
    nq = seq // ts
    width = strip.shape[-1]
    est = (2 * ts * LANES * 2 * 2 + 2 * 2 * seq * LANES * 2 + 2 * DIFF_KEY_BLOCK * width * 4
           + seq * LANES * 2 + tiles * seq * 2 * tq * (4 + 2) + DIFF_KEY_CHUNK * 2 * tq * 4)
    return pl.pallas_call(
        functools.partial(_diff_attn_kernel, seq=seq, lambda_init=lambda_init),
        grid=(batch, DIFF_HEADS, nq),
        in_specs=[
            pl.BlockSpec((4, DIFF_HEAD_DIM), lambda b, h, i: (0, 0)),
            pl.BlockSpec((ts, LANES), lambda b, h, i: (b * nq + i, h)),
            pl.BlockSpec((seq, LANES), lambda b, h, i: (b, h)),
            pl.BlockSpec((seq, LANES), lambda b, h, i: (b, h)),
            pl.BlockSpec((1, DIFF_KEY_BLOCK, width), lambda b, h, i: (h, 0, 0)),
            pl.BlockSpec((1, DIFF_V_DIM), lambda b, h, i: (0, 0)),
        ],
        out_specs=pl.BlockSpec((ts, LANES), lambda b, h, i: (b * nq + i, h)),
        out_shape=jax.ShapeDtypeStruct((batch * seq, DIFF_WIDTH), BF16),
        scratch_shapes=[pltpu.VMEM((DIFF_V_DIM, seq), BF16),
                        pltpu.VMEM((tiles, seq, 2 * tq), F32),
                        pltpu.VMEM((tiles, seq, 2 * tq), BF16)],
        compiler_params=pltpu.CompilerParams(
            dimension_semantics=("parallel", "parallel", "arbitrary"),
            vmem_limit_bytes=_vmem_limit(est)),
        name="diff_attn",
    )(lam_p, dq, dk, dv, strip, subln_g)


WIN_BAND = 3 * WIN_BLOCK
WIN_STRIP = 5 * WIN_BLOCK


def _win_attn_kernel(q_ref, k_ref, v_ref, strip_ref, sink_ref, o_ref, *, seq):
    n = pl.program_id(1)
    start = pl.multiple_of(jnp.clip(n * WIN_BLOCK - WIN_BLOCK, 0, seq - WIN_BAND), WIN_BLOCK)
    ustart = pl.multiple_of(start - n * WIN_BLOCK + 2 * WIN_BLOCK, WIN_BLOCK)
    row = lax.broadcasted_iota(jnp.int32, (LANES, 1), 0)
    lane = lax.broadcasted_iota(jnp.int32, (1, LANES), 1)
    for kv in range(WIN_KV_HEADS):
        kb = k_ref[pl.ds(start, WIN_BAND), kv * LANES:(kv + 1) * LANES]
        vb = v_ref[pl.ds(start, WIN_BAND), kv * LANES:(kv + 1) * LANES]
        cols = []
        for pair in range(WIN_GROUP // 2):
            c0 = kv * WIN_GROUP * WIN_HEAD_DIM + pair * LANES
            qt = q_ref[:, c0:c0 + LANES].astype(F32).T
            cols.append(jnp.where(row < WIN_HEAD_DIM, qt, 0.0))
            cols.append(jnp.where(row >= WIN_HEAD_DIM, qt, 0.0))
        qcat = jnp.concatenate(cols, axis=1).astype(BF16)
        s = jnp.dot(kb, qcat, preferred_element_type=F32)
        s = s + strip_ref[kv, pl.ds(ustart, WIN_BAND), :]
        sink = sink_ref[kv]
        m = jnp.maximum(jnp.max(s, axis=0, keepdims=True), sink)
        p = jnp.exp2(s - m)
        den = jnp.sum(p, axis=0, keepdims=True) + jnp.exp2(sink - m)
        vt = vb.astype(F32).T.astype(BF16)
        o = (jnp.dot(vt, p.astype(BF16), preferred_element_type=F32) / den).T
        for pair in range(WIN_GROUP // 2):
            c0 = kv * WIN_GROUP * WIN_HEAD_DIM + pair * LANES
            even = o[(2 * pair) * WIN_BLOCK:(2 * pair + 1) * WIN_BLOCK, :]
            odd = o[(2 * pair + 1) * WIN_BLOCK:(2 * pair + 2) * WIN_BLOCK, :]
            o_ref[:, c0:c0 + LANES] = jnp.where(lane < WIN_HEAD_DIM, even, odd).astype(o_ref.dtype)


def _win_bias_strip(rel_win):
    rel = jnp.arange(WIN_BLOCK + WIN_STRIP - 1) - (WIN_BLOCK - 1) - 2 * WIN_BLOCK
    t = jnp.where((jnp.abs(rel) <= WINDOW)[None],
                  rel_win[_rel_bucket(rel)].T.astype(F32) * LOG2E, NEG_INF)
    strip = _toeplitz(t, WIN_BLOCK, WIN_STRIP)
    strip = strip.reshape(WIN_KV_HEADS, WIN_GROUP, WIN_BLOCK, WIN_STRIP).transpose(0, 3, 1, 2)
    return strip.reshape(WIN_KV_HEADS, WIN_STRIP, WIN_GROUP * WIN_BLOCK)


def _win_attention(wq, wk, wv, sink, strip, batch, seq):
    nb = seq // WIN_BLOCK
    heads_lanes = WIN_GROUP * WIN_BLOCK
    sink_lanes = jnp.broadcast_to((sink.astype(F32) * LOG2E).reshape(WIN_KV_HEADS, WIN_GROUP, 1),
                                  (WIN_KV_HEADS, WIN_GROUP, WIN_BLOCK)).reshape(
                                      WIN_KV_HEADS, 1, heads_lanes)
    est = (2 * WIN_BLOCK * WIN_WIDTH * 2 * 2 + 2 * 2 * seq * 2 * LANES * 2
           + 2 * WIN_KV_HEADS * WIN_STRIP * heads_lanes * 4 + 8 * WIN_BAND * heads_lanes * 4)
    return pl.pallas_call(
        functools.partial(_win_attn_kernel, seq=seq),
        grid=(batch, nb),
        in_specs=[
            pl.BlockSpec((WIN_BLOCK, WIN_WIDTH), lambda b, n: (b * nb + n, 0)),
            pl.BlockSpec((seq, 2 * LANES), lambda b, n: (b, 0)),
            pl.BlockSpec((seq, 2 * LANES), lambda b, n: (b, 0)),
            pl.BlockSpec((WIN_KV_HEADS, WIN_STRIP, heads_lanes), lambda b, n: (0, 0, 0)),
            pl.BlockSpec((WIN_KV_HEADS, 1, heads_lanes), lambda b, n: (0, 0, 0)),
        ],
        out_specs=pl.BlockSpec((WIN_BLOCK, WIN_WIDTH), lambda b, n: (b * nb + n, 0)),
        out_shape=jax.ShapeDtypeStruct((batch * seq, WIN_WIDTH), BF16),
        compiler_params=pltpu.CompilerParams(
            dimension_semantics=("parallel", "parallel"), vmem_limit_bytes=_vmem_limit(est)),
        name="win_attn",
    )(wq, wk, wv, strip, sink_lanes)


ROUTE_E1, ROUTE_E2, ROUTE_W1, ROUTE_W2 = 0, 1, 2, 3


def _merge_kernel(x_ref, od_ref, ow_ref, gate_ref, pd_ref, pw_ref, wo_ref, g_ref, wr_ref, br_ref,
                  x1_ref, hn_ref, route_ref, counts_ref):
    @pl.when(pl.program_id(0) == 0)
    def _():
        counts_ref[...] = jnp.zeros_like(counts_ref)

    md = jnp.dot(od_ref[...], pd_ref[...], preferred_element_type=F32)
    mw = jnp.dot(ow_ref[...], pw_ref[...], preferred_element_type=F32)
    merged = (gate_ref[:, :D_MODEL].astype(F32) * md + gate_ref[:, D_MODEL:].astype(F32) * mw)
    x1 = x_ref[...] + jnp.dot(merged.astype(BF16), wo_ref[...], preferred_element_type=F32)
    x1_ref[...] = x1
    ms = jnp.mean(x1 * x1, axis=-1, keepdims=True)
    hn = x1 * lax.rsqrt(ms + RMS_EPS) * g_ref[...]
    hn_ref[...] = hn

    hn_hi = hn.astype(BF16)
    hn_lo = (hn - hn_hi.astype(F32)).astype(BF16)
    parts = (jnp.dot(hn_hi, wr_ref[...], preferred_element_type=F32)
             + jnp.dot(hn_lo, wr_ref[...], preferred_element_type=F32))
    logits = parts + pltpu.roll(parts, LANES // 2, axis=1) + br_ref[...]
    lane = lax.broadcasted_iota(jnp.int32, logits.shape, 1)
    big = jnp.int32(LANES)
    is_group = lane < N_GROUPS
    gl = jnp.where(is_group, logits, NEG_INF)
    gmax = jnp.max(gl, axis=-1, keepdims=True)
    gsel = jnp.min(jnp.where(gl == gmax, lane, big), axis=-1, keepdims=True)
    gden = jnp.sum(jnp.where(is_group, jnp.exp(gl - gmax), 0.0), axis=-1, keepdims=True)
    gw = 1.0 / gden
    lo = N_GROUPS + EXPERTS_PER_GROUP * gsel
    in_group = jnp.logical_and(lane >= lo, lane < lo + EXPERTS_PER_GROUP)
    el = jnp.where(in_group, logits, NEG_INF)
    t1 = jnp.max(el, axis=-1, keepdims=True)
    i1 = jnp.min(jnp.where(el == t1, lane, big), axis=-1, keepdims=True)
    el2 = jnp.where(lane == i1, NEG_INF, el)
    t2 = jnp.max(el2, axis=-1, keepdims=True)
    i2 = jnp.min(jnp.where(el2 == t2, lane, big), axis=-1, keepdims=True)
    z = jnp.exp(t2 - t1)
    w1 = gw / (1.0 + z)
    w2 = gw * z / (1.0 + z)
    e1 = (i1 - N_GROUPS).astype(F32)
    e2 = (i2 - N_GROUPS).astype(F32)
    route = jnp.where(lane == ROUTE_E1, e1,
                      jnp.where(lane == ROUTE_E2, e2,
                                jnp.where(lane == ROUTE_W1, w1,
                                          jnp.where(lane == ROUTE_W2, w2, 0.0))))
    route_ref[...] = route
    hits = (jnp.where(lane == i1 - N_GROUPS, 1.0, 0.0) + jnp.where(lane == i2 - N_GROUPS, 1.0, 0.0))
    counts_ref[...] += jnp.sum(hits, axis=0, keepdims=True)


def _merge(x2, od, ow, gates, pd, pw, wo, g, wr, br):
    n = x2.shape[0]
    tm = ROW_TILE
    est = (2 * tm * D_MODEL * 4 * 3 + 2 * tm * (DIFF_WIDTH + WIN_WIDTH + C_GATE) * 2
           + 2 * (DIFF_WIDTH + WIN_WIDTH + D_MODEL) * D_MODEL * 2 + D_MODEL * LANES * 4 * 2
           + 6 * tm * D_MODEL * 4)
    row = lambda c: pl.BlockSpec((tm, c), lambda i: (i, 0))
    full = lambda r, c: pl.BlockSpec((r, c), lambda i: (0, 0))
    return pl.pallas_call(
        _merge_kernel,
        grid=(n // tm,),
        in_specs=[row(D_MODEL), row(DIFF_WIDTH), row(WIN_WIDTH), row(C_GATE),
                  full(DIFF_WIDTH, D_MODEL), full(WIN_WIDTH, D_MODEL), full(D_MODEL, D_MODEL),
                  full(1, D_MODEL), full(D_MODEL, LANES), full(1, LANES)],
        out_specs=[row(D_MODEL), row(D_MODEL), row(LANES), full(1, LANES)],
        out_shape=[jax.ShapeDtypeStruct((n, D_MODEL), F32), jax.ShapeDtypeStruct((n, D_MODEL), F32),
                   jax.ShapeDtypeStruct((n, LANES), F32), jax.ShapeDtypeStruct((1, LANES), F32)],
        compiler_params=pltpu.CompilerParams(
            dimension_semantics=("arbitrary",), vmem_limit_bytes=_vmem_limit(est)),
        name="merge_route",
    )(x2, od, ow, gates, pd, pw, wo, g, wr, br)


def _positions_kernel(route_ref, pstart_ref, dest_ref, carry_ref):
    i = pl.program_id(0)

    @pl.when(i == 0)
    def _():
        carry_ref[...] = jnp.zeros_like(carry_ref)

    r = route_ref[...]
    tb = r.shape[0]
    lane = lax.broadcasted_iota(jnp.int32, r.shape, 1)
    lane_f = lane.astype(F32)
    oh1 = lane_f == r[:, ROUTE_E1:ROUTE_E1 + 1]
    oh2 = lane_f == r[:, ROUTE_E2:ROUTE_E2 + 1]
    cnt = jnp.where(oh1, 1.0, 0.0) + jnp.where(oh2, 1.0, 0.0)
    rows = lax.broadcasted_iota(jnp.int32, (tb, tb), 0)
    cols = lax.broadcasted_iota(jnp.int32, (tb, tb), 1)
    tri = jnp.where(rows >= cols, 1.0, 0.0).astype(BF16)
    incl = jnp.dot(tri, cnt.astype(BF16), preferred_element_type=F32)
    row_of = incl - cnt + carry_ref[...] + pstart_ref[...]
    dest1 = jnp.sum(jnp.where(oh1, row_of, 0.0), axis=-1, keepdims=True)
    dest2 = jnp.sum(jnp.where(oh2, row_of, 0.0), axis=-1, keepdims=True)
    dest_ref[...] = jnp.where(lane == ROUTE_E1, dest1, jnp.where(lane == ROUTE_E2, dest2, 0.0))
    carry_ref[...] = carry_ref[...] + incl[tb - 1:tb, :]


def _positions(route, pad_start):
    n = route.shape[0]
    tb = POS_TILE
    return pl.pallas_call(
        _positions_kernel,
        grid=(n // tb,),
        in_specs=[pl.BlockSpec((tb, LANES), lambda i: (i, 0)),
                  pl.BlockSpec((1, LANES), lambda i: (0, 0))],
        out_specs=pl.BlockSpec((tb, LANES), lambda i: (i, 0)),
        out_shape=jax.ShapeDtypeStruct((n, LANES), F32),
        scratch_shapes=[pltpu.VMEM((1, LANES), F32)],
        compiler_params=pltpu.CompilerParams(dimension_semantics=("arbitrary",)),
        name="positions",
    )(route, pad_start)


def _experts_kernel(be_ref, ns_ref, rin0_ref, rinb_ref, rina_next_ref, routa_ref, routb_prev_ref,
                    routb_ref, hn_hbm, wga_ref, wua_ref, wda_ref, wgb_ref, wub_ref, wdb_ref, y_hbm,
                    xa, xb, ya, yb, gsem_a, gsem_b, ssem_a, ssem_b):
    del be_ref
    s = pl.program_id(0)
    n_steps = ns_ref[0]
    tmb = xa.shape[0]

    def gather(idx_ref, buf, sem):
        for r in range(tmb):
            pltpu.make_async_copy(hn_hbm.at[pl.ds(idx_ref[r], 1)], buf.at[pl.ds(r, 1)],
                                  sem).start(priority=r % 2)

    def scatter(idx_ref, buf, sem):
        for r in range(tmb):
            pltpu.make_async_copy(buf.at[pl.ds(r, 1)], y_hbm.at[pl.ds(idx_ref[r], 1)],
                                  sem).start(priority=r % 2)

    def wait_gather(buf, sem):
        pltpu.make_async_copy(hn_hbm.at[pl.ds(0, tmb)], buf, sem).wait()

    def wait_scatter(buf, sem):
        pltpu.make_async_copy(buf, y_hbm.at[pl.ds(0, tmb)], sem).wait()

    def swiglu(x_ref, wg_ref, wu_ref, wd_ref):
        x = x_ref[...].astype(BF16)
        g = jnp.dot(x, wg_ref[0], preferred_element_type=F32)
        u = jnp.dot(x, wu_ref[0], preferred_element_type=F32)
        hid = (g * jax.nn.sigmoid(g) * u).astype(BF16)
        return jnp.dot(hid, wd_ref[0], preferred_element_type=F32)

    @pl.when(s == 0)
    def _():
        gather(rin0_ref, xa, gsem_a)
        ya[...] = jnp.zeros_like(ya)
        yb[...] = jnp.zeros_like(yb)
        n_assign = y_hbm.shape[0] - 2 * tmb
        spare_a = pltpu.make_async_copy(ya, y_hbm.at[pl.ds(n_assign, tmb)], ssem_a)
        spare_b = pltpu.make_async_copy(yb, y_hbm.at[pl.ds(n_assign + tmb, tmb)], ssem_b)
        spare_a.start()
        spare_b.start()
        spare_a.wait()
        spare_b.wait()

    @pl.when(jnp.logical_and(s > 0, s < n_steps))
    def _():
        wait_scatter(ya, ssem_a)

    @pl.when(s < n_steps)
    def _():
        wait_gather(xa, gsem_a)
        scatter(routb_prev_ref, yb, ssem_b)
        gather(rinb_ref, xb, gsem_b)
        ya[...] = swiglu(xa, wga_ref, wua_ref, wda_ref)
        wait_gather(xb, gsem_b)
        wait_scatter(yb, ssem_b)
        scatter(routa_ref, ya, ssem_a)
        gather(rina_next_ref, xa, gsem_a)
        yb[...] = swiglu(xb, wgb_ref, wub_ref, wdb_ref)

    @pl.when(s == n_steps - 1)
    def _():
        scatter(routb_ref, yb, ssem_b)
        wait_scatter(yb, ssem_b)
        wait_scatter(ya, ssem_a)
        wait_gather(xa, gsem_a)


def _experts(block_e, n_steps, row_in, row_out, row_out_prev, hn, wg, wu, wd, n_out_rows):
    tmb = MOE_TILE
    nblk = row_in.shape[0] // tmb
    assert nblk % 2 == 0
    idx = lambda f: pl.BlockSpec((tmb,), lambda s, be, ns: (f(s),), memory_space=pltpu.SMEM)
    wspec = lambda shape, blk: pl.BlockSpec(
        (1,) + shape, lambda s, be, ns: (be[2 * s + blk], 0, 0))
    up = (D_MODEL, EXPERT_HIDDEN)
    down = (EXPERT_HIDDEN, D_MODEL)
    est = 4 * tmb * D_MODEL * 4 + 2 * 2 * 3 * D_MODEL * EXPERT_HIDDEN * 2 + 6 * tmb * D_MODEL * 4
    return pl.pallas_call(
        _experts_kernel,
        grid_spec=pltpu.PrefetchScalarGridSpec(
            num_scalar_prefetch=2,
            grid=(nblk // 2,),
            in_specs=[
                idx(lambda s: 0),
                idx(lambda s: 2 * s + 1),
                idx(lambda s: jnp.minimum(2 * s + 2, nblk - 1)),
                idx(lambda s: 2 * s),
                idx(lambda s: 2 * s),
                idx(lambda s: 2 * s + 1),
                pl.BlockSpec(memory_space=pl.ANY),
                wspec(up, 0), wspec(up, 0), wspec(down, 0),
                wspec(up, 1), wspec(up, 1), wspec(down, 1),
            ],
            out_specs=pl.BlockSpec(memory_space=pl.ANY),
            scratch_shapes=[pltpu.VMEM((tmb, D_MODEL), F32)] * 4 + [pltpu.SemaphoreType.DMA(())] * 4,
        ),
        out_shape=jax.ShapeDtypeStruct((n_out_rows, D_MODEL), F32),
        compiler_params=pltpu.CompilerParams(
            dimension_semantics=("arbitrary",), vmem_limit_bytes=_vmem_limit(est)),
        name="experts",
    )(block_e, n_steps, row_in, row_in, row_in, row_out, row_out_prev, row_out, hn, wg, wu, wd,
      wg, wu, wd)


def _combine_kernel(route_ref, x_ref, y1_ref, y2_ref, g_ref, o_ref, *, final_norm):
    r = route_ref[...]
    w1 = r[:, ROUTE_W1:ROUTE_W1 + 1]
    w2 = r[:, ROUTE_W2:ROUTE_W2 + 1]
    out = x_ref[...] + w1 * y1_ref[...] + w2 * y2_ref[...]
    if final_norm:
        ms = jnp.mean(out * out, axis=-1, keepdims=True)
        out = out * lax.rsqrt(ms + RMS_EPS) * g_ref[...]
    o_ref[...] = out


def _combine(route, x1, y, g, final_norm):
    n = x1.shape[0]
    tc = ROW_TILE
    nt = n // tc
    est = 2 * 4 * tc * D_MODEL * 4 + 2 * tc * LANES * 4 + 3 * tc * D_MODEL * 4
    return pl.pallas_call(
        functools.partial(_combine_kernel, final_norm=final_norm),
        grid=(nt,),
        in_specs=[pl.BlockSpec((tc, LANES), lambda i: (i, 0)),
                  pl.BlockSpec((tc, D_MODEL), lambda i: (i, 0)),
                  pl.BlockSpec((tc, D_MODEL), lambda i: (i, 0)),
                  pl.BlockSpec((tc, D_MODEL), lambda i: (nt + i, 0)),
                  pl.BlockSpec((1, D_MODEL), lambda i: (0, 0))],
        out_specs=pl.BlockSpec((tc, D_MODEL), lambda i: (i, 0)),
        out_shape=jax.ShapeDtypeStruct((n, D_MODEL), F32),
        compiler_params=pltpu.CompilerParams(
            dimension_semantics=("parallel",), vmem_limit_bytes=_vmem_limit(est)),
        name="combine",
    )(route, x1, y, y, g)


def _plan_blocks(counts, n_rows):
    counts = counts[0, :N_EXPERTS].astype(jnp.int32)
    padded = (counts + MOE_TILE - 1) // MOE_TILE * MOE_TILE
    pad_end = jnp.cumsum(padded)
    pad_start = pad_end - padded
    blk_start = jnp.arange(n_rows // MOE_TILE, dtype=jnp.int32) * MOE_TILE
    block_e = jnp.minimum(jnp.sum((pad_end[None, :] <= blk_start[:, None]).astype(jnp.int32), axis=1),
                          N_EXPERTS - 1)
    n_steps = (pad_end[-1:] // MOE_TILE + 1) // 2
    pad_start_lanes = jnp.zeros((1, LANES), F32).at[0, :N_EXPERTS].set(pad_start.astype(F32))
    return pad_start_lanes, block_e, n_steps.astype(jnp.int32)


def _plan_rows(dest, n, n_rows):
    n_assign = 2 * n
    dest_flat = jnp.concatenate([dest[:, ROUTE_E1], dest[:, ROUTE_E2]]).astype(jnp.int32)
    row_assign = jnp.full((n_rows,), -1, jnp.int32).at[dest_flat].set(
        jnp.arange(n_assign, dtype=jnp.int32), unique_indices=True)
    rows = jnp.arange(n_rows, dtype=jnp.int32)
    valid = row_assign >= 0
    row_in = jnp.where(valid, row_assign % n, 0)
    row_out = jnp.where(valid, row_assign, n_assign + rows % (2 * MOE_TILE))
    spare_b = n_assign + MOE_TILE + jnp.arange(MOE_TILE, dtype=jnp.int32)
    row_out_prev = jnp.concatenate([spare_b, row_out[:-MOE_TILE]])
    return row_in, row_out, row_out_prev


def _forward(x, w_in, diff_lambda, diff_subln, win_sink, w_branch_diff, w_branch_win, w_out,
             rel_bias, norm_mix, norm_ffn, w_router_group, b_router_group, w_router_expert,
             b_router_expert, w_exp_gate, w_exp_up, w_exp_down, norm_final):
    batch, seq, _ = x.shape
    n = batch * seq
    depth = w_in.shape[0]
    assert seq % DIFF_KEY_CHUNK == 0 and seq % DIFF_TQ == 0 and seq >= WIN_BAND
    assert n % ROW_TILE == 0 and n % POS_TILE == 0

    diff_strip = _diff_bias_strip(rel_bias[:, :DIFF_HEADS], DIFF_TQ)
    win_strip = _win_bias_strip(rel_bias[:, DIFF_HEADS:])
    n_rows = -(-(2 * n + N_EXPERTS * (MOE_TILE - 1)) // (2 * MOE_TILE)) * (2 * MOE_TILE)
    n_out_rows = 2 * n + 2 * MOE_TILE

    x2 = x.reshape(n, D_MODEL)
    for l in range(depth):
        splits = (512, 1024, 1536, 2048, 2176, 2304)
        wdq, wdk, wdv, wwq, wwk, wwv, wgt = jnp.split(w_in[l], splits, axis=-1)
        dup = lambda w: jnp.concatenate(
            [w[:, :WIN_HEAD_DIM], w[:, :WIN_HEAD_DIM], w[:, WIN_HEAD_DIM:], w[:, WIN_HEAD_DIM:]],
            axis=-1)
        w_all = jnp.concatenate([wdq, wdk, wdv, wwq, dup(wwk), dup(wwv), wgt], axis=-1).astype(BF16)

        dq, dk, dv, wq, wk, wv, gates = _inproj(x2, norm_mix[l][None], w_all)

        lambda_init = 0.8 - 0.6 * math.exp(-0.3 * l)
        o_diff = _diff_attention(dq, dk, dv, diff_lambda[l], diff_strip, diff_subln[l][None],
                                 batch, seq, lambda_init)
        o_win = _win_attention(wq, wk, wv, win_sink[l], win_strip, batch, seq)

        wr = jnp.zeros((D_MODEL, LANES // 2), F32)
        wr = wr.at[:, :N_GROUPS].set(w_router_group[l])
        wr = wr.at[:, N_GROUPS:N_GROUPS + N_EXPERTS].set(w_router_expert[l])
        wr_hi = wr.astype(BF16)
        wr_lo = (wr - wr_hi.astype(F32)).astype(BF16)
        wr = jnp.concatenate([wr_hi, wr_lo], axis=1)
        br = jnp.zeros((1, LANES), F32)
        br = br.at[0, :N_GROUPS].set(b_router_group[l])
        br = br.at[0, N_GROUPS:N_GROUPS + N_EXPERTS].set(b_router_expert[l])
        x1, hn, route, counts = _merge(x2, o_diff, o_win, gates,
                                       w_branch_diff[l].astype(BF16), w_branch_win[l].astype(BF16),
                                       w_out[l].astype(BF16), norm_ffn[l][None], wr, br)

        pad_start, block_e, n_steps = _plan_blocks(counts, n_rows)
        dest = _positions(route, pad_start)
        row_in, row_out, row_out_prev = _plan_rows(dest, n, n_rows)
        y = _experts(block_e, n_steps, row_in, row_out, row_out_prev, hn,
                     w_exp_gate[l].astype(BF16), w_exp_up[l].astype(BF16),
                     w_exp_down[l].astype(BF16), n_out_rows)
        x2 = _combine(route, x1, y, norm_final[None], final_norm=(l == depth - 1))
    return x2.reshape(batch, seq, D_MODEL)


def kernel(x, w_in, diff_lambda, diff_subln, win_sink, w_branch_diff, w_branch_win, w_out, rel_bias, norm_mix, norm_ffn, w_router_group, b_router_group, w_router_expert, b_router_expert, w_exp_gate, w_exp_up, w_exp_down, norm_final):
    return _forward(x, w_in, diff_lambda, diff_subln, win_sink, w_branch_diff, w_branch_win, w_out,
                    rel_bias, norm_mix, norm_ffn, w_router_group, b_router_group, w_router_expert,
                    b_router_expert, w_exp_gate, w_exp_up, w_exp_down, norm_final)
```

```python
import functools
import math

import jax
import jax.numpy as jnp
from jax import lax
from jax.experimental import pallas as pl
from jax.experimental.pallas import tpu as pltpu

D_MODEL = 1024
DIFF_HEADS = 4
DIFF_HEAD_DIM = 64
DIFF_V_DIM = 2 * DIFF_HEAD_DIM
DIFF_WIDTH = DIFF_HEADS * DIFF_V_DIM
WIN_HEADS = 8
WIN_KV_HEADS = 2
WIN_GROUP = WIN_HEADS // WIN_KV_HEADS
WIN_HEAD_DIM = 64
WIN_WIDTH = WIN_HEADS * WIN_HEAD_DIM
WINDOW = 128
WIN_BLOCK = 128
REL_BUCKETS = 32
REL_MAX_DIST = 128
N_GROUPS = 4
EXPERTS_PER_GROUP = 8
N_EXPERTS = N_GROUPS * EXPERTS_PER_GROUP
EXPERT_HIDDEN = 512
RMS_EPS = 1e-6
NEG_INF = -1e30

LANES = 128
SUBLANES = 8
V7X_VMEM_BYTES = 64 * 1024 * 1024

ROW_TILE = 512
DIFF_TQ = 256
DIFF_TILES_PER_STEP = 4
DIFF_KEY_BLOCK = 128
DIFF_KEY_CHUNK = 512
DIFF_NEAR_BLOCKS = 3
DIFF_NEAR_BLOCKS_NEG = 2
LOG2E = math.log2(math.e)
POS_TILE = 256
MOE_TILE = 256
MOVE_TILE = 256

F32 = jnp.float32
BF16 = jnp.bfloat16


def _vmem_limit(nbytes):
    return int(min(max(2 * nbytes, 16 * 1024 * 1024), V7X_VMEM_BYTES - 8 * 1024 * 1024))


def _rel_bucket(rel):
    half = REL_BUCKETS // 2
    max_exact = half // 2
    n = jnp.abs(rel)
    nf = jnp.maximum(n, max_exact).astype(jnp.float32)
    large = max_exact + (jnp.log(nf / max_exact) / math.log(REL_MAX_DIST / max_exact)
                         * (half - max_exact)).astype(jnp.int32)
    large = jnp.minimum(large, half - 1)
    return jnp.where(rel > 0, half, 0) + jnp.where(n < max_exact, n, large)


C_DQ, C_DK, C_DV, C_WQ = 512, 512, 512, 512
C_WKD, C_WVD = 2 * LANES, 2 * LANES
C_GATE = 2 * D_MODEL
IN_COLS = C_DQ + C_DK + C_DV + C_WQ + C_WKD + C_WVD + C_GATE


def _inproj_kernel(x_ref, g_ref, w_ref, dq_ref, dk_ref, dv_ref, wq_ref, wk_ref, wv_ref, gate_ref):
    x = x_ref[...]
    ms = jnp.mean(x * x, axis=-1, keepdims=True)
    h = (x * lax.rsqrt(ms + RMS_EPS) * g_ref[...]).astype(BF16)

    col = 0

    def proj(width):
        nonlocal col
        out = jnp.dot(h, w_ref[:, col:col + width], preferred_element_type=F32)
        col += width
        return out

    dq_ref[...] = (proj(C_DQ) * (DIFF_HEAD_DIM ** -0.5 * LOG2E)).astype(BF16)
    dk_ref[...] = proj(C_DK).astype(BF16)
    dv_ref[...] = proj(C_DV).astype(BF16)
    wq_ref[...] = (proj(C_WQ) * (WIN_HEAD_DIM ** -0.5 * LOG2E)).astype(BF16)
    wk_ref[...] = proj(C_WKD).astype(BF16)
    wv_ref[...] = proj(C_WVD).astype(BF16)
    gate_ref[...] = jax.nn.sigmoid(proj(C_GATE)).astype(BF16)


def _inproj(x2, g, w):
    n = x2.shape[0]
    tm = ROW_TILE
    widths = (C_DQ, C_DK, C_DV, C_WQ, C_WKD, C_WVD, C_GATE)
    est = 2 * tm * D_MODEL * 4 + D_MODEL * IN_COLS * 2 + 2 * tm * IN_COLS * 2 + tm * IN_COLS * 4
    return pl.pallas_call(
        _inproj_kernel,
        grid=(n // tm,),
        in_specs=[
            pl.BlockSpec((tm, D_MODEL), lambda i: (i, 0)),
            pl.BlockSpec((1, D_MODEL), lambda i: (0, 0)),
            pl.BlockSpec((D_MODEL, IN_COLS), lambda i: (0, 0), pipeline_mode=pl.Buffered(1)),
        ],
        out_specs=[pl.BlockSpec((tm, c), lambda i: (i, 0)) for c in widths],
        out_shape=[jax.ShapeDtypeStruct((n, c), BF16) for c in widths],
        compiler_params=pltpu.CompilerParams(
            dimension_semantics=("parallel",), vmem_limit_bytes=_vmem_limit(est)),
        name="inproj",
    )(x2, g, w)


def _diff_attn_kernel(lam_ref, q_ref, k_ref, v_ref, bias_ref, g_ref, o_ref, vt_ref, s_ref, p_ref, *,
                      seq, lambda_init):
    qi = pl.program_id(2)

    @pl.when(qi == 0)
    def _():
        vt_ref[...] = v_ref[...].astype(F32).T.astype(BF16)

    lp = lam_ref[...]
    lam = (jnp.exp(jnp.sum(lp[0:1] * lp[1:2], axis=-1, keepdims=True))
           - jnp.exp(jnp.sum(lp[2:3] * lp[3:4], axis=-1, keepdims=True)) + lambda_init)

    tq = DIFF_TQ
    tiles = q_ref.shape[0] // tq
    kb = DIFF_KEY_BLOCK
    groups = kb // SUBLANES
    nkb = seq // kb
    per_chunk = DIFF_KEY_CHUNK // kb
    width = bias_ref.shape[-1]
    row = lax.broadcasted_iota(jnp.int32, (LANES, 1), 0)

    def scores(t):
        qt = q_ref[t * tq:(t + 1) * tq, :].astype(F32).T
        qcat = jnp.concatenate([jnp.where(row < DIFF_HEAD_DIM, qt, 0.0),
                                jnp.where(row >= DIFF_HEAD_DIM, qt, 0.0)], axis=1).astype(BF16)
        q0 = (qi * tiles + t) * tq
        mx = jnp.full((SUBLANES, 2 * tq), NEG_INF, F32)
        for c in range(seq // DIFF_KEY_CHUNK):
            s = jnp.dot(k_ref[c * DIFF_KEY_CHUNK:(c + 1) * DIFF_KEY_CHUNK, :], qcat,
                        preferred_element_type=F32)
            for rr in range(per_chunk):
                r = c * per_chunk + rr
                start = pl.multiple_of(
                    jnp.clip(DIFF_NEAR_BLOCKS * kb - r * kb + q0, 0, width - tq), LANES)
                b = bias_ref[0, :, pl.ds(start, tq)]
                blk = s[rr * kb:(rr + 1) * kb, :]
                blk = jnp.concatenate([blk[:, :tq] + b, blk[:, tq:] + b], axis=1)
                mx = jnp.maximum(mx, jnp.max(blk.reshape(groups, SUBLANES, 2 * tq), axis=0))
                s_ref[t, r * kb:(r + 1) * kb, :] = blk
        return jnp.max(mx, axis=0, keepdims=True)

    def probs(t, m):
        sm = jnp.zeros((SUBLANES, 2 * tq), F32)
        for r in range(nkb):
            p = jnp.exp2(s_ref[t, r * kb:(r + 1) * kb, :] - m)
            sm = sm + jnp.sum(p.reshape(groups, SUBLANES, 2 * tq), axis=0)
            p_ref[t, r * kb:(r + 1) * kb, :] = p.astype(BF16)
        return jnp.sum(sm, axis=0, keepdims=True)

    def values(t, l):
        vt = vt_ref[...]
        acc1 = jnp.dot(vt, p_ref[t, :, :tq], preferred_element_type=F32)
        acc2 = jnp.dot(vt, p_ref[t, :, tq:], preferred_element_type=F32)
        o = (acc1 / l[:, :tq] - lam * (acc2 / l[:, tq:])).T
        ms = jnp.mean(o * o, axis=-1, keepdims=True)
        o = o * lax.rsqrt(ms + RMS_EPS) * g_ref[...] * (1.0 - lambda_init)
        o_ref[t * tq:(t + 1) * tq, :] = o.astype(o_ref.dtype)

    m_next = scores(0)
    for t in range(tiles):
        m = m_next
        if t + 1 < tiles:
            m_next = scores(t + 1)
        values(t, probs(t, m))


def _toeplitz(t, rows, width):
    h, length = t.shape
    assert length == rows + width - 1
    flat = jnp.tile(jnp.pad(t, ((0, 0), (0, 1))), (1, rows))[:, :rows * length]
    return flat.reshape(h, rows, length)[:, :, rows - 1:]


def _diff_bias_strip(rel_diff, tq):
    kb = DIFF_KEY_BLOCK
    assert DIFF_NEAR_BLOCKS * kb - (tq - 1) > REL_MAX_DIST
    assert -DIFF_NEAR_BLOCKS_NEG * kb + kb - 1 < -REL_MAX_DIST
    width = tq + (DIFF_NEAR_BLOCKS + DIFF_NEAR_BLOCKS_NEG) * kb
    rel = jnp.arange(kb + width - 1) - (width - 1) + DIFF_NEAR_BLOCKS * kb
    t = rel_diff[_rel_bucket(rel)].T.astype(F32) * LOG2E
    return _toeplitz(t, kb, width)[:, ::-1, ::-1]


def _diff_attention(dq, dk, dv, lam_p, strip, subln_g, batch, seq, lambda_init):
    tq = DIFF_TQ
    tiles = min(DIFF_TILES_PER_STEP, seq // tq)
    ts = tiles * tq
    nq = seq // ts
    width = strip.shape[-1]
    est = (2 * ts * LANES * 2 * 2 + 2 * 2 * seq * LANES * 2 + 2 * DIFF_KEY_BLOCK * width * 4
           + seq * LANES * 2 + tiles * seq * 2 * tq * (4 + 2) + DIFF_KEY_CHUNK * 2 * tq * 4)
    return pl.pallas_call(
        functools.partial(_diff_attn_kernel, seq=seq, lambda_init=lambda_init),
        grid=(batch, DIFF_HEADS, nq),
        in_specs=[
            pl.BlockSpec((4, DIFF_HEAD_DIM), lambda b, h, i: (0, 0)),
            pl.BlockSpec((ts, LANES), lambda b, h, i: (b * nq + i, h)),
            pl.BlockSpec((seq, LANES), lambda b, h, i: (b, h)),
            pl.BlockSpec((seq, LANES), lambda b, h, i: (b, h)),
            pl.BlockSpec((1, DIFF_KEY_BLOCK, width), lambda b, h, i: (h, 0, 0)),
            pl.BlockSpec((1, DIFF_V_DIM), lambda b, h, i: (0, 0)),
        ],
        out_specs=pl.BlockSpec((ts, LANES), lambda b, h, i: (b * nq + i, h)),
        out_shape=jax.ShapeDtypeStruct((batch * seq, DIFF_WIDTH), BF16),
        scratch_shapes=[pltpu.VMEM((DIFF_V_DIM, seq), BF16),
                        pltpu.VMEM((tiles, seq, 2 * tq), F32),
                        pltpu.VMEM((tiles, seq, 2 * tq), BF16)],
        compiler_params=pltpu.CompilerParams(
            dimension_semantics=("parallel", "parallel", "arbitrary"),
            vmem_limit_bytes=_vmem_limit(est)),
        name="diff_attn",
    )(lam_p, dq, dk, dv, strip, subln_g)


WIN_BAND = 3 * WIN_BLOCK
WIN_STRIP = 5 * WIN_BLOCK


def _win_attn_kernel(q_ref, k_ref, v_ref, strip_ref, sink_ref, o_ref, *, seq):
    n = pl.program_id(1)
    start = pl.multiple_of(jnp.clip(n * WIN_BLOCK - WIN_BLOCK, 0, seq - WIN_BAND), WIN_BLOCK)
    ustart = pl.multiple_of(start - n * WIN_BLOCK + 2 * WIN_BLOCK, WIN_BLOCK)
    row = lax.broadcasted_iota(jnp.int32, (LANES, 1), 0)
    lane = lax.broadcasted_iota(jnp.int32, (1, LANES), 1)
    for kv in range(WIN_KV_HEADS):
        kb = k_ref[pl.ds(start, WIN_BAND), kv * LANES:(kv + 1) * LANES]
        vb = v_ref[pl.ds(start, WIN_BAND), kv * LANES:(kv + 1) * LANES]
        cols = []
        for pair in range(WIN_GROUP // 2):
            c0 = kv * WIN_GROUP * WIN_HEAD_DIM + pair * LANES
            qt = q_ref[:, c0:c0 + LANES].astype(F32).T
            cols.append(jnp.where(row < WIN_HEAD_DIM, qt, 0.0))
            cols.append(jnp.where(row >= WIN_HEAD_DIM, qt, 0.0))
        qcat = jnp.concatenate(cols, axis=1).astype(BF16)
        s = jnp.dot(kb, qcat, preferred_element_type=F32)
        s = s + strip_ref[kv, pl.ds(ustart, WIN_BAND), :]
        sink = sink_ref[kv]
        m = jnp.maximum(jnp.max(s, axis=0, keepdims=True), sink)
        p = jnp.exp2(s - m)
        den = jnp.sum(p, axis=0, keepdims=True) + jnp.exp2(sink - m)
        vt = vb.astype(F32).T.astype(BF16)
        o = (jnp.dot(vt, p.astype(BF16), preferred_element_type=F32) / den).T
        for pair in range(WIN_GROUP // 2):
            c0 = kv * WIN_GROUP * WIN_HEAD_DIM + pair * LANES
            even = o[(2 * pair) * WIN_BLOCK:(2 * pair + 1) * WIN_BLOCK, :]
            odd = o[(2 * pair + 1) * WIN_BLOCK:(2 * pair + 2) * WIN_BLOCK, :]
            o_ref[:, c0:c0 + LANES] = jnp.where(lane < WIN_HEAD_DIM, even, odd).astype(o_ref.dtype)


def _win_bias_strip(rel_win):
    rel = jnp.arange(WIN_BLOCK + WIN_STRIP - 1) - (WIN_BLOCK - 1) - 2 * WIN_BLOCK
    t = jnp.where((jnp.abs(rel) <= WINDOW)[None],
                  rel_win[_rel_bucket(rel)].T.astype(F32) * LOG2E, NEG_INF)
    strip = _toeplitz(t, WIN_BLOCK, WIN_STRIP)
    strip = strip.reshape(WIN_KV_HEADS, WIN_GROUP, WIN_BLOCK, WIN_STRIP).transpose(0, 3, 1, 2)
    return strip.reshape(WIN_KV_HEADS, WIN_STRIP, WIN_GROUP * WIN_BLOCK)


def _win_attention(wq, wk, wv, sink, strip, batch, seq):
    nb = seq // WIN_BLOCK
    heads_lanes = WIN_GROUP * WIN_BLOCK
    sink_lanes = jnp.broadcast_to((sink.astype(F32) * LOG2E).reshape(WIN_KV_HEADS, WIN_GROUP, 1),
                                  (WIN_KV_HEADS, WIN_GROUP, WIN_BLOCK)).reshape(
                                      WIN_KV_HEADS, 1, heads_lanes)
    est = (2 * WIN_BLOCK * WIN_WIDTH * 2 * 2 + 2 * 2 * seq * 2 * LANES * 2
           + 2 * WIN_KV_HEADS * WIN_STRIP * heads_lanes * 4 + 8 * WIN_BAND * heads_lanes * 4)
    return pl.pallas_call(
        functools.partial(_win_attn_kernel, seq=seq),
        grid=(batch, nb),
        in_specs=[
            pl.BlockSpec((WIN_BLOCK, WIN_WIDTH), lambda b, n: (b * nb + n, 0)),
            pl.BlockSpec((seq, 2 * LANES), lambda b, n: (b, 0)),
            pl.BlockSpec((seq, 2 * LANES), lambda b, n: (b, 0)),
            pl.BlockSpec((WIN_KV_HEADS, WIN_STRIP, heads_lanes), lambda b, n: (0, 0, 0)),
            pl.BlockSpec((WIN_KV_HEADS, 1, heads_lanes), lambda b, n: (0, 0, 0)),
        ],
        out_specs=pl.BlockSpec((WIN_BLOCK, WIN_WIDTH), lambda b, n: (b * nb + n, 0)),
        out_shape=jax.ShapeDtypeStruct((batch * seq, WIN_WIDTH), BF16),
        compiler_params=pltpu.CompilerParams(
            dimension_semantics=("parallel", "parallel"), vmem_limit_bytes=_vmem_limit(est)),
        name="win_attn",
    )(wq, wk, wv, strip, sink_lanes)


ROUTE_E1, ROUTE_E2, ROUTE_W1, ROUTE_W2 = 0, 1, 2, 3


def _merge_kernel(x_ref, od_ref, ow_ref, gate_ref, pd_ref, pw_ref, wo_ref, g_ref, wr_ref, br_ref,
                  x1_ref, hn_ref, route_ref, counts_ref):
    @pl.when(pl.program_id(0) == 0)
    def _():
        counts_ref[...] = jnp.zeros_like(counts_ref)

    md = jnp.dot(od_ref[...], pd_ref[...], preferred_element_type=F32)
    mw = jnp.dot(ow_ref[...], pw_ref[...], preferred_element_type=F32)
    merged = (gate_ref[:, :D_MODEL].astype(F32) * md + gate_ref[:, D_MODEL:].astype(F32) * mw)
    x1 = x_ref[...] + jnp.dot(merged.astype(BF16), wo_ref[...], preferred_element_type=F32)
    x1_ref[...] = x1
    ms = jnp.mean(x1 * x1, axis=-1, keepdims=True)
    hn = x1 * lax.rsqrt(ms + RMS_EPS) * g_ref[...]
    hn_ref[...] = hn

    hn_hi = hn.astype(BF16)
    hn_lo = (hn - hn_hi.astype(F32)).astype(BF16)
    parts = (jnp.dot(hn_hi, wr_ref[...], preferred_element_type=F32)
             + jnp.dot(hn_lo, wr_ref[...], preferred_element_type=F32))
    logits = parts + pltpu.roll(parts, LANES // 2, axis=1) + br_ref[...]
    lane = lax.broadcasted_iota(jnp.int32, logits.shape, 1)
    big = jnp.int32(LANES)
    is_group = lane < N_GROUPS
    gl = jnp.where(is_group, logits, NEG_INF)
    gmax = jnp.max(gl, axis=-1, keepdims=True)
    gsel = jnp.min(jnp.where(gl == gmax, lane, big), axis=-1, keepdims=True)
    gden = jnp.sum(jnp.where(is_group, jnp.exp(gl - gmax), 0.0), axis=-1, keepdims=True)
    gw = 1.0 / gden
    lo = N_GROUPS + EXPERTS_PER_GROUP * gsel
    in_group = jnp.logical_and(lane >= lo, lane < lo + EXPERTS_PER_GROUP)
    el = jnp.where(in_group, logits, NEG_INF)
    t1 = jnp.max(el, axis=-1, keepdims=True)
    i1 = jnp.min(jnp.where(el == t1, lane, big), axis=-1, keepdims=True)
    el2 = jnp.where(lane == i1, NEG_INF, el)
    t2 = jnp.max(el2, axis=-1, keepdims=True)
    i2 = jnp.min(jnp.where(el2 == t2, lane, big), axis=-1, keepdims=True)
    z = jnp.exp(t2 - t1)
    w1 = gw / (1.0 + z)
    w2 = gw * z / (1.0 + z)
    e1 = (i1 - N_GROUPS).astype(F32)
    e2 = (i2 - N_GROUPS).astype(F32)
    route = jnp.where(lane == ROUTE_E1, e1,
                      jnp.where(lane == ROUTE_E2, e2,
                                jnp.where(lane == ROUTE_W1, w1,
                                          jnp.where(lane == ROUTE_W2, w2, 0.0))))
    route_ref[...] = route
    hits = (jnp.where(lane == i1 - N_GROUPS, 1.0, 0.0) + jnp.where(lane == i2 - N_GROUPS, 1.0, 0.0))
    counts_ref[...] += jnp.sum(hits, axis=0, keepdims=True)


def _merge(x2, od, ow, gates, pd, pw, wo, g, wr, br):
    n = x2.shape[0]
    tm = ROW_TILE
    est = (2 * tm * D_MODEL * 4 * 3 + 2 * tm * (DIFF_WIDTH + WIN_WIDTH + C_GATE) * 2
           + 2 * (DIFF_WIDTH + WIN_WIDTH + D_MODEL) * D_MODEL * 2 + D_MODEL * LANES * 4 * 2
           + 6 * tm * D_MODEL * 4)
    row = lambda c: pl.BlockSpec((tm, c), lambda i: (i, 0))
    full = lambda r, c: pl.BlockSpec((r, c), lambda i: (0, 0))
    return pl.pallas_call(
        _merge_kernel,
        grid=(n // tm,),
        in_specs=[row(D_MODEL), row(DIFF_WIDTH), row(WIN_WIDTH), row(C_GATE),
                  full(DIFF_WIDTH, D_MODEL), full(WIN_WIDTH, D_MODEL), full(D_MODEL, D_MODEL),
                  full(1, D_MODEL), full(D_MODEL, LANES), full(1, LANES)],
        out_specs=[row(D_MODEL), row(D_MODEL), row(LANES), full(1, LANES)],
        out_shape=[jax.ShapeDtypeStruct((n, D_MODEL), F32), jax.ShapeDtypeStruct((n, D_MODEL), F32),
                   jax.ShapeDtypeStruct((n, LANES), F32), jax.ShapeDtypeStruct((1, LANES), F32)],
        compiler_params=pltpu.CompilerParams(
            dimension_semantics=("arbitrary",), vmem_limit_bytes=_vmem_limit(est)),
        name="merge_route",
    )(x2, od, ow, gates, pd, pw, wo, g, wr, br)


def _positions_kernel(route_ref, pstart_ref, dest_ref, carry_ref):
    i = pl.program_id(0)

    @pl.when(i == 0)
    def _():
        carry_ref[...] = jnp.zeros_like(carry_ref)

    r = route_ref[...]
    tb = r.shape[0]
    lane = lax.broadcasted_iota(jnp.int32, r.shape, 1)
    lane_f = lane.astype(F32)
    oh1 = lane_f == r[:, ROUTE_E1:ROUTE_E1 + 1]
    oh2 = lane_f == r[:, ROUTE_E2:ROUTE_E2 + 1]
    cnt = jnp.where(oh1, 1.0, 0.0) + jnp.where(oh2, 1.0, 0.0)
    rows = lax.broadcasted_iota(jnp.int32, (tb, tb), 0)
    cols = lax.broadcasted_iota(jnp.int32, (tb, tb), 1)
    tri = jnp.where(rows >= cols, 1.0, 0.0).astype(BF16)
    incl = jnp.dot(tri, cnt.astype(BF16), preferred_element_type=F32)
    row_of = incl - cnt + carry_ref[...] + pstart_ref[...]
    dest1 = jnp.sum(jnp.where(oh1, row_of, 0.0), axis=-1, keepdims=True)
    dest2 = jnp.sum(jnp.where(oh2, row_of, 0.0), axis=-1, keepdims=True)
    dest_ref[...] = jnp.where(lane == ROUTE_E1, dest1, jnp.where(lane == ROUTE_E2, dest2, 0.0))
    carry_ref[...] = carry_ref[...] + incl[tb - 1:tb, :]


def _positions(route, pad_start):
    n = route.shape[0]
    tb = POS_TILE
    return pl.pallas_call(
        _positions_kernel,
        grid=(n // tb,),
        in_specs=[pl.BlockSpec((tb, LANES), lambda i: (i, 0)),
                  pl.BlockSpec((1, LANES), lambda i: (0, 0))],
        out_specs=pl.BlockSpec((tb, LANES), lambda i: (i, 0)),
        out_shape=jax.ShapeDtypeStruct((n, LANES), F32),
        scratch_shapes=[pltpu.VMEM((1, LANES), F32)],
        compiler_params=pltpu.CompilerParams(dimension_semantics=("arbitrary",)),
        name="positions",
    )(route, pad_start)


def _dispatch_kernel(d1_ref, d2_ref, hn_ref, xs_init_hbm, xs_hbm, sem):
    del xs_init_hbm
    ts = hn_ref.shape[0]
    for t in range(ts):
        pltpu.make_async_copy(hn_ref.at[pl.ds(t, 1)], xs_hbm.at[pl.ds(d1_ref[t], 1)], sem).start()
        pltpu.make_async_copy(hn_ref.at[pl.ds(t, 1)], xs_hbm.at[pl.ds(d2_ref[t], 1)], sem).start()
    for _ in range(2):
        pltpu.make_async_copy(hn_ref, xs_hbm.at[pl.ds(0, ts)], sem).wait()


def _dispatch(dest1, dest2, hn, xs_init):
    n = hn.shape[0]
    ts = MOVE_TILE
    smem_blk = pl.BlockSpec((ts,), lambda i: (i,), memory_space=pltpu.SMEM)
    return pl.pallas_call(
        _dispatch_kernel,
        grid=(n // ts,),
        in_specs=[smem_blk, smem_blk,
                  pl.BlockSpec((ts, D_MODEL), lambda i: (i, 0)), pl.BlockSpec(memory_space=pl.ANY)],
        out_specs=pl.BlockSpec(memory_space=pl.ANY),
        out_shape=jax.ShapeDtypeStruct(xs_init.shape, F32),
        scratch_shapes=[pltpu.SemaphoreType.DMA(())],
        input_output_aliases={3: 0},
        compiler_params=pltpu.CompilerParams(dimension_semantics=("arbitrary",)),
        name="dispatch",
    )(dest1, dest2, hn, xs_init)


def _experts_kernel(be_ref, na_ref, xs_ref, wg_ref, wu_ref, wd_ref, ys_ref):
    del be_ref
    j = pl.program_id(0)

    @pl.when(j < na_ref[0])
    def _():
        x = xs_ref[...].astype(BF16)
        g = jnp.dot(x, wg_ref[0], preferred_element_type=F32)
        u = jnp.dot(x, wu_ref[0], preferred_element_type=F32)
        hid = (g * jax.nn.sigmoid(g) * u).astype(BF16)
        ys_ref[...] = jnp.dot(hid, wd_ref[0], preferred_element_type=F32)

    @pl.when(j >= na_ref[0])
    def _():
        ys_ref[...] = jnp.zeros_like(ys_ref)


def _experts(block_e, n_active, xs, wg, wu, wd):
    n_rows = xs.shape[0]
    tmb = MOE_TILE
    nblk = n_rows // tmb
    rows = pl.BlockSpec((tmb, D_MODEL), lambda j, be, na: (j, 0))
    est = (4 * tmb * D_MODEL * 4 + 2 * 3 * D_MODEL * EXPERT_HIDDEN * 2
           + 4 * tmb * EXPERT_HIDDEN * 4 + tmb * D_MODEL * 4)
    return pl.pallas_call(
        _experts_kernel,
        grid_spec=pltpu.PrefetchScalarGridSpec(
            num_scalar_prefetch=2,
            grid=(nblk,),
            in_specs=[
                rows,
                pl.BlockSpec((1, D_MODEL, EXPERT_HIDDEN), lambda j, be, na: (be[j], 0, 0)),
                pl.BlockSpec((1, D_MODEL, EXPERT_HIDDEN), lambda j, be, na: (be[j], 0, 0)),
                pl.BlockSpec((1, EXPERT_HIDDEN, D_MODEL), lambda j, be, na: (be[j], 0, 0)),
            ],
            out_specs=rows,
        ),
        out_shape=jax.ShapeDtypeStruct((n_rows, D_MODEL), F32),
        compiler_params=pltpu.CompilerParams(
            dimension_semantics=("arbitrary",), vmem_limit_bytes=_vmem_limit(est)),
        name="experts",
    )(block_e, n_active, xs, wg, wu, wd)


def _combine_kernel(d1_ref, d2_ref, route_ref, x_ref, g_ref, ys_hbm, o_ref, buf, sem, *, final_norm):
    tc = d1_ref.shape[0]
    for t in range(tc):
        pltpu.make_async_copy(ys_hbm.at[pl.ds(d1_ref[t], 1)], buf.at[pl.ds(t, 1)], sem).start()
        pltpu.make_async_copy(ys_hbm.at[pl.ds(d2_ref[t], 1)], buf.at[pl.ds(tc + t, 1)], sem).start()
    pltpu.make_async_copy(ys_hbm.at[pl.ds(0, 2 * tc)], buf, sem).wait()

    r = route_ref[...]
    w1 = r[:, ROUTE_W1:ROUTE_W1 + 1]
    w2 = r[:, ROUTE_W2:ROUTE_W2 + 1]
    out = x_ref[...] + w1 * buf[0:tc, :] + w2 * buf[tc:2 * tc, :]
    if final_norm:
        ms = jnp.mean(out * out, axis=-1, keepdims=True)
        out = out * lax.rsqrt(ms + RMS_EPS) * g_ref[...]
    o_ref[...] = out


def _combine(dest1, dest2, route, x1, g, ys, final_norm):
    n = x1.shape[0]
    tc = MOVE_TILE
    smem_blk = pl.BlockSpec((tc,), lambda i: (i,), memory_space=pltpu.SMEM)
    est = 2 * tc * D_MODEL * 4 * 2 + 2 * tc * D_MODEL * 4 + 2 * tc * LANES * 4 + 4 * tc * D_MODEL * 4
    return pl.pallas_call(
        functools.partial(_combine_kernel, final_norm=final_norm),
        grid=(n // tc,),
        in_specs=[smem_blk, smem_blk,
                  pl.BlockSpec((tc, LANES), lambda i: (i, 0)),
                  pl.BlockSpec((tc, D_MODEL), lambda i: (i, 0)),
                  pl.BlockSpec((1, D_MODEL), lambda i: (0, 0)),
                  pl.BlockSpec(memory_space=pl.ANY)],
        out_specs=pl.BlockSpec((tc, D_MODEL), lambda i: (i, 0)),
        out_shape=jax.ShapeDtypeStruct((n, D_MODEL), F32),
        scratch_shapes=[pltpu.VMEM((2 * tc, D_MODEL), F32), pltpu.SemaphoreType.DMA(())],
        compiler_params=pltpu.CompilerParams(
            dimension_semantics=("arbitrary",), vmem_limit_bytes=_vmem_limit(est)),
        name="combine",
    )(dest1, dest2, route, x1, g, ys)


def _plan_blocks(counts, n_rows):
    counts = counts[0, :N_EXPERTS].astype(jnp.int32)
    padded = (counts + MOE_TILE - 1) // MOE_TILE * MOE_TILE
    pad_end = jnp.cumsum(padded)
    pad_start = pad_end - padded
    blk_start = jnp.arange(n_rows // MOE_TILE, dtype=jnp.int32) * MOE_TILE
    block_e = jnp.minimum(jnp.sum((pad_end[None, :] <= blk_start[:, None]).astype(jnp.int32), axis=1),
                          N_EXPERTS - 1)
    n_active = (pad_end[-1:] // MOE_TILE).astype(jnp.int32)
    pad_start_lanes = jnp.zeros((1, LANES), F32).at[0, :N_EXPERTS].set(pad_start.astype(F32))
    return pad_start_lanes, block_e, n_active


def _forward(x, w_in, diff_lambda, diff_subln, win_sink, w_branch_diff, w_branch_win, w_out,
             rel_bias, norm_mix, norm_ffn, w_router_group, b_router_group, w_router_expert,
             b_router_expert, w_exp_gate, w_exp_up, w_exp_down, norm_final):
    batch, seq, _ = x.shape
    n = batch * seq
    depth = w_in.shape[0]
    assert seq % DIFF_KEY_CHUNK == 0 and seq % DIFF_TQ == 0 and seq >= WIN_BAND
    assert n % ROW_TILE == 0 and n % POS_TILE == 0 and n % MOVE_TILE == 0

    diff_strip = _diff_bias_strip(rel_bias[:, :DIFF_HEADS], DIFF_TQ)
    win_strip = _win_bias_strip(rel_bias[:, DIFF_HEADS:])
    n_rows = -(-(2 * n + N_EXPERTS * (MOE_TILE - 1)) // MOE_TILE) * MOE_TILE
    xs = jnp.zeros((n_rows, D_MODEL), F32)

    x2 = x.reshape(n, D_MODEL)
    for l in range(depth):
        splits = (512, 1024, 1536, 2048, 2176, 2304)
        wdq, wdk, wdv, wwq, wwk, wwv, wgt = jnp.split(w_in[l], splits, axis=-1)
        dup = lambda w: jnp.concatenate(
            [w[:, :WIN_HEAD_DIM], w[:, :WIN_HEAD_DIM], w[:, WIN_HEAD_DIM:], w[:, WIN_HEAD_DIM:]],
            axis=-1)
        w_all = jnp.concatenate([wdq, wdk, wdv, wwq, dup(wwk), dup(wwv), wgt], axis=-1).astype(BF16)

        dq, dk, dv, wq, wk, wv, gates = _inproj(x2, norm_mix[l][None], w_all)

        lambda_init = 0.8 - 0.6 * math.exp(-0.3 * l)
        o_diff = _diff_attention(dq, dk, dv, diff_lambda[l], diff_strip, diff_subln[l][None],
                                 batch, seq, lambda_init)
        o_win = _win_attention(wq, wk, wv, win_sink[l], win_strip, batch, seq)

        wr = jnp.zeros((D_MODEL, LANES // 2), F32)
        wr = wr.at[:, :N_GROUPS].set(w_router_group[l])
        wr = wr.at[:, N_GROUPS:N_GROUPS + N_EXPERTS].set(w_router_expert[l])
        wr_hi = wr.astype(BF16)
        wr_lo = (wr - wr_hi.astype(F32)).astype(BF16)
        wr = jnp.concatenate([wr_hi, wr_lo], axis=1)
        br = jnp.zeros((1, LANES), F32)
        br = br.at[0, :N_GROUPS].set(b_router_group[l])
        br = br.at[0, N_GROUPS:N_GROUPS + N_EXPERTS].set(b_router_expert[l])
        x1, hn, route, counts = _merge(x2, o_diff, o_win, gates,
                                       w_branch_diff[l].astype(BF16), w_branch_win[l].astype(BF16),
                                       w_out[l].astype(BF16), norm_ffn[l][None], wr, br)

        pad_start, block_e, n_active = _plan_blocks(counts, n_rows)
        dest = _positions(route, pad_start)
        dest1 = dest[:, ROUTE_E1].astype(jnp.int32)
        dest2 = dest[:, ROUTE_E2].astype(jnp.int32)
        xs = _dispatch(dest1, dest2, hn, xs)
        ys = _experts(block_e, n_active, xs, w_exp_gate[l].astype(BF16), w_exp_up[l].astype(BF16),
                      w_exp_down[l].astype(BF16))
        x2 = _combine(dest1, dest2, route, x1, norm_final[None], ys, final_norm=(l == depth - 1))
    return x2.reshape(batch, seq, D_MODEL)


def kernel(x, w_in, diff_lambda, diff_subln, win_sink, w_branch_diff, w_branch_win, w_out, rel_bias, norm_mix, norm_ffn, w_router_group, b_router_group, w_router_expert, b_router_expert, w_exp_gate, w_exp_up, w_exp_down, norm_final):
    return _forward(x, w_in, diff_lambda, diff_subln, win_sink, w_branch_diff, w_branch_win, w_out,
                    rel_bias, norm_mix, norm_ffn, w_router_group, b_router_group, w_router_expert,
                    b_router_expert, w_exp_gate, w_exp_up, w_exp_down, norm_final)
```

```python
import functools
import math

import jax
import jax.numpy as jnp
from jax import lax
from jax.experimental import pallas as pl
from jax.experimental.pallas import tpu as pltpu

D_MODEL = 1024
DIFF_HEADS = 4
DIFF_HEAD_DIM = 64
DIFF_V_DIM = 2 * DIFF_HEAD_DIM
DIFF_WIDTH = DIFF_HEADS * DIFF_V_DIM
WIN_HEADS = 8
WIN_KV_HEADS = 2
WIN_GROUP = WIN_HEADS // WIN_KV_HEADS
WIN_HEAD_DIM = 64
WIN_WIDTH = WIN_HEADS * WIN_HEAD_DIM
WINDOW = 128
WIN_BLOCK = 128
REL_BUCKETS = 32
REL_MAX_DIST = 128
N_GROUPS = 4
EXPERTS_PER_GROUP = 8
N_EXPERTS = N_GROUPS * EXPERTS_PER_GROUP
EXPERT_HIDDEN = 512
RMS_EPS = 1e-6
NEG_INF = -1e30

LANES = 128
SUBLANES = 8
V7X_VMEM_BYTES = 64 * 1024 * 1024

ROW_TILE = 512
DIFF_TQ = 256
DIFF_TILES_PER_STEP = 4
DIFF_KEY_BLOCK = 128
DIFF_KEY_CHUNK = 512
DIFF_NEAR_BLOCKS = 3
DIFF_NEAR_BLOCKS_NEG = 2
LOG2E = math.log2(math.e)
POS_TILE = 256
MOE_TILE = 256
MOVE_TILE = 256

F32 = jnp.float32
BF16 = jnp.bfloat16


def _vmem_limit(nbytes):
    return int(min(max(2 * nbytes, 16 * 1024 * 1024), V7X_VMEM_BYTES - 8 * 1024 * 1024))


def _rel_bucket(rel):
    half = REL_BUCKETS // 2
    max_exact = half // 2
    n = jnp.abs(rel)
    nf = jnp.maximum(n, max_exact).astype(jnp.float32)
    large = max_exact + (jnp.log(nf / max_exact) / math.log(REL_MAX_DIST / max_exact)
                         * (half - max_exact)).astype(jnp.int32)
    large = jnp.minimum(large, half - 1)
    return jnp.where(rel > 0, half, 0) + jnp.where(n < max_exact, n, large)


C_DQ, C_DK, C_DV, C_WQ = 512, 512, 512, 512
C_WKD, C_WVD = 2 * LANES, 2 * LANES
C_GATE = 2 * D_MODEL
IN_COLS = C_DQ + C_DK + C_DV + C_WQ + C_WKD + C_WVD + C_GATE


def _project(x, g_ref, w_ref, dq_ref, dk_ref, dv_ref, wq_ref, wk_ref, wv_ref, gate_ref):
    ms = jnp.mean(x * x, axis=-1, keepdims=True)
    h = (x * lax.rsqrt(ms + RMS_EPS) * g_ref[...]).astype(BF16)

    col = 0

    def proj(width):
        nonlocal col
        out = jnp.dot(h, w_ref[:, col:col + width], preferred_element_type=F32)
        col += width
        return out

    dq_ref[...] = (proj(C_DQ) * (DIFF_HEAD_DIM ** -0.5 * LOG2E)).astype(BF16)
    dk_ref[...] = proj(C_DK).astype(BF16)
    dv_ref[...] = proj(C_DV).astype(BF16)
    wq_ref[...] = (proj(C_WQ) * (WIN_HEAD_DIM ** -0.5 * LOG2E)).astype(BF16)
    wk_ref[...] = proj(C_WKD).astype(BF16)
    wv_ref[...] = proj(C_WVD).astype(BF16)
    gate_ref[...] = jax.nn.sigmoid(proj(C_GATE)).astype(BF16)


def _inproj_kernel(x_ref, g_ref, w_ref, *out_refs):
    _project(x_ref[...], g_ref, w_ref, *out_refs)


PROJ_WIDTHS = (C_DQ, C_DK, C_DV, C_WQ, C_WKD, C_WVD, C_GATE)


def _inproj(x2, g, w):
    n = x2.shape[0]
    tm = ROW_TILE
    est = 2 * tm * D_MODEL * 4 + D_MODEL * IN_COLS * 2 + 2 * tm * IN_COLS * 2 + tm * IN_COLS * 4
    return pl.pallas_call(
        _inproj_kernel,
        grid=(n // tm,),
        in_specs=[
            pl.BlockSpec((tm, D_MODEL), lambda i: (i, 0)),
            pl.BlockSpec((1, D_MODEL), lambda i: (0, 0)),
            pl.BlockSpec((D_MODEL, IN_COLS), lambda i: (0, 0), pipeline_mode=pl.Buffered(1)),
        ],
        out_specs=[pl.BlockSpec((tm, c), lambda i: (i, 0)) for c in PROJ_WIDTHS],
        out_shape=[jax.ShapeDtypeStruct((n, c), BF16) for c in PROJ_WIDTHS],
        compiler_params=pltpu.CompilerParams(
            dimension_semantics=("parallel",), vmem_limit_bytes=_vmem_limit(est)),
        name="inproj",
    )(x2, g, w)


def _combine_inproj_kernel(d1_next_ref, d2_next_ref, d1_first_ref, d2_first_ref, route_ref, x1_ref,
                           g_ref, w_ref, ys_hbm, x2_ref, *rest):
    out_refs, (buf, sem) = rest[:len(PROJ_WIDTHS)], rest[len(PROJ_WIDTHS):]
    i = pl.program_id(0)
    tm = x1_ref.shape[0]
    slot = i % 2

    def gather(i1_ref, i2_ref, dst):
        for t in range(tm):
            pltpu.make_async_copy(ys_hbm.at[pl.ds(i1_ref[t], 1)], buf.at[dst, pl.ds(t, 1)],
                                  sem.at[dst]).start(priority=0)
            pltpu.make_async_copy(ys_hbm.at[pl.ds(i2_ref[t], 1)], buf.at[dst, pl.ds(tm + t, 1)],
                                  sem.at[dst]).start(priority=1)

    @pl.when(i == 0)
    def _():
        gather(d1_first_ref, d2_first_ref, 0)

    @pl.when(i + 1 < pl.num_programs(0))
    def _():
        gather(d1_next_ref, d2_next_ref, 1 - slot)

    pltpu.make_async_copy(ys_hbm.at[pl.ds(0, 2 * tm)], buf.at[slot], sem.at[slot]).wait()
    r = route_ref[...]
    w1 = r[:, ROUTE_W1:ROUTE_W1 + 1]
    w2 = r[:, ROUTE_W2:ROUTE_W2 + 1]
    x = x1_ref[...] + w1 * buf[slot, 0:tm, :] + w2 * buf[slot, tm:2 * tm, :]
    x2_ref[...] = x
    _project(x, g_ref, w_ref, *out_refs)


def _combine_inproj(dest1, dest2, route, x1, ys, g, w):
    n = x1.shape[0]
    tm = ROW_TILE
    nt = n // tm
    nxt = pl.BlockSpec((tm,), lambda i: (jnp.minimum(i + 1, nt - 1),), memory_space=pltpu.SMEM)
    first = pl.BlockSpec((tm,), lambda i: (0,), memory_space=pltpu.SMEM)
    est = (4 * tm * D_MODEL * 4 + D_MODEL * IN_COLS * 2 + 2 * tm * IN_COLS * 2 + tm * IN_COLS * 4
           + 2 * 2 * tm * D_MODEL * 4)
    return pl.pallas_call(
        _combine_inproj_kernel,
        grid=(nt,),
        in_specs=[
            nxt, nxt, first, first,
            pl.BlockSpec((tm, LANES), lambda i: (i, 0)),
            pl.BlockSpec((tm, D_MODEL), lambda i: (i, 0)),
            pl.BlockSpec((1, D_MODEL), lambda i: (0, 0)),
            pl.BlockSpec((D_MODEL, IN_COLS), lambda i: (0, 0), pipeline_mode=pl.Buffered(1)),
            pl.BlockSpec(memory_space=pl.ANY),
        ],
        out_specs=[pl.BlockSpec((tm, D_MODEL), lambda i: (i, 0))]
        + [pl.BlockSpec((tm, c), lambda i: (i, 0)) for c in PROJ_WIDTHS],
        out_shape=[jax.ShapeDtypeStruct((n, D_MODEL), F32)]
        + [jax.ShapeDtypeStruct((n, c), BF16) for c in PROJ_WIDTHS],
        scratch_shapes=[pltpu.VMEM((2, 2 * tm, D_MODEL), F32), pltpu.SemaphoreType.DMA((2,))],
        compiler_params=pltpu.CompilerParams(
            dimension_semantics=("arbitrary",), vmem_limit_bytes=_vmem_limit(est)),
        name="combine_inproj",
    )(dest1, dest2, dest1, dest2, route, x1, g, w, ys)


def _diff_attn_kernel(lam_ref, q_ref, k_ref, v_ref, bias_ref, g_ref, o_ref, vt_ref, s_ref, p_ref, *,
                      seq, lambda_init):
    qi = pl.program_id(2)

    @pl.when(qi == 0)
    def _():
        vt_ref[...] = v_ref[...].astype(F32).T.astype(BF16)

    lp = lam_ref[...]
    lam = (jnp.exp(jnp.sum(lp[0:1] * lp[1:2], axis=-1, keepdims=True))
           - jnp.exp(jnp.sum(lp[2:3] * lp[3:4], axis=-1, keepdims=True)) + lambda_init)

    tq = DIFF_TQ
    tiles = q_ref.shape[0] // tq
    kb = DIFF_KEY_BLOCK
    groups = kb // SUBLANES
    nkb = seq // kb
    per_chunk = DIFF_KEY_CHUNK // kb
    width = bias_ref.shape[-1]
    row = lax.broadcasted_iota(jnp.int32, (LANES, 1), 0)

    def scores(t):
        qt = q_ref[t * tq:(t + 1) * tq, :].astype(F32).T
        qcat = jnp.concatenate([jnp.where(row < DIFF_HEAD_DIM, qt, 0.0),
                                jnp.where(row >= DIFF_HEAD_DIM, qt, 0.0)], axis=1).astype(BF16)
        q0 = (qi * tiles + t) * tq
        mx = jnp.full((SUBLANES, 2 * tq), NEG_INF, F32)
        for c in range(seq // DIFF_KEY_CHUNK):
            s = jnp.dot(k_ref[c * DIFF_KEY_CHUNK:(c + 1) * DIFF_KEY_CHUNK, :], qcat,
                        preferred_element_type=F32)
            for rr in range(per_chunk):
                r = c * per_chunk + rr
                start = pl.multiple_of(
                    jnp.clip(DIFF_NEAR_BLOCKS * kb - r * kb + q0, 0, width - tq), LANES)
                b = bias_ref[0, :, pl.ds(start, tq)]
                blk = s[rr * kb:(rr + 1) * kb, :]
                blk = jnp.concatenate([blk[:, :tq] + b, blk[:, tq:] + b], axis=1)
                mx = jnp.maximum(mx, jnp.max(blk.reshape(groups, SUBLANES, 2 * tq), axis=0))
                s_ref[t, r * kb:(r + 1) * kb, :] = blk
        return jnp.max(mx, axis=0, keepdims=True)

    def probs(t, m):
        sm = jnp.zeros((SUBLANES, 2 * tq), F32)
        for r in range(nkb):
            p = jnp.exp2(s_ref[t, r * kb:(r + 1) * kb, :] - m)
            sm = sm + jnp.sum(p.reshape(groups, SUBLANES, 2 * tq), axis=0)
            p_ref[t, r * kb:(r + 1) * kb, :] = p.astype(BF16)
        return jnp.sum(sm, axis=0, keepdims=True)

    def values(t, l):
        vt = vt_ref[...]
        acc1 = jnp.dot(vt, p_ref[t, :, :tq], preferred_element_type=F32)
        acc2 = jnp.dot(vt, p_ref[t, :, tq:], preferred_element_type=F32)
        o = (acc1 / l[:, :tq] - lam * (acc2 / l[:, tq:])).T
        ms = jnp.mean(o * o, axis=-1, keepdims=True)
        o = o * lax.rsqrt(ms + RMS_EPS) * g_ref[...] * (1.0 - lambda_init)
        o_ref[t * tq:(t + 1) * tq, :] = o.astype(o_ref.dtype)

    m_next = scores(0)
    for t in range(tiles):
        m = m_next
        if t + 1 < tiles:
            m_next = scores(t + 1)
        values(t, probs(t, m))


def _toeplitz(t, rows, width):
    h, length = t.shape
    assert length == rows + width - 1
    flat = jnp.tile(jnp.pad(t, ((0, 0), (0, 1))), (1, rows))[:, :rows * length]
    return flat.reshape(h, rows, length)[:, :, rows - 1:]


def _diff_bias_strip(rel_diff, tq):
    kb = DIFF_KEY_BLOCK
    assert DIFF_NEAR_BLOCKS * kb - (tq - 1) > REL_MAX_DIST
    assert -DIFF_NEAR_BLOCKS_NEG * kb + kb - 1 < -REL_MAX_DIST
    width = tq + (DIFF_NEAR_BLOCKS + DIFF_NEAR_BLOCKS_NEG) * kb
    rel = jnp.arange(kb + width - 1) - (width - 1) + DIFF_NEAR_BLOCKS * kb
    t = rel_diff[_rel_bucket(rel)].T.astype(F32) * LOG2E
    return _toeplitz(t, kb, width)[:, ::-1, ::-1]


def _diff_attention(dq, dk, dv, lam_p, strip, subln_g, batch, seq, lambda_init):
    tq = DIFF_TQ
    tiles = min(DIFF_TILES_PER_STEP, seq // tq)
    ts = tiles * tq
    nq = seq // ts
    width = strip.shape[-1]
    est = (2 * ts * LANES * 2 * 2 + 2 * 2 * seq * LANES * 2 + 2 * DIFF_KEY_BLOCK * width * 4
           + seq * LANES * 2 + tiles * seq * 2 * tq * (4 + 2) + DIFF_KEY_CHUNK * 2 * tq * 4)
    return pl.pallas_call(
        functools.partial(_diff_attn_kernel, seq=seq, lambda_init=lambda_init),
        grid=(batch, DIFF_HEADS, nq),
        in_specs=[
            pl.BlockSpec((4, DIFF_HEAD_DIM), lambda b, h, i: (0, 0)),
            pl.BlockSpec((ts, LANES), lambda b, h, i: (b * nq + i, h)),
            pl.BlockSpec((seq, LANES), lambda b, h, i: (b, h)),
            pl.BlockSpec((seq, LANES), lambda b, h, i: (b, h)),
            pl.BlockSpec((1, DIFF_KEY_BLOCK, width), lambda b, h, i: (h, 0, 0)),
            pl.BlockSpec((1, DIFF_V_DIM), lambda b, h, i: (0, 0)),
        ],
        out_specs=pl.BlockSpec((ts, LANES), lambda b, h, i: (b * nq + i, h)),
        out_shape=jax.ShapeDtypeStruct((batch * seq, DIFF_WIDTH), BF16),
        scratch_shapes=[pltpu.VMEM((DIFF_V_DIM, seq), BF16),
                        pltpu.VMEM((tiles, seq, 2 * tq), F32),
                        pltpu.VMEM((tiles, seq, 2 * tq), BF16)],
        compiler_params=pltpu.CompilerParams(
            dimension_semantics=("parallel", "parallel", "arbitrary"),
            vmem_limit_bytes=_vmem_limit(est)),
        name="diff_attn",
    )(lam_p, dq, dk, dv, strip, subln_g)


WIN_BAND = 3 * WIN_BLOCK
WIN_STRIP = 5 * WIN_BLOCK


def _win_attn_kernel(q_ref, k_ref, v_ref, strip_ref, sink_ref, o_ref, *, seq):
    n = pl.program_id(1)
    start = pl.multiple_of(jnp.clip(n * WIN_BLOCK - WIN_BLOCK, 0, seq - WIN_BAND), WIN_BLOCK)
    ustart = pl.multiple_of(start - n * WIN_BLOCK + 2 * WIN_BLOCK, WIN_BLOCK)
    row = lax.broadcasted_iota(jnp.int32, (LANES, 1), 0)
    lane = lax.broadcasted_iota(jnp.int32, (1, LANES), 1)
    for kv in range(WIN_KV_HEADS):
        kb = k_ref[pl.ds(start, WIN_BAND), kv * LANES:(kv + 1) * LANES]
        vb = v_ref[pl.ds(start, WIN_BAND), kv * LANES:(kv + 1) * LANES]
        cols = []
        for pair in range(WIN_GROUP // 2):
            c0 = kv * WIN_GROUP * WIN_HEAD_DIM + pair * LANES
            qt = q_ref[:, c0:c0 + LANES].astype(F32).T
            cols.append(jnp.where(row < WIN_HEAD_DIM, qt, 0.0))
            cols.append(jnp.where(row >= WIN_HEAD_DIM, qt, 0.0))
        qcat = jnp.concatenate(cols, axis=1).astype(BF16)
        s = jnp.dot(kb, qcat, preferred_element_type=F32)
        s = s + strip_ref[kv, pl.ds(ustart, WIN_BAND), :]
        sink = sink_ref[kv]
        m = jnp.maximum(jnp.max(s, axis=0, keepdims=True), sink)
        p = jnp.exp2(s - m)
        den = jnp.sum(p, axis=0, keepdims=True) + jnp.exp2(sink - m)
        vt = vb.astype(F32).T.astype(BF16)
        o = (jnp.dot(vt, p.astype(BF16), preferred_element_type=F32) / den).T
        for pair in range(WIN_GROUP // 2):
            c0 = kv * WIN_GROUP * WIN_HEAD_DIM + pair * LANES
            even = o[(2 * pair) * WIN_BLOCK:(2 * pair + 1) * WIN_BLOCK, :]
            odd = o[(2 * pair + 1) * WIN_BLOCK:(2 * pair + 2) * WIN_BLOCK, :]
            o_ref[:, c0:c0 + LANES] = jnp.where(lane < WIN_HEAD_DIM, even, odd).astype(o_ref.dtype)


def _win_bias_strip(rel_win):
    rel = jnp.arange(WIN_BLOCK + WIN_STRIP - 1) - (WIN_BLOCK - 1) - 2 * WIN_BLOCK
    t = jnp.where((jnp.abs(rel) <= WINDOW)[None],
                  rel_win[_rel_bucket(rel)].T.astype(F32) * LOG2E, NEG_INF)
    strip = _toeplitz(t, WIN_BLOCK, WIN_STRIP)
    strip = strip.reshape(WIN_KV_HEADS, WIN_GROUP, WIN_BLOCK, WIN_STRIP).transpose(0, 3, 1, 2)
    return strip.reshape(WIN_KV_HEADS, WIN_STRIP, WIN_GROUP * WIN_BLOCK)


def _win_attention(wq, wk, wv, sink, strip, batch, seq):
    nb = seq // WIN_BLOCK
    heads_lanes = WIN_GROUP * WIN_BLOCK
    sink_lanes = jnp.broadcast_to((sink.astype(F32) * LOG2E).reshape(WIN_KV_HEADS, WIN_GROUP, 1),
                                  (WIN_KV_HEADS, WIN_GROUP, WIN_BLOCK)).reshape(
                                      WIN_KV_HEADS, 1, heads_lanes)
    est = (2 * WIN_BLOCK * WIN_WIDTH * 2 * 2 + 2 * 2 * seq * 2 * LANES * 2
           + 2 * WIN_KV_HEADS * WIN_STRIP * heads_lanes * 4 + 8 * WIN_BAND * heads_lanes * 4)
    return pl.pallas_call(
        functools.partial(_win_attn_kernel, seq=seq),
        grid=(batch, nb),
        in_specs=[
            pl.BlockSpec((WIN_BLOCK, WIN_WIDTH), lambda b, n: (b * nb + n, 0)),
            pl.BlockSpec((seq, 2 * LANES), lambda b, n: (b, 0)),
            pl.BlockSpec((seq, 2 * LANES), lambda b, n: (b, 0)),
            pl.BlockSpec((WIN_KV_HEADS, WIN_STRIP, heads_lanes), lambda b, n: (0, 0, 0)),
            pl.BlockSpec((WIN_KV_HEADS, 1, heads_lanes), lambda b, n: (0, 0, 0)),
        ],
        out_specs=pl.BlockSpec((WIN_BLOCK, WIN_WIDTH), lambda b, n: (b * nb + n, 0)),
        out_shape=jax.ShapeDtypeStruct((batch * seq, WIN_WIDTH), BF16),
        compiler_params=pltpu.CompilerParams(
            dimension_semantics=("parallel", "parallel"), vmem_limit_bytes=_vmem_limit(est)),
        name="win_attn",
    )(wq, wk, wv, strip, sink_lanes)


ROUTE_E1, ROUTE_E2, ROUTE_W1, ROUTE_W2 = 0, 1, 2, 3


def _merge_kernel(x_ref, od_ref, ow_ref, gate_ref, pd_ref, pw_ref, wo_ref, g_ref, wr_ref, br_ref,
                  x1_ref, hn_ref, route_ref, counts_ref):
    @pl.when(pl.program_id(0) == 0)
    def _():
        counts_ref[...] = jnp.zeros_like(counts_ref)

    md = jnp.dot(od_ref[...], pd_ref[...], preferred_element_type=F32)
    mw = jnp.dot(ow_ref[...], pw_ref[...], preferred_element_type=F32)
    merged = (gate_ref[:, :D_MODEL].astype(F32) * md + gate_ref[:, D_MODEL:].astype(F32) * mw)
    x1 = x_ref[...] + jnp.dot(merged.astype(BF16), wo_ref[...], preferred_element_type=F32)
    x1_ref[...] = x1
    ms = jnp.mean(x1 * x1, axis=-1, keepdims=True)
    hn = x1 * lax.rsqrt(ms + RMS_EPS) * g_ref[...]
    hn_ref[...] = hn

    hn_hi = hn.astype(BF16)
    hn_lo = (hn - hn_hi.astype(F32)).astype(BF16)
    parts = (jnp.dot(hn_hi, wr_ref[...], preferred_element_type=F32)
             + jnp.dot(hn_lo, wr_ref[...], preferred_element_type=F32))
    logits = parts + pltpu.roll(parts, LANES // 2, axis=1) + br_ref[...]
    lane = lax.broadcasted_iota(jnp.int32, logits.shape, 1)
    big = jnp.int32(LANES)
    is_group = lane < N_GROUPS
    gl = jnp.where(is_group, logits, NEG_INF)
    gmax = jnp.max(gl, axis=-1, keepdims=True)
    gsel = jnp.min(jnp.where(gl == gmax, lane, big), axis=-1, keepdims=True)
    gden = jnp.sum(jnp.where(is_group, jnp.exp(gl - gmax), 0.0), axis=-1, keepdims=True)
    gw = 1.0 / gden
    lo = N_GROUPS + EXPERTS_PER_GROUP * gsel
    in_group = jnp.logical_and(lane >= lo, lane < lo + EXPERTS_PER_GROUP)
    el = jnp.where(in_group, logits, NEG_INF)
    t1 = jnp.max(el, axis=-1, keepdims=True)
    i1 = jnp.min(jnp.where(el == t1, lane, big), axis=-1, keepdims=True)
    el2 = jnp.where(lane == i1, NEG_INF, el)
    t2 = jnp.max(el2, axis=-1, keepdims=True)
    i2 = jnp.min(jnp.where(el2 == t2, lane, big), axis=-1, keepdims=True)
    z = jnp.exp(t2 - t1)
    w1 = gw / (1.0 + z)
    w2 = gw * z / (1.0 + z)
    e1 = (i1 - N_GROUPS).astype(F32)
    e2 = (i2 - N_GROUPS).astype(F32)
    route = jnp.where(lane == ROUTE_E1, e1,
                      jnp.where(lane == ROUTE_E2, e2,
                                jnp.where(lane == ROUTE_W1, w1,
                                          jnp.where(lane == ROUTE_W2, w2, 0.0))))
    route_ref[...] = route
    hits = (jnp.where(lane == i1 - N_GROUPS, 1.0, 0.0) + jnp.where(lane == i2 - N_GROUPS, 1.0, 0.0))
    counts_ref[...] += jnp.sum(hits, axis=0, keepdims=True)


def _merge(x2, od, ow, gates, pd, pw, wo, g, wr, br):
    n = x2.shape[0]
    tm = ROW_TILE
    est = (2 * tm * D_MODEL * 4 * 3 + 2 * tm * (DIFF_WIDTH + WIN_WIDTH + C_GATE) * 2
           + 2 * (DIFF_WIDTH + WIN_WIDTH + D_MODEL) * D_MODEL * 2 + D_MODEL * LANES * 4 * 2
           + 6 * tm * D_MODEL * 4)
    row = lambda c: pl.BlockSpec((tm, c), lambda i: (i, 0))
    full = lambda r, c: pl.BlockSpec((r, c), lambda i: (0, 0))
    return pl.pallas_call(
        _merge_kernel,
        grid=(n // tm,),
        in_specs=[row(D_MODEL), row(DIFF_WIDTH), row(WIN_WIDTH), row(C_GATE),
                  full(DIFF_WIDTH, D_MODEL), full(WIN_WIDTH, D_MODEL), full(D_MODEL, D_MODEL),
                  full(1, D_MODEL), full(D_MODEL, LANES), full(1, LANES)],
        out_specs=[row(D_MODEL), row(D_MODEL), row(LANES), full(1, LANES)],
        out_shape=[jax.ShapeDtypeStruct((n, D_MODEL), F32), jax.ShapeDtypeStruct((n, D_MODEL), F32),
                   jax.ShapeDtypeStruct((n, LANES), F32), jax.ShapeDtypeStruct((1, LANES), F32)],
        compiler_params=pltpu.CompilerParams(
            dimension_semantics=("arbitrary",), vmem_limit_bytes=_vmem_limit(est)),
        name="merge_route",
    )(x2, od, ow, gates, pd, pw, wo, g, wr, br)


def _positions_kernel(route_ref, pstart_ref, dest_ref, carry_ref):
    i = pl.program_id(0)

    @pl.when(i == 0)
    def _():
        carry_ref[...] = jnp.zeros_like(carry_ref)

    r = route_ref[...]
    tb = r.shape[0]
    lane = lax.broadcasted_iota(jnp.int32, r.shape, 1)
    lane_f = lane.astype(F32)
    oh1 = lane_f == r[:, ROUTE_E1:ROUTE_E1 + 1]
    oh2 = lane_f == r[:, ROUTE_E2:ROUTE_E2 + 1]
    cnt = jnp.where(oh1, 1.0, 0.0) + jnp.where(oh2, 1.0, 0.0)
    rows = lax.broadcasted_iota(jnp.int32, (tb, tb), 0)
    cols = lax.broadcasted_iota(jnp.int32, (tb, tb), 1)
    tri = jnp.where(rows >= cols, 1.0, 0.0).astype(BF16)
    incl = jnp.dot(tri, cnt.astype(BF16), preferred_element_type=F32)
    row_of = incl - cnt + carry_ref[...] + pstart_ref[...]
    dest1 = jnp.sum(jnp.where(oh1, row_of, 0.0), axis=-1, keepdims=True)
    dest2 = jnp.sum(jnp.where(oh2, row_of, 0.0), axis=-1, keepdims=True)
    dest_ref[...] = jnp.where(lane == ROUTE_E1, dest1, jnp.where(lane == ROUTE_E2, dest2, 0.0))
    carry_ref[...] = carry_ref[...] + incl[tb - 1:tb, :]


def _positions(route, pad_start):
    n = route.shape[0]
    tb = POS_TILE
    return pl.pallas_call(
        _positions_kernel,
        grid=(n // tb,),
        in_specs=[pl.BlockSpec((tb, LANES), lambda i: (i, 0)),
                  pl.BlockSpec((1, LANES), lambda i: (0, 0))],
        out_specs=pl.BlockSpec((tb, LANES), lambda i: (i, 0)),
        out_shape=jax.ShapeDtypeStruct((n, LANES), F32),
        scratch_shapes=[pltpu.VMEM((1, LANES), F32)],
        compiler_params=pltpu.CompilerParams(dimension_semantics=("arbitrary",)),
        name="positions",
    )(route, pad_start)


def _dispatch_kernel(d1_ref, d2_ref, hn_ref, xs_init_hbm, xs_hbm, sem):
    del xs_init_hbm
    ts = hn_ref.shape[0]
    for t in range(ts):
        pltpu.make_async_copy(hn_ref.at[pl.ds(t, 1)], xs_hbm.at[pl.ds(d1_ref[t], 1)],
                              sem).start(priority=0)
        pltpu.make_async_copy(hn_ref.at[pl.ds(t, 1)], xs_hbm.at[pl.ds(d2_ref[t], 1)],
                              sem).start(priority=1)
    for _ in range(2):
        pltpu.make_async_copy(hn_ref, xs_hbm.at[pl.ds(0, ts)], sem).wait()


def _dispatch(dest1, dest2, hn, xs_init):
    n = hn.shape[0]
    ts = MOVE_TILE
    smem_blk = pl.BlockSpec((ts,), lambda i: (i,), memory_space=pltpu.SMEM)
    return pl.pallas_call(
        _dispatch_kernel,
        grid=(n // ts,),
        in_specs=[smem_blk, smem_blk,
                  pl.BlockSpec((ts, D_MODEL), lambda i: (i, 0)), pl.BlockSpec(memory_space=pl.ANY)],
        out_specs=pl.BlockSpec(memory_space=pl.ANY),
        out_shape=jax.ShapeDtypeStruct(xs_init.shape, F32),
        scratch_shapes=[pltpu.SemaphoreType.DMA(())],
        input_output_aliases={3: 0},
        compiler_params=pltpu.CompilerParams(dimension_semantics=("arbitrary",)),
        name="dispatch",
    )(dest1, dest2, hn, xs_init)


def _experts_kernel(be_ref, na_ref, xs_ref, wg_ref, wu_ref, wd_ref, ys_ref):
    del be_ref
    j = pl.program_id(0)

    @pl.when(j < na_ref[0])
    def _():
        x = xs_ref[...].astype(BF16)
        g = jnp.dot(x, wg_ref[0], preferred_element_type=F32)
        u = jnp.dot(x, wu_ref[0], preferred_element_type=F32)
        hid = (g * jax.nn.sigmoid(g) * u).astype(BF16)
        ys_ref[...] = jnp.dot(hid, wd_ref[0], preferred_element_type=F32)

    @pl.when(j >= na_ref[0])
    def _():
        ys_ref[...] = jnp.zeros_like(ys_ref)


def _experts(block_e, n_active, xs, wg, wu, wd):
    n_rows = xs.shape[0]
    tmb = MOE_TILE
    nblk = n_rows // tmb
    rows = pl.BlockSpec((tmb, D_MODEL), lambda j, be, na: (j, 0))
    est = (4 * tmb * D_MODEL * 4 + 2 * 3 * D_MODEL * EXPERT_HIDDEN * 2
           + 4 * tmb * EXPERT_HIDDEN * 4 + tmb * D_MODEL * 4)
    return pl.pallas_call(
        _experts_kernel,
        grid_spec=pltpu.PrefetchScalarGridSpec(
            num_scalar_prefetch=2,
            grid=(nblk,),
            in_specs=[
                rows,
                pl.BlockSpec((1, D_MODEL, EXPERT_HIDDEN), lambda j, be, na: (be[j], 0, 0)),
                pl.BlockSpec((1, D_MODEL, EXPERT_HIDDEN), lambda j, be, na: (be[j], 0, 0)),
                pl.BlockSpec((1, EXPERT_HIDDEN, D_MODEL), lambda j, be, na: (be[j], 0, 0)),
            ],
            out_specs=rows,
        ),
        out_shape=jax.ShapeDtypeStruct((n_rows, D_MODEL), F32),
        compiler_params=pltpu.CompilerParams(
            dimension_semantics=("arbitrary",), vmem_limit_bytes=_vmem_limit(est)),
        name="experts",
    )(block_e, n_active, xs, wg, wu, wd)


def _combine_kernel(d1_ref, d2_ref, route_ref, x_ref, g_ref, ys_hbm, o_ref, buf, sem):
    tc = d1_ref.shape[0]
    for t in range(tc):
        pltpu.make_async_copy(ys_hbm.at[pl.ds(d1_ref[t], 1)], buf.at[pl.ds(t, 1)],
                              sem).start(priority=0)
        pltpu.make_async_copy(ys_hbm.at[pl.ds(d2_ref[t], 1)], buf.at[pl.ds(tc + t, 1)],
                              sem).start(priority=1)
    pltpu.make_async_copy(ys_hbm.at[pl.ds(0, 2 * tc)], buf, sem).wait()

    r = route_ref[...]
    w1 = r[:, ROUTE_W1:ROUTE_W1 + 1]
    w2 = r[:, ROUTE_W2:ROUTE_W2 + 1]
    out = x_ref[...] + w1 * buf[0:tc, :] + w2 * buf[tc:2 * tc, :]
    ms = jnp.mean(out * out, axis=-1, keepdims=True)
    o_ref[...] = out * lax.rsqrt(ms + RMS_EPS) * g_ref[...]


def _combine(dest1, dest2, route, x1, g, ys):
    n = x1.shape[0]
    tc = MOVE_TILE
    smem_blk = pl.BlockSpec((tc,), lambda i: (i,), memory_space=pltpu.SMEM)
    est = 2 * tc * D_MODEL * 4 * 2 + 2 * tc * D_MODEL * 4 + 2 * tc * LANES * 4 + 4 * tc * D_MODEL * 4
    return pl.pallas_call(
        _combine_kernel,
        grid=(n // tc,),
        in_specs=[smem_blk, smem_blk,
                  pl.BlockSpec((tc, LANES), lambda i: (i, 0)),
                  pl.BlockSpec((tc, D_MODEL), lambda i: (i, 0)),
                  pl.BlockSpec((1, D_MODEL), lambda i: (0, 0)),
                  pl.BlockSpec(memory_space=pl.ANY)],
        out_specs=pl.BlockSpec((tc, D_MODEL), lambda i: (i, 0)),
        out_shape=jax.ShapeDtypeStruct((n, D_MODEL), F32),
        scratch_shapes=[pltpu.VMEM((2 * tc, D_MODEL), F32), pltpu.SemaphoreType.DMA(())],
        compiler_params=pltpu.CompilerParams(
            dimension_semantics=("arbitrary",), vmem_limit_bytes=_vmem_limit(est)),
        name="combine",
    )(dest1, dest2, route, x1, g, ys)


def _plan_blocks(counts, n_rows):
    counts = counts[0, :N_EXPERTS].astype(jnp.int32)
    padded = (counts + MOE_TILE - 1) // MOE_TILE * MOE_TILE
    pad_end = jnp.cumsum(padded)
    pad_start = pad_end - padded
    blk_start = jnp.arange(n_rows // MOE_TILE, dtype=jnp.int32) * MOE_TILE
    block_e = jnp.minimum(jnp.sum((pad_end[None, :] <= blk_start[:, None]).astype(jnp.int32), axis=1),
                          N_EXPERTS - 1)
    n_active = (pad_end[-1:] // MOE_TILE).astype(jnp.int32)
    pad_start_lanes = jnp.zeros((1, LANES), F32).at[0, :N_EXPERTS].set(pad_start.astype(F32))
    return pad_start_lanes, block_e, n_active


def _forward(x, w_in, diff_lambda, diff_subln, win_sink, w_branch_diff, w_branch_win, w_out,
             rel_bias, norm_mix, norm_ffn, w_router_group, b_router_group, w_router_expert,
             b_router_expert, w_exp_gate, w_exp_up, w_exp_down, norm_final):
    batch, seq, _ = x.shape
    n = batch * seq
    depth = w_in.shape[0]
    assert seq % DIFF_KEY_CHUNK == 0 and seq % DIFF_TQ == 0 and seq >= WIN_BAND
    assert n % ROW_TILE == 0 and n % POS_TILE == 0 and n % MOVE_TILE == 0

    diff_strip = _diff_bias_strip(rel_bias[:, :DIFF_HEADS], DIFF_TQ)
    win_strip = _win_bias_strip(rel_bias[:, DIFF_HEADS:])
    n_rows = -(-(2 * n + N_EXPERTS * (MOE_TILE - 1)) // MOE_TILE) * MOE_TILE
    xs = jnp.zeros((n_rows, D_MODEL), F32)

    x2 = x.reshape(n, D_MODEL)
    moe = None
    for l in range(depth):
        splits = (512, 1024, 1536, 2048, 2176, 2304)
        wdq, wdk, wdv, wwq, wwk, wwv, wgt = jnp.split(w_in[l], splits, axis=-1)
        dup = lambda w: jnp.concatenate(
            [w[:, :WIN_HEAD_DIM], w[:, :WIN_HEAD_DIM], w[:, WIN_HEAD_DIM:], w[:, WIN_HEAD_DIM:]],
            axis=-1)
        w_all = jnp.concatenate([wdq, wdk, wdv, wwq, dup(wwk), dup(wwv), wgt], axis=-1).astype(BF16)

        if moe is None:
            dq, dk, dv, wq, wk, wv, gates = _inproj(x2, norm_mix[l][None], w_all)
        else:
            x2, dq, dk, dv, wq, wk, wv, gates = _combine_inproj(*moe, norm_mix[l][None], w_all)

        lambda_init = 0.8 - 0.6 * math.exp(-0.3 * l)
        o_diff = _diff_attention(dq, dk, dv, diff_lambda[l], diff_strip, diff_subln[l][None],
                                 batch, seq, lambda_init)
        o_win = _win_attention(wq, wk, wv, win_sink[l], win_strip, batch, seq)

        wr = jnp.zeros((D_MODEL, LANES // 2), F32)
        wr = wr.at[:, :N_GROUPS].set(w_router_group[l])
        wr = wr.at[:, N_GROUPS:N_GROUPS + N_EXPERTS].set(w_router_expert[l])
        wr_hi = wr.astype(BF16)
        wr_lo = (wr - wr_hi.astype(F32)).astype(BF16)
        wr = jnp.concatenate([wr_hi, wr_lo], axis=1)
        br = jnp.zeros((1, LANES), F32)
        br = br.at[0, :N_GROUPS].set(b_router_group[l])
        br = br.at[0, N_GROUPS:N_GROUPS + N_EXPERTS].set(b_router_expert[l])
        x1, hn, route, counts = _merge(x2, o_diff, o_win, gates,
                                       w_branch_diff[l].astype(BF16), w_branch_win[l].astype(BF16),
                                       w_out[l].astype(BF16), norm_ffn[l][None], wr, br)

        pad_start, block_e, n_active = _plan_blocks(counts, n_rows)
        dest = _positions(route, pad_start)
        dest1 = dest[:, ROUTE_E1].astype(jnp.int32)
        dest2 = dest[:, ROUTE_E2].astype(jnp.int32)
        xs = _dispatch(dest1, dest2, hn, xs)
        ys = _experts(block_e, n_active, xs, w_exp_gate[l].astype(BF16), w_exp_up[l].astype(BF16),
                      w_exp_down[l].astype(BF16))
        moe = (dest1, dest2, route, x1, ys)
    dest1, dest2, route, x1, ys = moe
    out = _combine(dest1, dest2, route, x1, norm_final[None], ys)
    return out.reshape(batch, seq, D_MODEL)


def kernel(x, w_in, diff_lambda, diff_subln, win_sink, w_branch_diff, w_branch_win, w_out, rel_bias, norm_mix, norm_ffn, w_router_group, b_router_group, w_router_expert, b_router_expert, w_exp_gate, w_exp_up, w_exp_down, norm_final):
    return _forward(x, w_in, diff_lambda, diff_subln, win_sink, w_branch_diff, w_branch_win, w_out,
                    rel_bias, norm_mix, norm_ffn, w_router_group, b_router_group, w_router_expert,
                    b_router_expert, w_exp_gate, w_exp_up, w_exp_down, norm_final)
```

```python
import functools
import math

import jax
import jax.numpy as jnp
from jax import lax
from jax.experimental import pallas as pl
from jax.experimental.pallas import tpu as pltpu

D_MODEL = 1024
DIFF_HEADS = 4
DIFF_HEAD_DIM = 64
DIFF_V_DIM = 2 * DIFF_HEAD_DIM
DIFF_WIDTH = DIFF_HEADS * DIFF_V_DIM
WIN_HEADS = 8
WIN_KV_HEADS = 2
WIN_GROUP = WIN_HEADS // WIN_KV_HEADS
WIN_HEAD_DIM = 64
WIN_WIDTH = WIN_HEADS * WIN_HEAD_DIM
WINDOW = 128
WIN_BLOCK = 128
REL_BUCKETS = 32
REL_MAX_DIST = 128
N_GROUPS = 4
EXPERTS_PER_GROUP = 8
N_EXPERTS = N_GROUPS * EXPERTS_PER_GROUP
EXPERT_HIDDEN = 512
RMS_EPS = 1e-6
NEG_INF = -1e30

LANES = 128
SUBLANES = 8
V7X_VMEM_BYTES = 64 * 1024 * 1024

ROW_TILE = 512
DIFF_TQ = 256
DIFF_TILES_PER_STEP = 4
DIFF_KEY_BLOCK = 128
DIFF_KEY_CHUNK = 512
DIFF_NEAR_BLOCKS = 3
DIFF_NEAR_BLOCKS_NEG = 2
LOG2E = math.log2(math.e)
WIN_BLOCKS_PER_STEP = 4
POS_TILE = 256
MOE_TILE = 256
MOVE_TILE = 256

F32 = jnp.float32
BF16 = jnp.bfloat16


def _vmem_limit(nbytes):
    return int(min(max(2 * nbytes, 16 * 1024 * 1024), V7X_VMEM_BYTES - 8 * 1024 * 1024))


def _rel_bucket(rel):
    half = REL_BUCKETS // 2
    max_exact = half // 2
    n = jnp.abs(rel)
    nf = jnp.maximum(n, max_exact).astype(jnp.float32)
    large = max_exact + (jnp.log(nf / max_exact) / math.log(REL_MAX_DIST / max_exact)
                         * (half - max_exact)).astype(jnp.int32)
    large = jnp.minimum(large, half - 1)
    return jnp.where(rel > 0, half, 0) + jnp.where(n < max_exact, n, large)


C_DQ, C_DK, C_DV, C_WQ = 512, 512, 512, 512
C_WKD, C_WVD = 2 * LANES, 2 * LANES
C_GATE = 2 * D_MODEL
IN_COLS = C_DQ + C_DK + C_DV + C_WQ + C_WKD + C_WVD + C_GATE


def _project(x, g_ref, w_ref, dq_ref, dk_ref, dv_ref, wq_ref, wk_ref, wv_ref, gate_ref):
    ms = jnp.mean(x * x, axis=-1, keepdims=True)
    h = (x * lax.rsqrt(ms + RMS_EPS) * g_ref[...]).astype(BF16)

    col = 0

    def proj(width):
        nonlocal col
        out = jnp.dot(h, w_ref[:, col:col + width], preferred_element_type=F32)
        col += width
        return out

    dq_ref[...] = (proj(C_DQ) * (DIFF_HEAD_DIM ** -0.5 * LOG2E)).astype(BF16)
    dk_ref[...] = proj(C_DK).astype(BF16)
    dv_ref[...] = proj(C_DV).astype(BF16)
    wq_ref[...] = (proj(C_WQ) * (WIN_HEAD_DIM ** -0.5 * LOG2E)).astype(BF16)
    wk_ref[...] = proj(C_WKD).astype(BF16)
    wv_ref[...] = proj(C_WVD).astype(BF16)
    gate_ref[...] = jax.nn.sigmoid(proj(C_GATE)).astype(BF16)


def _inproj_kernel(x_ref, g_ref, w_ref, *out_refs):
    _project(x_ref[...], g_ref, w_ref, *out_refs)


PROJ_WIDTHS = (C_DQ, C_DK, C_DV, C_WQ, C_WKD, C_WVD, C_GATE)


def _inproj(x2, g, w):
    n = x2.shape[0]
    tm = ROW_TILE
    est = 2 * tm * D_MODEL * 4 + D_MODEL * IN_COLS * 2 + 2 * tm * IN_COLS * 2 + tm * IN_COLS * 4
    return pl.pallas_call(
        _inproj_kernel,
        grid=(n // tm,),
        in_specs=[
            pl.BlockSpec((tm, D_MODEL), lambda i: (i, 0)),
            pl.BlockSpec((1, D_MODEL), lambda i: (0, 0)),
            pl.BlockSpec((D_MODEL, IN_COLS), lambda i: (0, 0), pipeline_mode=pl.Buffered(1)),
        ],
        out_specs=[pl.BlockSpec((tm, c), lambda i: (i, 0)) for c in PROJ_WIDTHS],
        out_shape=[jax.ShapeDtypeStruct((n, c), BF16) for c in PROJ_WIDTHS],
        compiler_params=pltpu.CompilerParams(
            dimension_semantics=("parallel",), vmem_limit_bytes=_vmem_limit(est)),
        name="inproj",
    )(x2, g, w)


def _combine_inproj_kernel(d1_next_ref, d2_next_ref, d1_first_ref, d2_first_ref, route_ref, x1_ref,
                           g_ref, w_ref, ys_hbm, x2_ref, *rest):
    out_refs, (buf, sem) = rest[:len(PROJ_WIDTHS)], rest[len(PROJ_WIDTHS):]
    i = pl.program_id(0)
    tm = x1_ref.shape[0]
    slot = i % 2

    def gather(i1_ref, i2_ref, dst):
        for t in range(tm):
            pltpu.make_async_copy(ys_hbm.at[pl.ds(i1_ref[t], 1)], buf.at[dst, pl.ds(t, 1)],
                                  sem.at[dst]).start(priority=0)
            pltpu.make_async_copy(ys_hbm.at[pl.ds(i2_ref[t], 1)], buf.at[dst, pl.ds(tm + t, 1)],
                                  sem.at[dst]).start(priority=1)

    @pl.when(i == 0)
    def _():
        gather(d1_first_ref, d2_first_ref, 0)

    @pl.when(i + 1 < pl.num_programs(0))
    def _():
        gather(d1_next_ref, d2_next_ref, 1 - slot)

    pltpu.make_async_copy(ys_hbm.at[pl.ds(0, 2 * tm)], buf.at[slot], sem.at[slot]).wait()
    r = route_ref[...]
    w1 = r[:, ROUTE_W1:ROUTE_W1 + 1]
    w2 = r[:, ROUTE_W2:ROUTE_W2 + 1]
    x = x1_ref[...] + w1 * buf[slot, 0:tm, :] + w2 * buf[slot, tm:2 * tm, :]
    x2_ref[...] = x
    _project(x, g_ref, w_ref, *out_refs)


def _combine_inproj(dest1, dest2, route, x1, ys, g, w):
    n = x1.shape[0]
    tm = ROW_TILE
    nt = n // tm
    nxt = pl.BlockSpec((tm,), lambda i: (jnp.minimum(i + 1, nt - 1),), memory_space=pltpu.SMEM)
    first = pl.BlockSpec((tm,), lambda i: (0,), memory_space=pltpu.SMEM)
    est = (4 * tm * D_MODEL * 4 + D_MODEL * IN_COLS * 2 + 2 * tm * IN_COLS * 2 + tm * IN_COLS * 4
           + 2 * 2 * tm * D_MODEL * 4)
    return pl.pallas_call(
        _combine_inproj_kernel,
        grid=(nt,),
        in_specs=[
            nxt, nxt, first, first,
            pl.BlockSpec((tm, LANES), lambda i: (i, 0)),
            pl.BlockSpec((tm, D_MODEL), lambda i: (i, 0)),
            pl.BlockSpec((1, D_MODEL), lambda i: (0, 0)),
            pl.BlockSpec((D_MODEL, IN_COLS), lambda i: (0, 0), pipeline_mode=pl.Buffered(1)),
            pl.BlockSpec(memory_space=pl.ANY),
        ],
        out_specs=[pl.BlockSpec((tm, D_MODEL), lambda i: (i, 0))]
        + [pl.BlockSpec((tm, c), lambda i: (i, 0)) for c in PROJ_WIDTHS],
        out_shape=[jax.ShapeDtypeStruct((n, D_MODEL), F32)]
        + [jax.ShapeDtypeStruct((n, c), BF16) for c in PROJ_WIDTHS],
        scratch_shapes=[pltpu.VMEM((2, 2 * tm, D_MODEL), F32), pltpu.SemaphoreType.DMA((2,))],
        compiler_params=pltpu.CompilerParams(
            dimension_semantics=("arbitrary",), vmem_limit_bytes=_vmem_limit(est)),
        name="combine_inproj",
    )(dest1, dest2, dest1, dest2, route, x1, g, w, ys)


def _diff_attn_kernel(lam_ref, q_ref, k_ref, v_ref, bias_ref, g_ref, o_ref, vt_ref, s_ref, p_ref, *,
                      seq, lambda_init):
    qi = pl.program_id(2)

    @pl.when(qi == 0)
    def _():
        vt_ref[...] = v_ref[...].astype(F32).T.astype(BF16)

    lp = lam_ref[...]
    lam = (jnp.exp(jnp.sum(lp[0:1] * lp[1:2], axis=-1, keepdims=True))
           - jnp.exp(jnp.sum(lp[2:3] * lp[3:4], axis=-1, keepdims=True)) + lambda_init)

    tq = DIFF_TQ
    tiles = q_ref.shape[0] // tq
    kb = DIFF_KEY_BLOCK
    groups = kb // SUBLANES
    nkb = seq // kb
    per_chunk = DIFF_KEY_CHUNK // kb
    width = bias_ref.shape[-1]
    row = lax.broadcasted_iota(jnp.int32, (LANES, 1), 0)

    def scores(t):
        qt = q_ref[t * tq:(t + 1) * tq, :].astype(F32).T
        qcat = jnp.concatenate([jnp.where(row < DIFF_HEAD_DIM, qt, 0.0),
                                jnp.where(row >= DIFF_HEAD_DIM, qt, 0.0)], axis=1).astype(BF16)
        q0 = (qi * tiles + t) * tq
        mx = jnp.full((SUBLANES, 2 * tq), NEG_INF, F32)
        for c in range(seq // DIFF_KEY_CHUNK):
            s = jnp.dot(k_ref[c * DIFF_KEY_CHUNK:(c + 1) * DIFF_KEY_CHUNK, :], qcat,
                        preferred_element_type=F32)
            for rr in range(per_chunk):
                r = c * per_chunk + rr
                start = pl.multiple_of(
                    jnp.clip(DIFF_NEAR_BLOCKS * kb - r * kb + q0, 0, width - tq), LANES)
                b = bias_ref[0, :, pl.ds(start, tq)]
                blk = s[rr * kb:(rr + 1) * kb, :]
                blk = jnp.concatenate([blk[:, :tq] + b, blk[:, tq:] + b], axis=1)
                mx = jnp.maximum(mx, jnp.max(blk.reshape(groups, SUBLANES, 2 * tq), axis=0))
                s_ref[t, r * kb:(r + 1) * kb, :] = blk
        return jnp.max(mx, axis=0, keepdims=True)

    def probs(t, m):
        sm = jnp.zeros((SUBLANES, 2 * tq), F32)
        for r in range(nkb):
            p = jnp.exp2(s_ref[t, r * kb:(r + 1) * kb, :] - m)
            sm = sm + jnp.sum(p.reshape(groups, SUBLANES, 2 * tq), axis=0)
            p_ref[t, r * kb:(r + 1) * kb, :] = p.astype(BF16)
        return jnp.sum(sm, axis=0, keepdims=True)

    def values(t, l):
        vt = vt_ref[...]
        acc1 = jnp.dot(vt, p_ref[t, :, :tq], preferred_element_type=F32)
        acc2 = jnp.dot(vt, p_ref[t, :, tq:], preferred_element_type=F32)
        o = (acc1 / l[:, :tq] - lam * (acc2 / l[:, tq:])).T
        ms = jnp.mean(o * o, axis=-1, keepdims=True)
        o = o * lax.rsqrt(ms + RMS_EPS) * g_ref[...] * (1.0 - lambda_init)
        o_ref[t * tq:(t + 1) * tq, :] = o.astype(o_ref.dtype)

    m_next = scores(0)
    for t in range(tiles):
        m = m_next
        if t + 1 < tiles:
            m_next = scores(t + 1)
        values(t, probs(t, m))


def _toeplitz(t, rows, width):
    h, length = t.shape
    assert length == rows + width - 1
    flat = jnp.tile(jnp.pad(t, ((0, 0), (0, 1))), (1, rows))[:, :rows * length]
    return flat.reshape(h, rows, length)[:, :, rows - 1:]


def _diff_bias_strip(rel_diff, tq):
    kb = DIFF_KEY_BLOCK
    assert DIFF_NEAR_BLOCKS * kb - (tq - 1) > REL_MAX_DIST
    assert -DIFF_NEAR_BLOCKS_NEG * kb + kb - 1 < -REL_MAX_DIST
    width = tq + (DIFF_NEAR_BLOCKS + DIFF_NEAR_BLOCKS_NEG) * kb
    rel = jnp.arange(kb + width - 1) - (width - 1) + DIFF_NEAR_BLOCKS * kb
    t = rel_diff[_rel_bucket(rel)].T.astype(F32) * LOG2E
    return _toeplitz(t, kb, width)[:, ::-1, ::-1]


def _diff_attention(dq, dk, dv, lam_p, strip, subln_g, batch, seq, lambda_init):
    tq = DIFF_TQ
    tiles = min(DIFF_TILES_PER_STEP, seq // tq)
    ts = tiles * tq
    nq = seq // ts
    width = strip.shape[-1]
    est = (2 * ts * LANES * 2 * 2 + 2 * 2 * seq * LANES * 2 + 2 * DIFF_KEY_BLOCK * width * 4
           + seq * LANES * 2 + tiles * seq * 2 * tq * (4 + 2) + DIFF_KEY_CHUNK * 2 * tq * 4)
    return pl.pallas_call(
        functools.partial(_diff_attn_kernel, seq=seq, lambda_init=lambda_init),
        grid=(batch, DIFF_HEADS, nq),
        in_specs=[
            pl.BlockSpec((4, DIFF_HEAD_DIM), lambda b, h, i: (0, 0)),
            pl.BlockSpec((ts, LANES), lambda b, h, i: (b * nq + i, h)),
            pl.BlockSpec((seq, LANES), lambda b, h, i: (b, h)),
            pl.BlockSpec((seq, LANES), lambda b, h, i: (b, h)),
            pl.BlockSpec((1, DIFF_KEY_BLOCK, width), lambda b, h, i: (h, 0, 0)),
            pl.BlockSpec((1, DIFF_V_DIM), lambda b, h, i: (0, 0)),
        ],
        out_specs=pl.BlockSpec((ts, LANES), lambda b, h, i: (b * nq + i, h)),
        out_shape=jax.ShapeDtypeStruct((batch * seq, DIFF_WIDTH), BF16),
        scratch_shapes=[pltpu.VMEM((DIFF_V_DIM, seq), BF16),
                        pltpu.VMEM((tiles, seq, 2 * tq), F32),
                        pltpu.VMEM((tiles, seq, 2 * tq), BF16)],
        compiler_params=pltpu.CompilerParams(
            dimension_semantics=("parallel", "parallel", "arbitrary"),
            vmem_limit_bytes=_vmem_limit(est)),
        name="diff_attn",
    )(lam_p, dq, dk, dv, strip, subln_g)


WIN_BAND = 3 * WIN_BLOCK
WIN_STRIP = 5 * WIN_BLOCK


def _win_attn_kernel(q_ref, k_ref, v_ref, strip_ref, sink_ref, o_ref, *, seq):
    row = lax.broadcasted_iota(jnp.int32, (LANES, 1), 0)
    lane = lax.broadcasted_iota(jnp.int32, (1, LANES), 1)
    blocks = q_ref.shape[0] // WIN_BLOCK
    for sub in range(blocks):
        _win_block(pl.program_id(1) * blocks + sub, sub, row, lane, q_ref, k_ref, v_ref, strip_ref,
                   sink_ref, o_ref, seq)


def _win_block(n, sub, row, lane, q_ref, k_ref, v_ref, strip_ref, sink_ref, o_ref, seq):
    rows = slice(sub * WIN_BLOCK, (sub + 1) * WIN_BLOCK)
    start = pl.multiple_of(jnp.clip(n * WIN_BLOCK - WIN_BLOCK, 0, seq - WIN_BAND), WIN_BLOCK)
    ustart = pl.multiple_of(start - n * WIN_BLOCK + 2 * WIN_BLOCK, WIN_BLOCK)
    for kv in range(WIN_KV_HEADS):
        kb = k_ref[pl.ds(start, WIN_BAND), kv * LANES:(kv + 1) * LANES]
        vb = v_ref[pl.ds(start, WIN_BAND), kv * LANES:(kv + 1) * LANES]
        cols = []
        for pair in range(WIN_GROUP // 2):
            c0 = kv * WIN_GROUP * WIN_HEAD_DIM + pair * LANES
            qt = q_ref[rows, c0:c0 + LANES].astype(F32).T
            cols.append(jnp.where(row < WIN_HEAD_DIM, qt, 0.0))
            cols.append(jnp.where(row >= WIN_HEAD_DIM, qt, 0.0))
        qcat = jnp.concatenate(cols, axis=1).astype(BF16)
        s = jnp.dot(kb, qcat, preferred_element_type=F32)
        s = s + strip_ref[kv, pl.ds(ustart, WIN_BAND), :]
        sink = sink_ref[kv]
        m = jnp.maximum(jnp.max(s, axis=0, keepdims=True), sink)
        p = jnp.exp2(s - m)
        den = jnp.sum(p, axis=0, keepdims=True) + jnp.exp2(sink - m)
        vt = vb.astype(F32).T.astype(BF16)
        o = (jnp.dot(vt, p.astype(BF16), preferred_element_type=F32) / den).T
        for pair in range(WIN_GROUP // 2):
            c0 = kv * WIN_GROUP * WIN_HEAD_DIM + pair * LANES
            even = o[(2 * pair) * WIN_BLOCK:(2 * pair + 1) * WIN_BLOCK, :]
            odd = o[(2 * pair + 1) * WIN_BLOCK:(2 * pair + 2) * WIN_BLOCK, :]
            o_ref[rows, c0:c0 + LANES] = jnp.where(lane < WIN_HEAD_DIM, even, odd).astype(o_ref.dtype)


def _win_bias_strip(rel_win):
    rel = jnp.arange(WIN_BLOCK + WIN_STRIP - 1) - (WIN_BLOCK - 1) - 2 * WIN_BLOCK
    t = jnp.where((jnp.abs(rel) <= WINDOW)[None],
                  rel_win[_rel_bucket(rel)].T.astype(F32) * LOG2E, NEG_INF)
    strip = _toeplitz(t, WIN_BLOCK, WIN_STRIP)
    strip = strip.reshape(WIN_KV_HEADS, WIN_GROUP, WIN_BLOCK, WIN_STRIP).transpose(0, 3, 1, 2)
    return strip.reshape(WIN_KV_HEADS, WIN_STRIP, WIN_GROUP * WIN_BLOCK)


def _win_attention(wq, wk, wv, sink, strip, batch, seq):
    heads_lanes = WIN_GROUP * WIN_BLOCK
    sink_lanes = jnp.broadcast_to((sink.astype(F32) * LOG2E).reshape(WIN_KV_HEADS, WIN_GROUP, 1),
                                  (WIN_KV_HEADS, WIN_GROUP, WIN_BLOCK)).reshape(
                                      WIN_KV_HEADS, 1, heads_lanes)
    rows = WIN_BLOCKS_PER_STEP * WIN_BLOCK
    assert seq % rows == 0
    nb = seq // rows
    est = (2 * rows * WIN_WIDTH * 2 * 2 + 2 * 2 * seq * 2 * LANES * 2
           + 2 * WIN_KV_HEADS * WIN_STRIP * heads_lanes * 4 + 8 * WIN_BAND * heads_lanes * 4)
    return pl.pallas_call(
        functools.partial(_win_attn_kernel, seq=seq),
        grid=(batch, nb),
        in_specs=[
            pl.BlockSpec((rows, WIN_WIDTH), lambda b, n: (b * nb + n, 0)),
            pl.BlockSpec((seq, 2 * LANES), lambda b, n: (b, 0)),
            pl.BlockSpec((seq, 2 * LANES), lambda b, n: (b, 0)),
            pl.BlockSpec((WIN_KV_HEADS, WIN_STRIP, heads_lanes), lambda b, n: (0, 0, 0)),
            pl.BlockSpec((WIN_KV_HEADS, 1, heads_lanes), lambda b, n: (0, 0, 0)),
        ],
        out_specs=pl.BlockSpec((rows, WIN_WIDTH), lambda b, n: (b * nb + n, 0)),
        out_shape=jax.ShapeDtypeStruct((batch * seq, WIN_WIDTH), BF16),
        compiler_params=pltpu.CompilerParams(
            dimension_semantics=("parallel", "parallel"), vmem_limit_bytes=_vmem_limit(est)),
        name="win_attn",
    )(wq, wk, wv, strip, sink_lanes)


ROUTE_E1, ROUTE_E2, ROUTE_W1, ROUTE_W2 = 0, 1, 2, 3


def _merge_kernel(x_ref, od_ref, ow_ref, gate_ref, pd_ref, pw_ref, wo_ref, g_ref, wr_ref, br_ref,
                  x1_ref, hn_ref, route_ref, counts_ref):
    @pl.when(pl.program_id(0) == 0)
    def _():
        counts_ref[...] = jnp.zeros_like(counts_ref)

    md = jnp.dot(od_ref[...], pd_ref[...], preferred_element_type=F32)
    mw = jnp.dot(ow_ref[...], pw_ref[...], preferred_element_type=F32)
    merged = (gate_ref[:, :D_MODEL].astype(F32) * md + gate_ref[:, D_MODEL:].astype(F32) * mw)
    x1 = x_ref[...] + jnp.dot(merged.astype(BF16), wo_ref[...], preferred_element_type=F32)
    x1_ref[...] = x1
    ms = jnp.mean(x1 * x1, axis=-1, keepdims=True)
    hn = x1 * lax.rsqrt(ms + RMS_EPS) * g_ref[...]
    hn_ref[...] = hn

    hn_hi = hn.astype(BF16)
    hn_lo = (hn - hn_hi.astype(F32)).astype(BF16)
    parts = (jnp.dot(hn_hi, wr_ref[...], preferred_element_type=F32)
             + jnp.dot(hn_lo, wr_ref[...], preferred_element_type=F32))
    logits = parts + pltpu.roll(parts, LANES // 2, axis=1) + br_ref[...]
    lane = lax.broadcasted_iota(jnp.int32, logits.shape, 1)
    big = jnp.int32(LANES)
    is_group = lane < N_GROUPS
    gl = jnp.where(is_group, logits, NEG_INF)
    gmax = jnp.max(gl, axis=-1, keepdims=True)
    gsel = jnp.min(jnp.where(gl == gmax, lane, big), axis=-1, keepdims=True)
    gden = jnp.sum(jnp.where(is_group, jnp.exp(gl - gmax), 0.0), axis=-1, keepdims=True)
    gw = 1.0 / gden
    lo = N_GROUPS + EXPERTS_PER_GROUP * gsel
    in_group = jnp.logical_and(lane >= lo, lane < lo + EXPERTS_PER_GROUP)
    el = jnp.where(in_group, logits, NEG_INF)
    t1 = jnp.max(el, axis=-1, keepdims=True)
    i1 = jnp.min(jnp.where(el == t1, lane, big), axis=-1, keepdims=True)
    el2 = jnp.where(lane == i1, NEG_INF, el)
    t2 = jnp.max(el2, axis=-1, keepdims=True)
    i2 = jnp.min(jnp.where(el2 == t2, lane, big), axis=-1, keepdims=True)
    z = jnp.exp(t2 - t1)
    w1 = gw / (1.0 + z)
    w2 = gw * z / (1.0 + z)
    e1 = (i1 - N_GROUPS).astype(F32)
    e2 = (i2 - N_GROUPS).astype(F32)
    route = jnp.where(lane == ROUTE_E1, e1,
                      jnp.where(lane == ROUTE_E2, e2,
                                jnp.where(lane == ROUTE_W1, w1,
                                          jnp.where(lane == ROUTE_W2, w2, 0.0))))
    route_ref[...] = route
    hits = (jnp.where(lane == i1 - N_GROUPS, 1.0, 0.0) + jnp.where(lane == i2 - N_GROUPS, 1.0, 0.0))
    counts_ref[...] += jnp.sum(hits, axis=0, keepdims=True)


def _merge(x2, od, ow, gates, pd, pw, wo, g, wr, br):
    n = x2.shape[0]
    tm = ROW_TILE
    est = (2 * tm * D_MODEL * 4 * 3 + 2 * tm * (DIFF_WIDTH + WIN_WIDTH + C_GATE) * 2
           + 2 * (DIFF_WIDTH + WIN_WIDTH + D_MODEL) * D_MODEL * 2 + D_MODEL * LANES * 4 * 2
           + 6 * tm * D_MODEL * 4)
    row = lambda c: pl.BlockSpec((tm, c), lambda i: (i, 0))
    full = lambda r, c: pl.BlockSpec((r, c), lambda i: (0, 0))
    return pl.pallas_call(
        _merge_kernel,
        grid=(n // tm,),
        in_specs=[row(D_MODEL), row(DIFF_WIDTH), row(WIN_WIDTH), row(C_GATE),
                  full(DIFF_WIDTH, D_MODEL), full(WIN_WIDTH, D_MODEL), full(D_MODEL, D_MODEL),
                  full(1, D_MODEL), full(D_MODEL, LANES), full(1, LANES)],
        out_specs=[row(D_MODEL), row(D_MODEL), row(LANES), full(1, LANES)],
        out_shape=[jax.ShapeDtypeStruct((n, D_MODEL), F32), jax.ShapeDtypeStruct((n, D_MODEL), F32),
                   jax.ShapeDtypeStruct((n, LANES), F32), jax.ShapeDtypeStruct((1, LANES), F32)],
        compiler_params=pltpu.CompilerParams(
            dimension_semantics=("arbitrary",), vmem_limit_bytes=_vmem_limit(est)),
        name="merge_route",
    )(x2, od, ow, gates, pd, pw, wo, g, wr, br)


def _positions_kernel(route_ref, pstart_ref, dest_ref, carry_ref):
    i = pl.program_id(0)

    @pl.when(i == 0)
    def _():
        carry_ref[...] = jnp.zeros_like(carry_ref)

    r = route_ref[...]
    tb = r.shape[0]
    lane = lax.broadcasted_iota(jnp.int32, r.shape, 1)
    lane_f = lane.astype(F32)
    oh1 = lane_f == r[:, ROUTE_E1:ROUTE_E1 + 1]
    oh2 = lane_f == r[:, ROUTE_E2:ROUTE_E2 + 1]
    cnt = jnp.where(oh1, 1.0, 0.0) + jnp.where(oh2, 1.0, 0.0)
    rows = lax.broadcasted_iota(jnp.int32, (tb, tb), 0)
    cols = lax.broadcasted_iota(jnp.int32, (tb, tb), 1)
    tri = jnp.where(rows >= cols, 1.0, 0.0).astype(BF16)
    incl = jnp.dot(tri, cnt.astype(BF16), preferred_element_type=F32)
    row_of = incl - cnt + carry_ref[...] + pstart_ref[...]
    dest1 = jnp.sum(jnp.where(oh1, row_of, 0.0), axis=-1, keepdims=True)
    dest2 = jnp.sum(jnp.where(oh2, row_of, 0.0), axis=-1, keepdims=True)
    dest_ref[...] = jnp.where(lane == ROUTE_E1, dest1, jnp.where(lane == ROUTE_E2, dest2, 0.0))
    carry_ref[...] = carry_ref[...] + incl[tb - 1:tb, :]


def _positions(route, pad_start):
    n = route.shape[0]
    tb = POS_TILE
    return pl.pallas_call(
        _positions_kernel,
        grid=(n // tb,),
        in_specs=[pl.BlockSpec((tb, LANES), lambda i: (i, 0)),
                  pl.BlockSpec((1, LANES), lambda i: (0, 0))],
        out_specs=pl.BlockSpec((tb, LANES), lambda i: (i, 0)),
        out_shape=jax.ShapeDtypeStruct((n, LANES), F32),
        scratch_shapes=[pltpu.VMEM((1, LANES), F32)],
        compiler_params=pltpu.CompilerParams(dimension_semantics=("arbitrary",)),
        name="positions",
    )(route, pad_start)


def _dispatch_kernel(d1_ref, d2_ref, hn_ref, xs_init_hbm, xs_hbm, sem):
    del xs_init_hbm
    ts = hn_ref.shape[0]
    for t in range(ts):
        pltpu.make_async_copy(hn_ref.at[pl.ds(t, 1)], xs_hbm.at[pl.ds(d1_ref[t], 1)],
                              sem).start(priority=0)
        pltpu.make_async_copy(hn_ref.at[pl.ds(t, 1)], xs_hbm.at[pl.ds(d2_ref[t], 1)],
                              sem).start(priority=1)
    for _ in range(2):
        pltpu.make_async_copy(hn_ref, xs_hbm.at[pl.ds(0, ts)], sem).wait()


def _dispatch(dest1, dest2, hn, xs_init):
    n = hn.shape[0]
    ts = MOVE_TILE
    smem_blk = pl.BlockSpec((ts,), lambda i: (i,), memory_space=pltpu.SMEM)
    return pl.pallas_call(
        _dispatch_kernel,
        grid=(n // ts,),
        in_specs=[smem_blk, smem_blk,
                  pl.BlockSpec((ts, D_MODEL), lambda i: (i, 0)), pl.BlockSpec(memory_space=pl.ANY)],
        out_specs=pl.BlockSpec(memory_space=pl.ANY),
        out_shape=jax.ShapeDtypeStruct(xs_init.shape, F32),
        scratch_shapes=[pltpu.SemaphoreType.DMA(())],
        input_output_aliases={3: 0},
        compiler_params=pltpu.CompilerParams(dimension_semantics=("arbitrary",)),
        name="dispatch",
    )(dest1, dest2, hn, xs_init)


def _experts_kernel(be_ref, na_ref, xs_ref, wg_ref, wu_ref, wd_ref, ys_ref, wg_bf, wu_bf, wd_bf):
    j = pl.program_id(0)
    active = j < na_ref[0]
    new_expert = jnp.logical_or(j == 0, be_ref[j] != be_ref[jnp.maximum(j - 1, 0)])

    @pl.when(jnp.logical_and(active, new_expert))
    def _():
        wg_bf[...] = wg_ref[0, 0].astype(BF16)
        wu_bf[...] = wu_ref[0, 0].astype(BF16)
        wd_bf[...] = wd_ref[0, 0].astype(BF16)

    @pl.when(active)
    def _():
        x = xs_ref[...].astype(BF16)
        g = jnp.dot(x, wg_bf[...], preferred_element_type=F32)
        u = jnp.dot(x, wu_bf[...], preferred_element_type=F32)
        hid = (g * jax.nn.sigmoid(g) * u).astype(BF16)
        ys_ref[...] = jnp.dot(hid, wd_bf[...], preferred_element_type=F32)

    @pl.when(j >= na_ref[0])
    def _():
        ys_ref[...] = jnp.zeros_like(ys_ref)


def _experts(block_e, n_active, xs, wg, wu, wd, layer):
    n_rows = xs.shape[0]
    tmb = MOE_TILE
    nblk = n_rows // tmb
    rows = pl.BlockSpec((tmb, D_MODEL), lambda j, be, na: (j, 0))
    est = (4 * tmb * D_MODEL * 4 + 3 * D_MODEL * EXPERT_HIDDEN * (2 * 4 + 2)
           + 4 * tmb * EXPERT_HIDDEN * 4 + tmb * D_MODEL * 4)
    return pl.pallas_call(
        _experts_kernel,
        grid_spec=pltpu.PrefetchScalarGridSpec(
            num_scalar_prefetch=2,
            grid=(nblk,),
            in_specs=[
                rows,
                pl.BlockSpec((1, 1, D_MODEL, EXPERT_HIDDEN), lambda j, be, na: (layer, be[j], 0, 0)),
                pl.BlockSpec((1, 1, D_MODEL, EXPERT_HIDDEN), lambda j, be, na: (layer, be[j], 0, 0)),
                pl.BlockSpec((1, 1, EXPERT_HIDDEN, D_MODEL), lambda j, be, na: (layer, be[j], 0, 0)),
            ],
            out_specs=rows,
            scratch_shapes=[pltpu.VMEM((D_MODEL, EXPERT_HIDDEN), BF16),
                            pltpu.VMEM((D_MODEL, EXPERT_HIDDEN), BF16),
                            pltpu.VMEM((EXPERT_HIDDEN, D_MODEL), BF16)],
        ),
        out_shape=jax.ShapeDtypeStruct((n_rows, D_MODEL), F32),
        compiler_params=pltpu.CompilerParams(
            dimension_semantics=("arbitrary",), vmem_limit_bytes=_vmem_limit(est)),
        name="experts",
    )(block_e, n_active, xs, wg, wu, wd)


def _combine_kernel(d1_ref, d2_ref, route_ref, x_ref, g_ref, ys_hbm, o_ref, buf, sem):
    tc = d1_ref.shape[0]
    for t in range(tc):
        pltpu.make_async_copy(ys_hbm.at[pl.ds(d1_ref[t], 1)], buf.at[pl.ds(t, 1)],
                              sem).start(priority=0)
        pltpu.make_async_copy(ys_hbm.at[pl.ds(d2_ref[t], 1)], buf.at[pl.ds(tc + t, 1)],
                              sem).start(priority=1)
    pltpu.make_async_copy(ys_hbm.at[pl.ds(0, 2 * tc)], buf, sem).wait()

    r = route_ref[...]
    w1 = r[:, ROUTE_W1:ROUTE_W1 + 1]
    w2 = r[:, ROUTE_W2:ROUTE_W2 + 1]
    out = x_ref[...] + w1 * buf[0:tc, :] + w2 * buf[tc:2 * tc, :]
    ms = jnp.mean(out * out, axis=-1, keepdims=True)
    o_ref[...] = out * lax.rsqrt(ms + RMS_EPS) * g_ref[...]


def _combine(dest1, dest2, route, x1, g, ys):
    n = x1.shape[0]
    tc = MOVE_TILE
    smem_blk = pl.BlockSpec((tc,), lambda i: (i,), memory_space=pltpu.SMEM)
    est = 2 * tc * D_MODEL * 4 * 2 + 2 * tc * D_MODEL * 4 + 2 * tc * LANES * 4 + 4 * tc * D_MODEL * 4
    return pl.pallas_call(
        _combine_kernel,
        grid=(n // tc,),
        in_specs=[smem_blk, smem_blk,
                  pl.BlockSpec((tc, LANES), lambda i: (i, 0)),
                  pl.BlockSpec((tc, D_MODEL), lambda i: (i, 0)),
                  pl.BlockSpec((1, D_MODEL), lambda i: (0, 0)),
                  pl.BlockSpec(memory_space=pl.ANY)],
        out_specs=pl.BlockSpec((tc, D_MODEL), lambda i: (i, 0)),
        out_shape=jax.ShapeDtypeStruct((n, D_MODEL), F32),
        scratch_shapes=[pltpu.VMEM((2 * tc, D_MODEL), F32), pltpu.SemaphoreType.DMA(())],
        compiler_params=pltpu.CompilerParams(
            dimension_semantics=("arbitrary",), vmem_limit_bytes=_vmem_limit(est)),
        name="combine",
    )(dest1, dest2, route, x1, g, ys)


def _plan_blocks(counts, n_rows):
    counts = counts[0, :N_EXPERTS].astype(jnp.int32)
    padded = (counts + MOE_TILE - 1) // MOE_TILE * MOE_TILE
    pad_end = jnp.cumsum(padded)
    pad_start = pad_end - padded
    blk_start = jnp.arange(n_rows // MOE_TILE, dtype=jnp.int32) * MOE_TILE
    block_e = jnp.minimum(jnp.sum((pad_end[None, :] <= blk_start[:, None]).astype(jnp.int32), axis=1),
                          N_EXPERTS - 1)
    n_active = (pad_end[-1:] // MOE_TILE).astype(jnp.int32)
    pad_start_lanes = jnp.zeros((1, LANES), F32).at[0, :N_EXPERTS].set(pad_start.astype(F32))
    return pad_start_lanes, block_e, n_active


def _forward(x, w_in, diff_lambda, diff_subln, win_sink, w_branch_diff, w_branch_win, w_out,
             rel_bias, norm_mix, norm_ffn, w_router_group, b_router_group, w_router_expert,
             b_router_expert, w_exp_gate, w_exp_up, w_exp_down, norm_final):
    batch, seq, _ = x.shape
    n = batch * seq
    depth = w_in.shape[0]
    assert seq % DIFF_KEY_CHUNK == 0 and seq % DIFF_TQ == 0 and seq >= WIN_BAND
    assert n % ROW_TILE == 0 and n % POS_TILE == 0 and n % MOVE_TILE == 0

    diff_strip = _diff_bias_strip(rel_bias[:, :DIFF_HEADS], DIFF_TQ)
    win_strip = _win_bias_strip(rel_bias[:, DIFF_HEADS:])
    n_rows = -(-(2 * n + N_EXPERTS * (MOE_TILE - 1)) // MOE_TILE) * MOE_TILE
    xs = jnp.zeros((n_rows, D_MODEL), F32)

    x2 = x.reshape(n, D_MODEL)
    moe = None
    for l in range(depth):
        splits = (512, 1024, 1536, 2048, 2176, 2304)
        wdq, wdk, wdv, wwq, wwk, wwv, wgt = jnp.split(w_in[l], splits, axis=-1)
        dup = lambda w: jnp.concatenate(
            [w[:, :WIN_HEAD_DIM], w[:, :WIN_HEAD_DIM], w[:, WIN_HEAD_DIM:], w[:, WIN_HEAD_DIM:]],
            axis=-1)
        w_all = jnp.concatenate([wdq, wdk, wdv, wwq, dup(wwk), dup(wwv), wgt], axis=-1).astype(BF16)

        if moe is None:
            dq, dk, dv, wq, wk, wv, gates = _inproj(x2, norm_mix[l][None], w_all)
        else:
            x2, dq, dk, dv, wq, wk, wv, gates = _combine_inproj(*moe, norm_mix[l][None], w_all)

        lambda_init = 0.8 - 0.6 * math.exp(-0.3 * l)
        o_diff = _diff_attention(dq, dk, dv, diff_lambda[l], diff_strip, diff_subln[l][None],
                                 batch, seq, lambda_init)
        o_win = _win_attention(wq, wk, wv, win_sink[l], win_strip, batch, seq)

        wr = jnp.zeros((D_MODEL, LANES // 2), F32)
        wr = wr.at[:, :N_GROUPS].set(w_router_group[l])
        wr = wr.at[:, N_GROUPS:N_GROUPS + N_EXPERTS].set(w_router_expert[l])
        wr_hi = wr.astype(BF16)
        wr_lo = (wr - wr_hi.astype(F32)).astype(BF16)
        wr = jnp.concatenate([wr_hi, wr_lo], axis=1)
        br = jnp.zeros((1, LANES), F32)
        br = br.at[0, :N_GROUPS].set(b_router_group[l])
        br = br.at[0, N_GROUPS:N_GROUPS + N_EXPERTS].set(b_router_expert[l])
        x1, hn, route, counts = _merge(x2, o_diff, o_win, gates,
                                       w_branch_diff[l].astype(BF16), w_branch_win[l].astype(BF16),
                                       w_out[l].astype(BF16), norm_ffn[l][None], wr, br)

        pad_start, block_e, n_active = _plan_blocks(counts, n_rows)
        dest = _positions(route, pad_start)
        dest1 = dest[:, ROUTE_E1].astype(jnp.int32)
        dest2 = dest[:, ROUTE_E2].astype(jnp.int32)
        xs = _dispatch(dest1, dest2, hn, xs)
        ys = _experts(block_e, n_active, xs, w_exp_gate, w_exp_up, w_exp_down, l)
        moe = (dest1, dest2, route, x1, ys)
    dest1, dest2, route, x1, ys = moe
    out = _combine(dest1, dest2, route, x1, norm_final[None], ys)
    return out.reshape(batch, seq, D_MODEL)


def kernel(x, w_in, diff_lambda, diff_subln, win_sink, w_branch_diff, w_branch_win, w_out, rel_bias, norm_mix, norm_ffn, w_router_group, b_router_group, w_router_expert, b_router_expert, w_exp_gate, w_exp_up, w_exp_down, norm_final):
    return _forward(x, w_in, diff_lambda, diff_subln, win_sink, w_branch_diff, w_branch_win, w_out,
                    rel_bias, norm_mix, norm_ffn, w_router_group, b_router_group, w_router_expert,
                    b_router_expert, w_exp_gate, w_exp_up, w_exp_down, norm_final)
```

```python
import functools
import math

import jax
import jax.numpy as jnp
from jax import lax
from jax.experimental import pallas as pl
from jax.experimental.pallas import tpu as pltpu

D_MODEL = 1024
DIFF_HEADS = 4
DIFF_HEAD_DIM = 64
DIFF_V_DIM = 2 * DIFF_HEAD_DIM
DIFF_WIDTH = DIFF_HEADS * DIFF_V_DIM
WIN_HEADS = 8
WIN_KV_HEADS = 2
WIN_GROUP = WIN_HEADS // WIN_KV_HEADS
WIN_HEAD_DIM = 64
WIN_WIDTH = WIN_HEADS * WIN_HEAD_DIM
WINDOW = 128
WIN_BLOCK = 128
REL_BUCKETS = 32
REL_MAX_DIST = 128
N_GROUPS = 4
EXPERTS_PER_GROUP = 8
N_EXPERTS = N_GROUPS * EXPERTS_PER_GROUP
EXPERT_HIDDEN = 512
RMS_EPS = 1e-6
NEG_INF = -1e30

LANES = 128
SUBLANES = 8
V7X_VMEM_BYTES = 64 * 1024 * 1024

ROW_TILE = 512
MERGE_SUBTILES = 2
DIFF_TQ = 256
DIFF_TILES_PER_STEP = 8
DIFF_KEY_BLOCK = 128
DIFF_KEY_CHUNK = 512
DIFF_VALUE_CHUNK = 256
DIFF_NEAR_BLOCKS = 3
DIFF_NEAR_BLOCKS_NEG = 2
LOG2E = math.log2(math.e)
WIN_BLOCKS_PER_STEP = 4
POS_TILE = 256
MOE_TILE = 256
MOVE_TILE = 256

F32 = jnp.float32
BF16 = jnp.bfloat16


def _vmem_limit(nbytes):
    return int(min(max(2 * nbytes, 16 * 1024 * 1024), V7X_VMEM_BYTES - 8 * 1024 * 1024))


def _rel_bucket(rel):
    half = REL_BUCKETS // 2
    max_exact = half // 2
    n = jnp.abs(rel)
    nf = jnp.maximum(n, max_exact).astype(jnp.float32)
    large = max_exact + (jnp.log(nf / max_exact) / math.log(REL_MAX_DIST / max_exact)
                         * (half - max_exact)).astype(jnp.int32)
    large = jnp.minimum(large, half - 1)
    return jnp.where(rel > 0, half, 0) + jnp.where(n < max_exact, n, large)


C_DQ, C_DK, C_DV, C_WQ = 512, 512, 512, 512
C_WKD, C_WVD = 2 * LANES, 2 * LANES
C_GATE = 2 * D_MODEL
IN_COLS = C_DQ + C_DK + C_DV + C_WQ + C_WKD + C_WVD + C_GATE


def _project(x, g_ref, w_ref, dq_ref, dk_ref, dv_ref, wq_ref, wk_ref, wv_ref, gate_ref):
    ms = jnp.mean(x * x, axis=-1, keepdims=True)
    h = (x * lax.rsqrt(ms + RMS_EPS) * g_ref[...]).astype(BF16)

    col = 0

    def proj(width):
        nonlocal col
        out = jnp.dot(h, w_ref[:, col:col + width], preferred_element_type=F32)
        col += width
        return out

    dq_ref[...] = (proj(C_DQ) * (DIFF_HEAD_DIM ** -0.5 * LOG2E)).astype(BF16)
    dk_ref[...] = proj(C_DK).astype(BF16)
    dv_ref[...] = proj(C_DV).astype(BF16)
    wq_ref[...] = (proj(C_WQ) * (WIN_HEAD_DIM ** -0.5 * LOG2E)).astype(BF16)
    wk_ref[...] = proj(C_WKD).astype(BF16)
    wv_ref[...] = proj(C_WVD).astype(BF16)
    gate_ref[...] = jax.nn.sigmoid(proj(C_GATE)).astype(BF16)


def _inproj_kernel(x_ref, g_ref, w_ref, *out_refs):
    _project(x_ref[...], g_ref, w_ref, *out_refs)


PROJ_WIDTHS = (C_DQ, C_DK, C_DV, C_WQ, C_WKD, C_WVD, C_GATE)


def _inproj(x2, g, w):
    n = x2.shape[0]
    tm = ROW_TILE
    est = 2 * tm * D_MODEL * 4 + D_MODEL * IN_COLS * 2 + 2 * tm * IN_COLS * 2 + tm * IN_COLS * 4
    return pl.pallas_call(
        _inproj_kernel,
        grid=(n // tm,),
        in_specs=[
            pl.BlockSpec((tm, D_MODEL), lambda i: (i, 0)),
            pl.BlockSpec((1, D_MODEL), lambda i: (0, 0)),
            pl.BlockSpec((D_MODEL, IN_COLS), lambda i: (0, 0), pipeline_mode=pl.Buffered(1)),
        ],
        out_specs=[pl.BlockSpec((tm, c), lambda i: (i, 0)) for c in PROJ_WIDTHS],
        out_shape=[jax.ShapeDtypeStruct((n, c), BF16) for c in PROJ_WIDTHS],
        compiler_params=pltpu.CompilerParams(
            dimension_semantics=("parallel",), vmem_limit_bytes=_vmem_limit(est)),
        name="inproj",
    )(x2, g, w)


def _combine_inproj_kernel(d1_next_ref, d2_next_ref, d1_first_ref, d2_first_ref, route_ref, x1_ref,
                           g_ref, w_ref, ys_hbm, x2_ref, *rest):
    out_refs, (buf, sem) = rest[:len(PROJ_WIDTHS)], rest[len(PROJ_WIDTHS):]
    i = pl.program_id(0)
    tm = x1_ref.shape[0]
    slot = i % 2

    def gather(i1_ref, i2_ref, dst):
        for t in range(tm):
            pltpu.make_async_copy(ys_hbm.at[pl.ds(i1_ref[t], 1)], buf.at[dst, pl.ds(t, 1)],
                                  sem.at[dst]).start(priority=0)
            pltpu.make_async_copy(ys_hbm.at[pl.ds(i2_ref[t], 1)], buf.at[dst, pl.ds(tm + t, 1)],
                                  sem.at[dst]).start(priority=1)

    @pl.when(i == 0)
    def _():
        gather(d1_first_ref, d2_first_ref, 0)

    @pl.when(i + 1 < pl.num_programs(0))
    def _():
        gather(d1_next_ref, d2_next_ref, 1 - slot)

    pltpu.make_async_copy(ys_hbm.at[pl.ds(0, 2 * tm)], buf.at[slot], sem.at[slot]).wait()
    r = route_ref[...]
    w1 = r[:, ROUTE_W1:ROUTE_W1 + 1]
    w2 = r[:, ROUTE_W2:ROUTE_W2 + 1]
    x = x1_ref[...] + w1 * buf[slot, 0:tm, :] + w2 * buf[slot, tm:2 * tm, :]
    x2_ref[...] = x
    _project(x, g_ref, w_ref, *out_refs)


def _combine_inproj(dest1, dest2, route, x1, ys, g, w):
    n = x1.shape[0]
    tm = ROW_TILE
    nt = n // tm
    nxt = pl.BlockSpec((tm,), lambda i: (jnp.minimum(i + 1, nt - 1),), memory_space=pltpu.SMEM)
    first = pl.BlockSpec((tm,), lambda i: (0,), memory_space=pltpu.SMEM)
    est = (4 * tm * D_MODEL * 4 + D_MODEL * IN_COLS * 2 + 2 * tm * IN_COLS * 2 + tm * IN_COLS * 4
           + 2 * 2 * tm * D_MODEL * 4)
    return pl.pallas_call(
        _combine_inproj_kernel,
        grid=(nt,),
        in_specs=[
            nxt, nxt, first, first,
            pl.BlockSpec((tm, LANES), lambda i: (i, 0)),
            pl.BlockSpec((tm, D_MODEL), lambda i: (i, 0)),
            pl.BlockSpec((1, D_MODEL), lambda i: (0, 0)),
            pl.BlockSpec((D_MODEL, IN_COLS), lambda i: (0, 0), pipeline_mode=pl.Buffered(1)),
            pl.BlockSpec(memory_space=pl.ANY),
        ],
        out_specs=[pl.BlockSpec((tm, D_MODEL), lambda i: (i, 0))]
        + [pl.BlockSpec((tm, c), lambda i: (i, 0)) for c in PROJ_WIDTHS],
        out_shape=[jax.ShapeDtypeStruct((n, D_MODEL), F32)]
        + [jax.ShapeDtypeStruct((n, c), BF16) for c in PROJ_WIDTHS],
        scratch_shapes=[pltpu.VMEM((2, 2 * tm, D_MODEL), F32), pltpu.SemaphoreType.DMA((2,))],
        compiler_params=pltpu.CompilerParams(
            dimension_semantics=("arbitrary",), vmem_limit_bytes=_vmem_limit(est)),
        name="combine_inproj",
    )(dest1, dest2, dest1, dest2, route, x1, g, w, ys)


def _diff_attn_kernel(lam_ref, q_ref, k_ref, v_ref, bias_ref, g_ref, o_ref, vt_ref, s_ref, *,
                      seq, lambda_init):
    qi = pl.program_id(2)

    @pl.when(qi == 0)
    def _():
        vt_ref[...] = v_ref[...].astype(F32).T.astype(BF16)

    lp = lam_ref[...]
    lam = (jnp.exp(jnp.sum(lp[0:1] * lp[1:2], axis=-1, keepdims=True))
           - jnp.exp(jnp.sum(lp[2:3] * lp[3:4], axis=-1, keepdims=True)) + lambda_init)

    tq = DIFF_TQ
    tiles = q_ref.shape[0] // tq
    kb = DIFF_KEY_BLOCK
    groups = kb // SUBLANES
    nkb = seq // kb
    per_chunk = DIFF_KEY_CHUNK // kb
    width = bias_ref.shape[-1]
    row = lax.broadcasted_iota(jnp.int32, (LANES, 1), 0)

    def scores(t):
        qt = q_ref[t * tq:(t + 1) * tq, :].astype(F32).T
        qcat = jnp.concatenate([jnp.where(row < DIFF_HEAD_DIM, qt, 0.0),
                                jnp.where(row >= DIFF_HEAD_DIM, qt, 0.0)], axis=1).astype(BF16)
        q0 = (qi * tiles + t) * tq
        mx = jnp.full((SUBLANES, 2 * tq), NEG_INF, F32)
        for c in range(seq // DIFF_KEY_CHUNK):
            s = jnp.dot(k_ref[c * DIFF_KEY_CHUNK:(c + 1) * DIFF_KEY_CHUNK, :], qcat,
                        preferred_element_type=F32)
            for rr in range(per_chunk):
                r = c * per_chunk + rr
                start = pl.multiple_of(
                    jnp.clip(DIFF_NEAR_BLOCKS * kb - r * kb + q0, 0, width - tq), LANES)
                b = bias_ref[0, :, pl.ds(start, tq)]
                blk = s[rr * kb:(rr + 1) * kb, :]
                blk = jnp.concatenate([blk[:, :tq] + b, blk[:, tq:] + b], axis=1)
                mx = jnp.maximum(mx, jnp.max(blk.reshape(groups, SUBLANES, 2 * tq), axis=0))
                s_ref[t, r * kb:(r + 1) * kb, :] = blk
        return jnp.max(mx, axis=0, keepdims=True)

    def probs(t, m):
        sm = jnp.zeros((SUBLANES, 2 * tq), F32)
        acc = jnp.zeros((DIFF_V_DIM, 2 * tq), F32)
        kc = DIFF_VALUE_CHUNK
        for c in range(seq // kc):
            p = jnp.exp2(s_ref[t, c * kc:(c + 1) * kc, :] - m)
            sm = sm + jnp.sum(p.reshape(kc // SUBLANES, SUBLANES, 2 * tq), axis=0)
            acc = acc + jnp.dot(vt_ref[:, c * kc:(c + 1) * kc], p.astype(BF16),
                                preferred_element_type=F32)
        return jnp.sum(sm, axis=0, keepdims=True), acc

    def values(t, l_acc):
        l, acc = l_acc
        o = (acc[:, :tq] / l[:, :tq] - lam * (acc[:, tq:] / l[:, tq:])).T
        ms = jnp.mean(o * o, axis=-1, keepdims=True)
        o = o * lax.rsqrt(ms + RMS_EPS) * g_ref[...] * (1.0 - lambda_init)
        o_ref[t * tq:(t + 1) * tq, :] = o.astype(o_ref.dtype)

    m_next = scores(0)
    for t in range(tiles):
        m = m_next
        if t + 1 < tiles:
            m_next = scores(t + 1)
        values(t, probs(t, m))


def _toeplitz(t, rows, width):
    h, length = t.shape
    assert length == rows + width - 1
    flat = jnp.tile(jnp.pad(t, ((0, 0), (0, 1))), (1, rows))[:, :rows * length]
    return flat.reshape(h, rows, length)[:, :, rows - 1:]


def _diff_bias_strip(rel_diff, tq):
    kb = DIFF_KEY_BLOCK
    assert DIFF_NEAR_BLOCKS * kb - (tq - 1) > REL_MAX_DIST
    assert -DIFF_NEAR_BLOCKS_NEG * kb + kb - 1 < -REL_MAX_DIST
    width = tq + (DIFF_NEAR_BLOCKS + DIFF_NEAR_BLOCKS_NEG) * kb
    rel = jnp.arange(kb + width - 1) - (width - 1) + DIFF_NEAR_BLOCKS * kb
    t = rel_diff[_rel_bucket(rel)].T.astype(F32) * LOG2E
    return _toeplitz(t, kb, width)[:, ::-1, ::-1]


def _diff_attention(dq, dk, dv, lam_p, strip, subln_g, batch, seq, lambda_init):
    tq = DIFF_TQ
    tiles = min(DIFF_TILES_PER_STEP, seq // tq)
    ts = tiles * tq
    nq = seq // ts
    width = strip.shape[-1]
    est = (2 * ts * LANES * 2 * 2 + 2 * 2 * seq * LANES * 2 + 2 * DIFF_KEY_BLOCK * width * 4
           + seq * LANES * 2 + tiles * seq * 2 * tq * 4 + 2 * DIFF_KEY_CHUNK * 2 * tq * 4)
    return pl.pallas_call(
        functools.partial(_diff_attn_kernel, seq=seq, lambda_init=lambda_init),
        grid=(batch, DIFF_HEADS, nq),
        in_specs=[
            pl.BlockSpec((4, DIFF_HEAD_DIM), lambda b, h, i: (0, 0)),
            pl.BlockSpec((ts, LANES), lambda b, h, i: (b * nq + i, h)),
            pl.BlockSpec((seq, LANES), lambda b, h, i: (b, h)),
            pl.BlockSpec((seq, LANES), lambda b, h, i: (b, h)),
            pl.BlockSpec((1, DIFF_KEY_BLOCK, width), lambda b, h, i: (h, 0, 0)),
            pl.BlockSpec((1, DIFF_V_DIM), lambda b, h, i: (0, 0)),
        ],
        out_specs=pl.BlockSpec((ts, LANES), lambda b, h, i: (b * nq + i, h)),
        out_shape=jax.ShapeDtypeStruct((batch * seq, DIFF_WIDTH), BF16),
        scratch_shapes=[pltpu.VMEM((DIFF_V_DIM, seq), BF16),
                        pltpu.VMEM((tiles, seq, 2 * tq), F32)],
        compiler_params=pltpu.CompilerParams(
            dimension_semantics=("parallel", "parallel", "arbitrary"),
            vmem_limit_bytes=_vmem_limit(est)),
        name="diff_attn",
    )(lam_p, dq, dk, dv, strip, subln_g)


WIN_BAND = 3 * WIN_BLOCK
WIN_STRIP = 5 * WIN_BLOCK


def _win_attn_kernel(q_ref, k_ref, v_ref, strip_ref, sink_ref, o_ref, *, seq):
    row = lax.broadcasted_iota(jnp.int32, (LANES, 1), 0)
    lane = lax.broadcasted_iota(jnp.int32, (1, LANES), 1)
    blocks = q_ref.shape[0] // WIN_BLOCK
    for sub in range(blocks):
        _win_block(pl.program_id(1) * blocks + sub, sub, row, lane, q_ref, k_ref, v_ref, strip_ref,
                   sink_ref, o_ref, seq)


def _win_block(n, sub, row, lane, q_ref, k_ref, v_ref, strip_ref, sink_ref, o_ref, seq):
    rows = slice(sub * WIN_BLOCK, (sub + 1) * WIN_BLOCK)
    start = pl.multiple_of(jnp.clip(n * WIN_BLOCK - WIN_BLOCK, 0, seq - WIN_BAND), WIN_BLOCK)
    ustart = pl.multiple_of(start - n * WIN_BLOCK + 2 * WIN_BLOCK, WIN_BLOCK)
    for kv in range(WIN_KV_HEADS):
        kb = k_ref[pl.ds(start, WIN_BAND), kv * LANES:(kv + 1) * LANES]
        vb = v_ref[pl.ds(start, WIN_BAND), kv * LANES:(kv + 1) * LANES]
        cols = []
        for pair in range(WIN_GROUP // 2):
            c0 = kv * WIN_GROUP * WIN_HEAD_DIM + pair * LANES
            qt = q_ref[rows, c0:c0 + LANES].astype(F32).T
            cols.append(jnp.where(row < WIN_HEAD_DIM, qt, 0.0))
            cols.append(jnp.where(row >= WIN_HEAD_DIM, qt, 0.0))
        qcat = jnp.concatenate(cols, axis=1).astype(BF16)
        s = jnp.dot(kb, qcat, preferred_element_type=F32)
        s = s + strip_ref[kv, pl.ds(ustart, WIN_BAND), :]
        sink = sink_ref[kv]
        m = jnp.maximum(jnp.max(s, axis=0, keepdims=True), sink)
        p = jnp.exp2(s - m)
        den = jnp.sum(p, axis=0, keepdims=True) + jnp.exp2(sink - m)
        vt = vb.astype(F32).T.astype(BF16)
        o = (jnp.dot(vt, p.astype(BF16), preferred_element_type=F32) / den).T
        for pair in range(WIN_GROUP // 2):
            c0 = kv * WIN_GROUP * WIN_HEAD_DIM + pair * LANES
            even = o[(2 * pair) * WIN_BLOCK:(2 * pair + 1) * WIN_BLOCK, :]
            odd = o[(2 * pair + 1) * WIN_BLOCK:(2 * pair + 2) * WIN_BLOCK, :]
            o_ref[rows, c0:c0 + LANES] = jnp.where(lane < WIN_HEAD_DIM, even, odd).astype(o_ref.dtype)


def _win_bias_strip(rel_win):
    rel = jnp.arange(WIN_BLOCK + WIN_STRIP - 1) - (WIN_BLOCK - 1) - 2 * WIN_BLOCK
    t = jnp.where((jnp.abs(rel) <= WINDOW)[None],
                  rel_win[_rel_bucket(rel)].T.astype(F32) * LOG2E, NEG_INF)
    strip = _toeplitz(t, WIN_BLOCK, WIN_STRIP)
    strip = strip.reshape(WIN_KV_HEADS, WIN_GROUP, WIN_BLOCK, WIN_STRIP).transpose(0, 3, 1, 2)
    return strip.reshape(WIN_KV_HEADS, WIN_STRIP, WIN_GROUP * WIN_BLOCK)


def _win_attention(wq, wk, wv, sink, strip, batch, seq):
    heads_lanes = WIN_GROUP * WIN_BLOCK
    sink_lanes = jnp.broadcast_to((sink.astype(F32) * LOG2E).reshape(WIN_KV_HEADS, WIN_GROUP, 1),
                                  (WIN_KV_HEADS, WIN_GROUP, WIN_BLOCK)).reshape(
                                      WIN_KV_HEADS, 1, heads_lanes)
    rows = WIN_BLOCKS_PER_STEP * WIN_BLOCK
    assert seq % rows == 0
    nb = seq // rows
    est = (2 * rows * WIN_WIDTH * 2 * 2 + 2 * 2 * seq * 2 * LANES * 2
           + 2 * WIN_KV_HEADS * WIN_STRIP * heads_lanes * 4 + 8 * WIN_BAND * heads_lanes * 4)
    return pl.pallas_call(
        functools.partial(_win_attn_kernel, seq=seq),
        grid=(batch, nb),
        in_specs=[
            pl.BlockSpec((rows, WIN_WIDTH), lambda b, n: (b * nb + n, 0)),
            pl.BlockSpec((seq, 2 * LANES), lambda b, n: (b, 0)),
            pl.BlockSpec((seq, 2 * LANES), lambda b, n: (b, 0)),
            pl.BlockSpec((WIN_KV_HEADS, WIN_STRIP, heads_lanes), lambda b, n: (0, 0, 0)),
            pl.BlockSpec((WIN_KV_HEADS, 1, heads_lanes), lambda b, n: (0, 0, 0)),
        ],
        out_specs=pl.BlockSpec((rows, WIN_WIDTH), lambda b, n: (b * nb + n, 0)),
        out_shape=jax.ShapeDtypeStruct((batch * seq, WIN_WIDTH), BF16),
        compiler_params=pltpu.CompilerParams(
            dimension_semantics=("parallel", "parallel"), vmem_limit_bytes=_vmem_limit(est)),
        name="win_attn",
    )(wq, wk, wv, strip, sink_lanes)


ROUTE_E1, ROUTE_E2, ROUTE_W1, ROUTE_W2 = 0, 1, 2, 3


def _merge_kernel(x_ref, od_ref, ow_ref, gate_ref, pd_ref, pw_ref, wo_ref, g_ref, wr_ref, br_ref,
                  x1_ref, hn_ref, route_ref, counts_ref):
    @pl.when(pl.program_id(0) == 0)
    def _():
        counts_ref[...] = jnp.zeros_like(counts_ref)

    sub_rows = x_ref.shape[0] // MERGE_SUBTILES
    hits = [_merge_rows(slice(i * sub_rows, (i + 1) * sub_rows), x_ref, od_ref, ow_ref, gate_ref,
                        pd_ref, pw_ref, wo_ref, g_ref, wr_ref, br_ref, x1_ref, hn_ref, route_ref)
            for i in range(MERGE_SUBTILES)]
    counts_ref[...] += sum(hits[1:], hits[0])


def _merge_rows(rows, x_ref, od_ref, ow_ref, gate_ref, pd_ref, pw_ref, wo_ref, g_ref, wr_ref, br_ref,
                x1_ref, hn_ref, route_ref):
    md = jnp.dot(od_ref[rows, :], pd_ref[...], preferred_element_type=F32)
    mw = jnp.dot(ow_ref[rows, :], pw_ref[...], preferred_element_type=F32)
    merged = (gate_ref[rows, :D_MODEL].astype(F32) * md + gate_ref[rows, D_MODEL:].astype(F32) * mw)
    x1 = x_ref[rows, :] + jnp.dot(merged.astype(BF16), wo_ref[...], preferred_element_type=F32)
    x1_ref[rows, :] = x1
    ms = jnp.mean(x1 * x1, axis=-1, keepdims=True)
    hn = x1 * lax.rsqrt(ms + RMS_EPS) * g_ref[...]
    hn_ref[rows, :] = hn

    hn_hi = hn.astype(BF16)
    hn_lo = (hn - hn_hi.astype(F32)).astype(BF16)
    parts = (jnp.dot(hn_hi, wr_ref[...], preferred_element_type=F32)
             + jnp.dot(hn_lo, wr_ref[...], preferred_element_type=F32))
    logits = parts + pltpu.roll(parts, LANES // 2, axis=1) + br_ref[...]
    lane = lax.broadcasted_iota(jnp.int32, logits.shape, 1)
    big = jnp.int32(LANES)
    is_group = lane < N_GROUPS
    gl = jnp.where(is_group, logits, NEG_INF)
    gmax = jnp.max(gl, axis=-1, keepdims=True)
    gsel = jnp.min(jnp.where(gl == gmax, lane, big), axis=-1, keepdims=True)
    gden = jnp.sum(jnp.where(is_group, jnp.exp(gl - gmax), 0.0), axis=-1, keepdims=True)
    gw = 1.0 / gden
    lo = N_GROUPS + EXPERTS_PER_GROUP * gsel
    in_group = jnp.logical_and(lane >= lo, lane < lo + EXPERTS_PER_GROUP)
    el = jnp.where(in_group, logits, NEG_INF)
    t1 = jnp.max(el, axis=-1, keepdims=True)
    i1 = jnp.min(jnp.where(el == t1, lane, big), axis=-1, keepdims=True)
    el2 = jnp.where(lane == i1, NEG_INF, el)
    t2 = jnp.max(el2, axis=-1, keepdims=True)
    i2 = jnp.min(jnp.where(el2 == t2, lane, big), axis=-1, keepdims=True)
    z = jnp.exp(t2 - t1)
    w1 = gw / (1.0 + z)
    w2 = gw * z / (1.0 + z)
    e1 = (i1 - N_GROUPS).astype(F32)
    e2 = (i2 - N_GROUPS).astype(F32)
    route = jnp.where(lane == ROUTE_E1, e1,
                      jnp.where(lane == ROUTE_E2, e2,
                                jnp.where(lane == ROUTE_W1, w1,
                                          jnp.where(lane == ROUTE_W2, w2, 0.0))))
    route_ref[rows, :] = route
    hits = (jnp.where(lane == i1 - N_GROUPS, 1.0, 0.0) + jnp.where(lane == i2 - N_GROUPS, 1.0, 0.0))
    return jnp.sum(hits, axis=0, keepdims=True)


def _merge(x2, od, ow, gates, pd, pw, wo, g, wr, br):
    n = x2.shape[0]
    tm = ROW_TILE
    est = (2 * tm * D_MODEL * 4 * 3 + 2 * tm * (DIFF_WIDTH + WIN_WIDTH + C_GATE) * 2
           + 2 * (DIFF_WIDTH + WIN_WIDTH + D_MODEL) * D_MODEL * 2 + D_MODEL * LANES * 4 * 2
           + 6 * tm * D_MODEL * 4)
    row = lambda c: pl.BlockSpec((tm, c), lambda i: (i, 0))
    full = lambda r, c: pl.BlockSpec((r, c), lambda i: (0, 0))
    return pl.pallas_call(
        _merge_kernel,
        grid=(n // tm,),
        in_specs=[row(D_MODEL), row(DIFF_WIDTH), row(WIN_WIDTH), row(C_GATE),
                  full(DIFF_WIDTH, D_MODEL), full(WIN_WIDTH, D_MODEL), full(D_MODEL, D_MODEL),
                  full(1, D_MODEL), full(D_MODEL, LANES), full(1, LANES)],
        out_specs=[row(D_MODEL), row(D_MODEL), row(LANES), full(1, LANES)],
        out_shape=[jax.ShapeDtypeStruct((n, D_MODEL), F32), jax.ShapeDtypeStruct((n, D_MODEL), F32),
                   jax.ShapeDtypeStruct((n, LANES), F32), jax.ShapeDtypeStruct((1, LANES), F32)],
        compiler_params=pltpu.CompilerParams(
            dimension_semantics=("arbitrary",), vmem_limit_bytes=_vmem_limit(est)),
        name="merge_route",
    )(x2, od, ow, gates, pd, pw, wo, g, wr, br)


def _positions_kernel(route_ref, pstart_ref, dest_ref, carry_ref):
    i = pl.program_id(0)

    @pl.when(i == 0)
    def _():
        carry_ref[...] = jnp.zeros_like(carry_ref)

    r = route_ref[...]
    tb = r.shape[0]
    lane = lax.broadcasted_iota(jnp.int32, r.shape, 1)
    lane_f = lane.astype(F32)
    oh1 = lane_f == r[:, ROUTE_E1:ROUTE_E1 + 1]
    oh2 = lane_f == r[:, ROUTE_E2:ROUTE_E2 + 1]
    cnt = jnp.where(oh1, 1.0, 0.0) + jnp.where(oh2, 1.0, 0.0)
    rows = lax.broadcasted_iota(jnp.int32, (tb, tb), 0)
    cols = lax.broadcasted_iota(jnp.int32, (tb, tb), 1)
    tri = jnp.where(rows >= cols, 1.0, 0.0).astype(BF16)
    incl = jnp.dot(tri, cnt.astype(BF16), preferred_element_type=F32)
    row_of = incl - cnt + carry_ref[...] + pstart_ref[...]
    dest1 = jnp.sum(jnp.where(oh1, row_of, 0.0), axis=-1, keepdims=True)
    dest2 = jnp.sum(jnp.where(oh2, row_of, 0.0), axis=-1, keepdims=True)
    dest_ref[...] = jnp.where(lane == ROUTE_E1, dest1, jnp.where(lane == ROUTE_E2, dest2, 0.0))
    carry_ref[...] = carry_ref[...] + incl[tb - 1:tb, :]


def _positions(route, pad_start):
    n = route.shape[0]
    tb = POS_TILE
    return pl.pallas_call(
        _positions_kernel,
        grid=(n // tb,),
        in_specs=[pl.BlockSpec((tb, LANES), lambda i: (i, 0)),
                  pl.BlockSpec((1, LANES), lambda i: (0, 0))],
        out_specs=pl.BlockSpec((tb, LANES), lambda i: (i, 0)),
        out_shape=jax.ShapeDtypeStruct((n, LANES), F32),
        scratch_shapes=[pltpu.VMEM((1, LANES), F32)],
        compiler_params=pltpu.CompilerParams(dimension_semantics=("arbitrary",)),
        name="positions",
    )(route, pad_start)


def _dispatch_kernel(d1_ref, d2_ref, hn_ref, xs_init_hbm, xs_hbm, sem):
    del xs_init_hbm
    ts = hn_ref.shape[0]
    for t in range(ts):
        pltpu.make_async_copy(hn_ref.at[pl.ds(t, 1)], xs_hbm.at[pl.ds(d1_ref[t], 1)],
                              sem).start(priority=0)
        pltpu.make_async_copy(hn_ref.at[pl.ds(t, 1)], xs_hbm.at[pl.ds(d2_ref[t], 1)],
                              sem).start(priority=1)
    for _ in range(2):
        pltpu.make_async_copy(hn_ref, xs_hbm.at[pl.ds(0, ts)], sem).wait()


def _dispatch(dest1, dest2, hn, xs_init):
    n = hn.shape[0]
    ts = MOVE_TILE
    smem_blk = pl.BlockSpec((ts,), lambda i: (i,), memory_space=pltpu.SMEM)
    return pl.pallas_call(
        _dispatch_kernel,
        grid=(n // ts,),
        in_specs=[smem_blk, smem_blk,
                  pl.BlockSpec((ts, D_MODEL), lambda i: (i, 0)), pl.BlockSpec(memory_space=pl.ANY)],
        out_specs=pl.BlockSpec(memory_space=pl.ANY),
        out_shape=jax.ShapeDtypeStruct(xs_init.shape, F32),
        scratch_shapes=[pltpu.SemaphoreType.DMA(())],
        input_output_aliases={3: 0},
        compiler_params=pltpu.CompilerParams(dimension_semantics=("arbitrary",)),
        name="dispatch",
    )(dest1, dest2, hn, xs_init)


def _experts_kernel(be_ref, na_ref, xs_ref, wg_ref, wu_ref, wd_ref, ys_ref, wg_bf, wu_bf, wd_bf):
    j = pl.program_id(0)
    active = j < na_ref[0]
    new_expert = jnp.logical_or(j == 0, be_ref[j] != be_ref[jnp.maximum(j - 1, 0)])

    @pl.when(jnp.logical_and(active, new_expert))
    def _():
        wg_bf[...] = wg_ref[0, 0].astype(BF16)
        wu_bf[...] = wu_ref[0, 0].astype(BF16)
        wd_bf[...] = wd_ref[0, 0].astype(BF16)

    @pl.when(active)
    def _():
        x = xs_ref[...].astype(BF16)
        g = jnp.dot(x, wg_bf[...], preferred_element_type=F32)
        u = jnp.dot(x, wu_bf[...], preferred_element_type=F32)
        hid = (g * jax.nn.sigmoid(g) * u).astype(BF16)
        ys_ref[...] = jnp.dot(hid, wd_bf[...], preferred_element_type=F32)

    @pl.when(j >= na_ref[0])
    def _():
        ys_ref[...] = jnp.zeros_like(ys_ref)


def _experts(block_e, n_active, xs, wg, wu, wd, layer):
    n_rows = xs.shape[0]
    tmb = MOE_TILE
    nblk = n_rows // tmb
    rows = pl.BlockSpec((tmb, D_MODEL), lambda j, be, na: (j, 0))
    est = (4 * tmb * D_MODEL * 4 + 3 * D_MODEL * EXPERT_HIDDEN * (2 * 4 + 2)
           + 4 * tmb * EXPERT_HIDDEN * 4 + tmb * D_MODEL * 4)
    return pl.pallas_call(
        _experts_kernel,
        grid_spec=pltpu.PrefetchScalarGridSpec(
            num_scalar_prefetch=2,
            grid=(nblk,),
            in_specs=[
                rows,
                pl.BlockSpec((1, 1, D_MODEL, EXPERT_HIDDEN), lambda j, be, na: (layer, be[j], 0, 0)),
                pl.BlockSpec((1, 1, D_MODEL, EXPERT_HIDDEN), lambda j, be, na: (layer, be[j], 0, 0)),
                pl.BlockSpec((1, 1, EXPERT_HIDDEN, D_MODEL), lambda j, be, na: (layer, be[j], 0, 0)),
            ],
            out_specs=rows,
            scratch_shapes=[pltpu.VMEM((D_MODEL, EXPERT_HIDDEN), BF16),
                            pltpu.VMEM((D_MODEL, EXPERT_HIDDEN), BF16),
                            pltpu.VMEM((EXPERT_HIDDEN, D_MODEL), BF16)],
        ),
        out_shape=jax.ShapeDtypeStruct((n_rows, D_MODEL), F32),
        compiler_params=pltpu.CompilerParams(
            dimension_semantics=("arbitrary",), vmem_limit_bytes=_vmem_limit(est)),
        name="experts",
    )(block_e, n_active, xs, wg, wu, wd)


def _combine_kernel(d1_ref, d2_ref, route_ref, x_ref, g_ref, ys_hbm, o_ref, buf, sem):
    tc = d1_ref.shape[0]
    for t in range(tc):
        pltpu.make_async_copy(ys_hbm.at[pl.ds(d1_ref[t], 1)], buf.at[pl.ds(t, 1)],
                              sem).start(priority=0)
        pltpu.make_async_copy(ys_hbm.at[pl.ds(d2_ref[t], 1)], buf.at[pl.ds(tc + t, 1)],
                              sem).start(priority=1)
    pltpu.make_async_copy(ys_hbm.at[pl.ds(0, 2 * tc)], buf, sem).wait()

    r = route_ref[...]
    w1 = r[:, ROUTE_W1:ROUTE_W1 + 1]
    w2 = r[:, ROUTE_W2:ROUTE_W2 + 1]
    out = x_ref[...] + w1 * buf[0:tc, :] + w2 * buf[tc:2 * tc, :]
    ms = jnp.mean(out * out, axis=-1, keepdims=True)
    o_ref[...] = out * lax.rsqrt(ms + RMS_EPS) * g_ref[...]


def _combine(dest1, dest2, route, x1, g, ys):
    n = x1.shape[0]
    tc = MOVE_TILE
    smem_blk = pl.BlockSpec((tc,), lambda i: (i,), memory_space=pltpu.SMEM)
    est = 2 * tc * D_MODEL * 4 * 2 + 2 * tc * D_MODEL * 4 + 2 * tc * LANES * 4 + 4 * tc * D_MODEL * 4
    return pl.pallas_call(
        _combine_kernel,
        grid=(n // tc,),
        in_specs=[smem_blk, smem_blk,
                  pl.BlockSpec((tc, LANES), lambda i: (i, 0)),
                  pl.BlockSpec((tc, D_MODEL), lambda i: (i, 0)),
                  pl.BlockSpec((1, D_MODEL), lambda i: (0, 0)),
                  pl.BlockSpec(memory_space=pl.ANY)],
        out_specs=pl.BlockSpec((tc, D_MODEL), lambda i: (i, 0)),
        out_shape=jax.ShapeDtypeStruct((n, D_MODEL), F32),
        scratch_shapes=[pltpu.VMEM((2 * tc, D_MODEL), F32), pltpu.SemaphoreType.DMA(())],
        compiler_params=pltpu.CompilerParams(
            dimension_semantics=("arbitrary",), vmem_limit_bytes=_vmem_limit(est)),
        name="combine",
    )(dest1, dest2, route, x1, g, ys)


def _plan_blocks(counts, n_rows):
    counts = counts[0, :N_EXPERTS].astype(jnp.int32)
    padded = (counts + MOE_TILE - 1) // MOE_TILE * MOE_TILE
    pad_end = jnp.cumsum(padded)
    pad_start = pad_end - padded
    blk_start = jnp.arange(n_rows // MOE_TILE, dtype=jnp.int32) * MOE_TILE
    block_e = jnp.minimum(jnp.sum((pad_end[None, :] <= blk_start[:, None]).astype(jnp.int32), axis=1),
                          N_EXPERTS - 1)
    n_active = (pad_end[-1:] // MOE_TILE).astype(jnp.int32)
    pad_start_lanes = jnp.zeros((1, LANES), F32).at[0, :N_EXPERTS].set(pad_start.astype(F32))
    return pad_start_lanes, block_e, n_active


def _forward(x, w_in, diff_lambda, diff_subln, win_sink, w_branch_diff, w_branch_win, w_out,
             rel_bias, norm_mix, norm_ffn, w_router_group, b_router_group, w_router_expert,
             b_router_expert, w_exp_gate, w_exp_up, w_exp_down, norm_final):
    batch, seq, _ = x.shape
    n = batch * seq
    depth = w_in.shape[0]
    assert seq % DIFF_KEY_CHUNK == 0 and seq % DIFF_TQ == 0 and seq >= WIN_BAND
    assert n % ROW_TILE == 0 and n % POS_TILE == 0 and n % MOVE_TILE == 0

    diff_strip = _diff_bias_strip(rel_bias[:, :DIFF_HEADS], DIFF_TQ)
    win_strip = _win_bias_strip(rel_bias[:, DIFF_HEADS:])
    n_rows = -(-(2 * n + N_EXPERTS * (MOE_TILE - 1)) // MOE_TILE) * MOE_TILE
    xs = jnp.zeros((n_rows, D_MODEL), F32)

    x2 = x.reshape(n, D_MODEL)
    moe = None
    for l in range(depth):
        splits = (512, 1024, 1536, 2048, 2176, 2304)
        wdq, wdk, wdv, wwq, wwk, wwv, wgt = jnp.split(w_in[l], splits, axis=-1)
        dup = lambda w: jnp.concatenate(
            [w[:, :WIN_HEAD_DIM], w[:, :WIN_HEAD_DIM], w[:, WIN_HEAD_DIM:], w[:, WIN_HEAD_DIM:]],
            axis=-1)
        w_all = jnp.concatenate([wdq, wdk, wdv, wwq, dup(wwk), dup(wwv), wgt], axis=-1).astype(BF16)

        if moe is None:
            dq, dk, dv, wq, wk, wv, gates = _inproj(x2, norm_mix[l][None], w_all)
        else:
            x2, dq, dk, dv, wq, wk, wv, gates = _combine_inproj(*moe, norm_mix[l][None], w_all)

        lambda_init = 0.8 - 0.6 * math.exp(-0.3 * l)
        o_diff = _diff_attention(dq, dk, dv, diff_lambda[l], diff_strip, diff_subln[l][None],
                                 batch, seq, lambda_init)
        o_win = _win_attention(wq, wk, wv, win_sink[l], win_strip, batch, seq)

        wr = jnp.zeros((D_MODEL, LANES // 2), F32)
        wr = wr.at[:, :N_GROUPS].set(w_router_group[l])
        wr = wr.at[:, N_GROUPS:N_GROUPS + N_EXPERTS].set(w_router_expert[l])
        wr_hi = wr.astype(BF16)
        wr_lo = (wr - wr_hi.astype(F32)).astype(BF16)
        wr = jnp.concatenate([wr_hi, wr_lo], axis=1)
        br = jnp.zeros((1, LANES), F32)
        br = br.at[0, :N_GROUPS].set(b_router_group[l])
        br = br.at[0, N_GROUPS:N_GROUPS + N_EXPERTS].set(b_router_expert[l])
        x1, hn, route, counts = _merge(x2, o_diff, o_win, gates,
                                       w_branch_diff[l].astype(BF16), w_branch_win[l].astype(BF16),
                                       w_out[l].astype(BF16), norm_ffn[l][None], wr, br)

        pad_start, block_e, n_active = _plan_blocks(counts, n_rows)
        dest = _positions(route, pad_start)
        dest1 = dest[:, ROUTE_E1].astype(jnp.int32)
        dest2 = dest[:, ROUTE_E2].astype(jnp.int32)
        xs = _dispatch(dest1, dest2, hn, xs)
        ys = _experts(block_e, n_active, xs, w_exp_gate, w_exp_up, w_exp_down, l)
        moe = (dest1, dest2, route, x1, ys)
    dest1, dest2, route, x1, ys = moe
    out = _combine(dest1, dest2, route, x1, norm_final[None], ys)
    return out.reshape(batch, seq, D_MODEL)


def kernel(x, w_in, diff_lambda, diff_subln, win_sink, w_branch_diff, w_branch_win, w_out, rel_bias, norm_mix, norm_ffn, w_router_group, b_router_group, w_router_expert, b_router_expert, w_exp_gate, w_exp_up, w_exp_down, norm_final):
    return _forward(x, w_in, diff_lambda, diff_subln, win_sink, w_branch_diff, w_branch_win, w_out,
                    rel_bias, norm_mix, norm_ffn, w_router_group, b_router_group, w_router_expert,
                    b_router_expert, w_exp_gate, w_exp_up, w_exp_down, norm_final)
```

```python
import functools
import math

import jax
import jax.numpy as jnp
from jax import lax
from jax.experimental import pallas as pl
from jax.experimental.pallas import tpu as pltpu

D_MODEL = 1024
DIFF_HEADS = 4
DIFF_HEAD_DIM = 64
DIFF_V_DIM = 2 * DIFF_HEAD_DIM
DIFF_WIDTH = DIFF_HEADS * DIFF_V_DIM
WIN_HEADS = 8
WIN_KV_HEADS = 2
WIN_GROUP = WIN_HEADS // WIN_KV_HEADS
WIN_HEAD_DIM = 64
WIN_WIDTH = WIN_HEADS * WIN_HEAD_DIM
WINDOW = 128
WIN_BLOCK = 128
REL_BUCKETS = 32
REL_MAX_DIST = 128
N_GROUPS = 4
EXPERTS_PER_GROUP = 8
N_EXPERTS = N_GROUPS * EXPERTS_PER_GROUP
EXPERT_HIDDEN = 512
RMS_EPS = 1e-6
NEG_INF = -1e30

LANES = 128
SUBLANES = 8
V7X_VMEM_BYTES = 64 * 1024 * 1024

ROW_TILE = 512
MERGE_SUBTILES = 2
DIFF_TQ = 256
DIFF_TILES_PER_STEP = 8
DIFF_KEY_BLOCK = 128
DIFF_KEY_CHUNK = 512
DIFF_VALUE_CHUNK = 256
DIFF_NEAR_BLOCKS = 3
DIFF_NEAR_BLOCKS_NEG = 2
LOG2E = math.log2(math.e)
WIN_BLOCKS_PER_STEP = 4
POS_TILE = 256
MOE_TILE = 512
MOVE_TILE = 256

F32 = jnp.float32
BF16 = jnp.bfloat16


def _vmem_limit(nbytes):
    return int(min(max(2 * nbytes, 16 * 1024 * 1024), V7X_VMEM_BYTES - 8 * 1024 * 1024))


def _rel_bucket(rel):
    half = REL_BUCKETS // 2
    max_exact = half // 2
    n = jnp.abs(rel)
    nf = jnp.maximum(n, max_exact).astype(jnp.float32)
    large = max_exact + (jnp.log(nf / max_exact) / math.log(REL_MAX_DIST / max_exact)
                         * (half - max_exact)).astype(jnp.int32)
    large = jnp.minimum(large, half - 1)
    return jnp.where(rel > 0, half, 0) + jnp.where(n < max_exact, n, large)


C_DQ, C_DK, C_DV, C_WQ = 512, 512, 512, 512
C_WKD, C_WVD = 2 * LANES, 2 * LANES
C_GATE = 2 * D_MODEL
IN_COLS = C_DQ + C_DK + C_DV + C_WQ + C_WKD + C_WVD + C_GATE


def _project(x, g_ref, w_ref, dq_ref, dk_ref, dv_ref, wq_ref, wk_ref, wv_ref, gate_ref):
    ms = jnp.mean(x * x, axis=-1, keepdims=True)
    h = (x * lax.rsqrt(ms + RMS_EPS) * g_ref[...]).astype(BF16)

    col = 0

    def proj(width):
        nonlocal col
        out = jnp.dot(h, w_ref[:, col:col + width], preferred_element_type=F32)
        col += width
        return out

    dq_ref[...] = (proj(C_DQ) * (DIFF_HEAD_DIM ** -0.5 * LOG2E)).astype(BF16)
    dk_ref[...] = proj(C_DK).astype(BF16)
    dv_ref[...] = proj(C_DV).astype(BF16)
    wq_ref[...] = (proj(C_WQ) * (WIN_HEAD_DIM ** -0.5 * LOG2E)).astype(BF16)
    wk_ref[...] = proj(C_WKD).astype(BF16)
    wv_ref[...] = proj(C_WVD).astype(BF16)
    gate_ref[...] = jax.nn.sigmoid(proj(C_GATE)).astype(BF16)


def _inproj_kernel(x_ref, g_ref, w_ref, *out_refs):
    _project(x_ref[...], g_ref, w_ref, *out_refs)


PROJ_WIDTHS = (C_DQ, C_DK, C_DV, C_WQ, C_WKD, C_WVD, C_GATE)


def _inproj(x2, g, w):
    n = x2.shape[0]
    tm = ROW_TILE
    est = 2 * tm * D_MODEL * 4 + D_MODEL * IN_COLS * 2 + 2 * tm * IN_COLS * 2 + tm * IN_COLS * 4
    return pl.pallas_call(
        _inproj_kernel,
        grid=(n // tm,),
        in_specs=[
            pl.BlockSpec((tm, D_MODEL), lambda i: (i, 0)),
            pl.BlockSpec((1, D_MODEL), lambda i: (0, 0)),
            pl.BlockSpec((D_MODEL, IN_COLS), lambda i: (0, 0), pipeline_mode=pl.Buffered(1)),
        ],
        out_specs=[pl.BlockSpec((tm, c), lambda i: (i, 0)) for c in PROJ_WIDTHS],
        out_shape=[jax.ShapeDtypeStruct((n, c), BF16) for c in PROJ_WIDTHS],
        compiler_params=pltpu.CompilerParams(
            dimension_semantics=("parallel",), vmem_limit_bytes=_vmem_limit(est)),
        name="inproj",
    )(x2, g, w)


def _combined_rows(d1_next_ref, d2_next_ref, d1_first_ref, d2_first_ref, route_ref, x1_ref, ys_hbm,
                   buf, sem):
    i = pl.program_id(0)
    tm = x1_ref.shape[0]
    slot = i % 2

    def gather(i1_ref, i2_ref, dst):
        for t in range(tm):
            pltpu.make_async_copy(ys_hbm.at[pl.ds(i1_ref[t], 1)], buf.at[dst, pl.ds(t, 1)],
                                  sem.at[dst]).start(priority=0)
            pltpu.make_async_copy(ys_hbm.at[pl.ds(i2_ref[t], 1)], buf.at[dst, pl.ds(tm + t, 1)],
                                  sem.at[dst]).start(priority=1)

    @pl.when(i == 0)
    def _():
        gather(d1_first_ref, d2_first_ref, 0)

    @pl.when(i + 1 < pl.num_programs(0))
    def _():
        gather(d1_next_ref, d2_next_ref, 1 - slot)

    pltpu.make_async_copy(ys_hbm.at[pl.ds(0, 2 * tm)], buf.at[slot], sem.at[slot]).wait()
    r = route_ref[...]
    w1 = r[:, ROUTE_W1:ROUTE_W1 + 1]
    w2 = r[:, ROUTE_W2:ROUTE_W2 + 1]
    return x1_ref[...] + w1 * buf[slot, 0:tm, :] + w2 * buf[slot, tm:2 * tm, :]


def _combine_inproj_kernel(d1_next_ref, d2_next_ref, d1_first_ref, d2_first_ref, route_ref, x1_ref,
                           g_ref, w_ref, ys_hbm, x2_ref, *rest):
    out_refs, (buf, sem) = rest[:len(PROJ_WIDTHS)], rest[len(PROJ_WIDTHS):]
    x = _combined_rows(d1_next_ref, d2_next_ref, d1_first_ref, d2_first_ref, route_ref, x1_ref,
                       ys_hbm, buf, sem)
    x2_ref[...] = x
    _project(x, g_ref, w_ref, *out_refs)


def _combine_final_kernel(d1_next_ref, d2_next_ref, d1_first_ref, d2_first_ref, route_ref, x1_ref,
                          g_ref, ys_hbm, o_ref, buf, sem):
    x = _combined_rows(d1_next_ref, d2_next_ref, d1_first_ref, d2_first_ref, route_ref, x1_ref,
                       ys_hbm, buf, sem)
    ms = jnp.mean(x * x, axis=-1, keepdims=True)
    o_ref[...] = x * lax.rsqrt(ms + RMS_EPS) * g_ref[...]


def _combine_specs(n):
    tm = ROW_TILE
    nt = n // tm
    nxt = pl.BlockSpec((tm,), lambda i: (jnp.minimum(i + 1, nt - 1),), memory_space=pltpu.SMEM)
    first = pl.BlockSpec((tm,), lambda i: (0,), memory_space=pltpu.SMEM)
    specs = [nxt, nxt, first, first,
             pl.BlockSpec((tm, LANES), lambda i: (i, 0)),
             pl.BlockSpec((tm, D_MODEL), lambda i: (i, 0))]
    scratch = [pltpu.VMEM((2, 2 * tm, D_MODEL), F32), pltpu.SemaphoreType.DMA((2,))]
    return tm, nt, specs, scratch


def _combine_final(dest1, dest2, route, x1, ys, g):
    n = x1.shape[0]
    tm, nt, specs, scratch = _combine_specs(n)
    est = 4 * tm * D_MODEL * 4 + 2 * 2 * tm * D_MODEL * 4 + 4 * tm * D_MODEL * 4
    return pl.pallas_call(
        _combine_final_kernel,
        grid=(nt,),
        in_specs=specs + [pl.BlockSpec((1, D_MODEL), lambda i: (0, 0)),
                          pl.BlockSpec(memory_space=pl.ANY)],
        out_specs=pl.BlockSpec((tm, D_MODEL), lambda i: (i, 0)),
        out_shape=jax.ShapeDtypeStruct((n, D_MODEL), F32),
        scratch_shapes=scratch,
        compiler_params=pltpu.CompilerParams(
            dimension_semantics=("arbitrary",), vmem_limit_bytes=_vmem_limit(est)),
        name="combine_final",
    )(dest1, dest2, dest1, dest2, route, x1, g, ys)


def _combine_inproj(dest1, dest2, route, x1, ys, g, w):
    n = x1.shape[0]
    tm, nt, specs, scratch = _combine_specs(n)
    est = (4 * tm * D_MODEL * 4 + D_MODEL * IN_COLS * 2 + 2 * tm * IN_COLS * 2 + tm * IN_COLS * 4
           + 2 * 2 * tm * D_MODEL * 4)
    return pl.pallas_call(
        _combine_inproj_kernel,
        grid=(nt,),
        in_specs=specs + [
            pl.BlockSpec((1, D_MODEL), lambda i: (0, 0)),
            pl.BlockSpec((D_MODEL, IN_COLS), lambda i: (0, 0), pipeline_mode=pl.Buffered(1)),
            pl.BlockSpec(memory_space=pl.ANY),
        ],
        out_specs=[pl.BlockSpec((tm, D_MODEL), lambda i: (i, 0))]
        + [pl.BlockSpec((tm, c), lambda i: (i, 0)) for c in PROJ_WIDTHS],
        out_shape=[jax.ShapeDtypeStruct((n, D_MODEL), F32)]
        + [jax.ShapeDtypeStruct((n, c), BF16) for c in PROJ_WIDTHS],
        scratch_shapes=scratch,
        compiler_params=pltpu.CompilerParams(
            dimension_semantics=("arbitrary",), vmem_limit_bytes=_vmem_limit(est)),
        name="combine_inproj",
    )(dest1, dest2, dest1, dest2, route, x1, g, w, ys)


def _diff_attn_kernel(lam_ref, q_ref, k_ref, v_ref, bias_ref, g_ref, o_ref, vt_ref, s_ref, *,
                      seq, lambda_init):
    qi = pl.program_id(2)

    @pl.when(qi == 0)
    def _():
        vt_ref[...] = v_ref[...].astype(F32).T.astype(BF16)

    lp = lam_ref[...]
    lam = (jnp.exp(jnp.sum(lp[0:1] * lp[1:2], axis=-1, keepdims=True))
           - jnp.exp(jnp.sum(lp[2:3] * lp[3:4], axis=-1, keepdims=True)) + lambda_init)

    tq = DIFF_TQ
    tiles = q_ref.shape[0] // tq
    kb = DIFF_KEY_BLOCK
    groups = kb // SUBLANES
    nkb = seq // kb
    per_chunk = DIFF_KEY_CHUNK // kb
    width = bias_ref.shape[-1]
    row = lax.broadcasted_iota(jnp.int32, (LANES, 1), 0)

    def scores(t):
        qt = q_ref[t * tq:(t + 1) * tq, :].astype(F32).T
        qcat = jnp.concatenate([jnp.where(row < DIFF_HEAD_DIM, qt, 0.0),
                                jnp.where(row >= DIFF_HEAD_DIM, qt, 0.0)], axis=1).astype(BF16)
        q0 = (qi * tiles + t) * tq
        mx = jnp.full((SUBLANES, 2 * tq), NEG_INF, F32)
        for c in range(seq // DIFF_KEY_CHUNK):
            s = jnp.dot(k_ref[c * DIFF_KEY_CHUNK:(c + 1) * DIFF_KEY_CHUNK, :], qcat,
                        preferred_element_type=F32)
            for rr in range(per_chunk):
                r = c * per_chunk + rr
                start = pl.multiple_of(
                    jnp.clip(DIFF_NEAR_BLOCKS * kb - r * kb + q0, 0, width - tq), LANES)
                b = bias_ref[0, :, pl.ds(start, tq)]
                blk = s[rr * kb:(rr + 1) * kb, :]
                blk = jnp.concatenate([blk[:, :tq] + b, blk[:, tq:] + b], axis=1)
                mx = jnp.maximum(mx, jnp.max(blk.reshape(groups, SUBLANES, 2 * tq), axis=0))
                s_ref[t, r * kb:(r + 1) * kb, :] = blk
        return jnp.max(mx, axis=0, keepdims=True)

    def probs(t, m):
        sm = jnp.zeros((SUBLANES, 2 * tq), F32)
        acc = jnp.zeros((DIFF_V_DIM, 2 * tq), F32)
        kc = DIFF_VALUE_CHUNK
        for c in range(seq // kc):
            p = jnp.exp2(s_ref[t, c * kc:(c + 1) * kc, :] - m)
            sm = sm + jnp.sum(p.reshape(kc // SUBLANES, SUBLANES, 2 * tq), axis=0)
            acc = acc + jnp.dot(vt_ref[:, c * kc:(c + 1) * kc], p.astype(BF16),
                                preferred_element_type=F32)
        return jnp.sum(sm, axis=0, keepdims=True), acc

    def values(t, l_acc):
        l, acc = l_acc
        o = (acc[:, :tq] / l[:, :tq] - lam * (acc[:, tq:] / l[:, tq:])).T
        ms = jnp.mean(o * o, axis=-1, keepdims=True)
        o = o * lax.rsqrt(ms + RMS_EPS) * g_ref[...] * (1.0 - lambda_init)
        o_ref[t * tq:(t + 1) * tq, :] = o.astype(o_ref.dtype)

    m_next = scores(0)
    for t in range(tiles):
        m = m_next
        if t + 1 < tiles:
            m_next = scores(t + 1)
        values(t, probs(t, m))


def _toeplitz(t, rows, width):
    h, length = t.shape
    assert length == rows + width - 1
    flat = jnp.tile(jnp.pad(t, ((0, 0), (0, 1))), (1, rows))[:, :rows * length]
    return flat.reshape(h, rows, length)[:, :, rows - 1:]


def _diff_bias_strip(rel_diff, tq):
    kb = DIFF_KEY_BLOCK
    assert DIFF_NEAR_BLOCKS * kb - (tq - 1) > REL_MAX_DIST
    assert -DIFF_NEAR_BLOCKS_NEG * kb + kb - 1 < -REL_MAX_DIST
    width = tq + (DIFF_NEAR_BLOCKS + DIFF_NEAR_BLOCKS_NEG) * kb
    rel = jnp.arange(kb + width - 1) - (width - 1) + DIFF_NEAR_BLOCKS * kb
    t = rel_diff[_rel_bucket(rel)].T.astype(F32) * LOG2E
    return _toeplitz(t, kb, width)[:, ::-1, ::-1]


def _diff_attention(dq, dk, dv, lam_p, strip, subln_g, batch, seq, lambda_init):
    tq = DIFF_TQ
    tiles = min(DIFF_TILES_PER_STEP, seq // tq)
    ts = tiles * tq
    nq = seq // ts
    width = strip.shape[-1]
    est = (2 * ts * LANES * 2 * 2 + 2 * 2 * seq * LANES * 2 + 2 * DIFF_KEY_BLOCK * width * 4
           + seq * LANES * 2 + tiles * seq * 2 * tq * 4 + 2 * DIFF_KEY_CHUNK * 2 * tq * 4)
    return pl.pallas_call(
        functools.partial(_diff_attn_kernel, seq=seq, lambda_init=lambda_init),
        grid=(batch, DIFF_HEADS, nq),
        in_specs=[
            pl.BlockSpec((4, DIFF_HEAD_DIM), lambda b, h, i: (0, 0)),
            pl.BlockSpec((ts, LANES), lambda b, h, i: (b * nq + i, h)),
            pl.BlockSpec((seq, LANES), lambda b, h, i: (b, h)),
            pl.BlockSpec((seq, LANES), lambda b, h, i: (b, h)),
            pl.BlockSpec((1, DIFF_KEY_BLOCK, width), lambda b, h, i: (h, 0, 0)),
            pl.BlockSpec((1, DIFF_V_DIM), lambda b, h, i: (0, 0)),
        ],
        out_specs=pl.BlockSpec((ts, LANES), lambda b, h, i: (b * nq + i, h)),
        out_shape=jax.ShapeDtypeStruct((batch * seq, DIFF_WIDTH), BF16),
        scratch_shapes=[pltpu.VMEM((DIFF_V_DIM, seq), BF16),
                        pltpu.VMEM((tiles, seq, 2 * tq), F32)],
        compiler_params=pltpu.CompilerParams(
            dimension_semantics=("parallel", "parallel", "arbitrary"),
            vmem_limit_bytes=_vmem_limit(est)),
        name="diff_attn",
    )(lam_p, dq, dk, dv, strip, subln_g)


WIN_BAND = 3 * WIN_BLOCK
WIN_STRIP = 5 * WIN_BLOCK


def _win_attn_kernel(q_ref, k_ref, v_ref, strip_ref, sink_ref, o_ref, *, seq):
    row = lax.broadcasted_iota(jnp.int32, (LANES, 1), 0)
    lane = lax.broadcasted_iota(jnp.int32, (1, LANES), 1)
    blocks = q_ref.shape[0] // WIN_BLOCK
    for sub in range(blocks):
        _win_block(pl.program_id(1) * blocks + sub, sub, row, lane, q_ref, k_ref, v_ref, strip_ref,
                   sink_ref, o_ref, seq)


def _win_block(n, sub, row, lane, q_ref, k_ref, v_ref, strip_ref, sink_ref, o_ref, seq):
    rows = slice(sub * WIN_BLOCK, (sub + 1) * WIN_BLOCK)
    start = pl.multiple_of(jnp.clip(n * WIN_BLOCK - WIN_BLOCK, 0, seq - WIN_BAND), WIN_BLOCK)
    ustart = pl.multiple_of(start - n * WIN_BLOCK + 2 * WIN_BLOCK, WIN_BLOCK)
    for kv in range(WIN_KV_HEADS):
        kb = k_ref[pl.ds(start, WIN_BAND), kv * LANES:(kv + 1) * LANES]
        vb = v_ref[pl.ds(start, WIN_BAND), kv * LANES:(kv + 1) * LANES]
        cols = []
        for pair in range(WIN_GROUP // 2):
            c0 = kv * WIN_GROUP * WIN_HEAD_DIM + pair * LANES
            qt = q_ref[rows, c0:c0 + LANES].astype(F32).T
            cols.append(jnp.where(row < WIN_HEAD_DIM, qt, 0.0))
            cols.append(jnp.where(row >= WIN_HEAD_DIM, qt, 0.0))
        qcat = jnp.concatenate(cols, axis=1).astype(BF16)
        s = jnp.dot(kb, qcat, preferred_element_type=F32)
        s = s + strip_ref[kv, pl.ds(ustart, WIN_BAND), :]
        sink = sink_ref[kv]
        m = jnp.maximum(jnp.max(s, axis=0, keepdims=True), sink)
        p = jnp.exp2(s - m)
        den = jnp.sum(p, axis=0, keepdims=True) + jnp.exp2(sink - m)
        vt = vb.astype(F32).T.astype(BF16)
        o = (jnp.dot(vt, p.astype(BF16), preferred_element_type=F32) / den).T
        for pair in range(WIN_GROUP // 2):
            c0 = kv * WIN_GROUP * WIN_HEAD_DIM + pair * LANES
            even = o[(2 * pair) * WIN_BLOCK:(2 * pair + 1) * WIN_BLOCK, :]
            odd = o[(2 * pair + 1) * WIN_BLOCK:(2 * pair + 2) * WIN_BLOCK, :]
            o_ref[rows, c0:c0 + LANES] = jnp.where(lane < WIN_HEAD_DIM, even, odd).astype(o_ref.dtype)


def _win_bias_strip(rel_win):
    rel = jnp.arange(WIN_BLOCK + WIN_STRIP - 1) - (WIN_BLOCK - 1) - 2 * WIN_BLOCK
    t = jnp.where((jnp.abs(rel) <= WINDOW)[None],
                  rel_win[_rel_bucket(rel)].T.astype(F32) * LOG2E, NEG_INF)
    strip = _toeplitz(t, WIN_BLOCK, WIN_STRIP)
    strip = strip.reshape(WIN_KV_HEADS, WIN_GROUP, WIN_BLOCK, WIN_STRIP).transpose(0, 3, 1, 2)
    return strip.reshape(WIN_KV_HEADS, WIN_STRIP, WIN_GROUP * WIN_BLOCK)


def _win_attention(wq, wk, wv, sink, strip, batch, seq):
    heads_lanes = WIN_GROUP * WIN_BLOCK
    sink_lanes = jnp.broadcast_to((sink.astype(F32) * LOG2E).reshape(WIN_KV_HEADS, WIN_GROUP, 1),
                                  (WIN_KV_HEADS, WIN_GROUP, WIN_BLOCK)).reshape(
                                      WIN_KV_HEADS, 1, heads_lanes)
    rows = WIN_BLOCKS_PER_STEP * WIN_BLOCK
    assert seq % rows == 0
    nb = seq // rows
    est = (2 * rows * WIN_WIDTH * 2 * 2 + 2 * 2 * seq * 2 * LANES * 2
           + 2 * WIN_KV_HEADS * WIN_STRIP * heads_lanes * 4 + 8 * WIN_BAND * heads_lanes * 4)
    return pl.pallas_call(
        functools.partial(_win_attn_kernel, seq=seq),
        grid=(batch, nb),
        in_specs=[
            pl.BlockSpec((rows, WIN_WIDTH), lambda b, n: (b * nb + n, 0)),
            pl.BlockSpec((seq, 2 * LANES), lambda b, n: (b, 0)),
            pl.BlockSpec((seq, 2 * LANES), lambda b, n: (b, 0)),
            pl.BlockSpec((WIN_KV_HEADS, WIN_STRIP, heads_lanes), lambda b, n: (0, 0, 0)),
            pl.BlockSpec((WIN_KV_HEADS, 1, heads_lanes), lambda b, n: (0, 0, 0)),
        ],
        out_specs=pl.BlockSpec((rows, WIN_WIDTH), lambda b, n: (b * nb + n, 0)),
        out_shape=jax.ShapeDtypeStruct((batch * seq, WIN_WIDTH), BF16),
        compiler_params=pltpu.CompilerParams(
            dimension_semantics=("parallel", "parallel"), vmem_limit_bytes=_vmem_limit(est)),
        name="win_attn",
    )(wq, wk, wv, strip, sink_lanes)


ROUTE_E1, ROUTE_E2, ROUTE_W1, ROUTE_W2 = 0, 1, 2, 3


def _merge_kernel(x_ref, od_ref, ow_ref, gate_ref, pd_ref, pw_ref, wo_ref, g_ref, wr_ref, br_ref,
                  x1_ref, hn_ref, route_ref, counts_ref):
    @pl.when(pl.program_id(0) == 0)
    def _():
        counts_ref[...] = jnp.zeros_like(counts_ref)

    sub_rows = x_ref.shape[0] // MERGE_SUBTILES
    hits = [_merge_rows(slice(i * sub_rows, (i + 1) * sub_rows), x_ref, od_ref, ow_ref, gate_ref,
                        pd_ref, pw_ref, wo_ref, g_ref, wr_ref, br_ref, x1_ref, hn_ref, route_ref)
            for i in range(MERGE_SUBTILES)]
    counts_ref[...] += sum(hits[1:], hits[0])


def _merge_rows(rows, x_ref, od_ref, ow_ref, gate_ref, pd_ref, pw_ref, wo_ref, g_ref, wr_ref, br_ref,
                x1_ref, hn_ref, route_ref):
    md = jnp.dot(od_ref[rows, :], pd_ref[...], preferred_element_type=F32)
    mw = jnp.dot(ow_ref[rows, :], pw_ref[...], preferred_element_type=F32)
    merged = (gate_ref[rows, :D_MODEL].astype(F32) * md + gate_ref[rows, D_MODEL:].astype(F32) * mw)
    x1 = x_ref[rows, :] + jnp.dot(merged.astype(BF16), wo_ref[...], preferred_element_type=F32)
    x1_ref[rows, :] = x1
    ms = jnp.mean(x1 * x1, axis=-1, keepdims=True)
    hn = x1 * lax.rsqrt(ms + RMS_EPS) * g_ref[...]
    hn_ref[rows, :] = hn

    hn_hi = hn.astype(BF16)
    hn_lo = (hn - hn_hi.astype(F32)).astype(BF16)
    parts = (jnp.dot(hn_hi, wr_ref[...], preferred_element_type=F32)
             + jnp.dot(hn_lo, wr_ref[...], preferred_element_type=F32))
    logits = parts + pltpu.roll(parts, LANES // 2, axis=1) + br_ref[...]
    lane = lax.broadcasted_iota(jnp.int32, logits.shape, 1)
    big = jnp.int32(LANES)
    is_group = lane < N_GROUPS
    gl = jnp.where(is_group, logits, NEG_INF)
    gmax = jnp.max(gl, axis=-1, keepdims=True)
    gsel = jnp.min(jnp.where(gl == gmax, lane, big), axis=-1, keepdims=True)
    gden = jnp.sum(jnp.where(is_group, jnp.exp(gl - gmax), 0.0), axis=-1, keepdims=True)
    gw = 1.0 / gden
    lo = N_GROUPS + EXPERTS_PER_GROUP * gsel
    in_group = jnp.logical_and(lane >= lo, lane < lo + EXPERTS_PER_GROUP)
    el = jnp.where(in_group, logits, NEG_INF)
    t1 = jnp.max(el, axis=-1, keepdims=True)
    i1 = jnp.min(jnp.where(el == t1, lane, big), axis=-1, keepdims=True)
    el2 = jnp.where(lane == i1, NEG_INF, el)
    t2 = jnp.max(el2, axis=-1, keepdims=True)
    i2 = jnp.min(jnp.where(el2 == t2, lane, big), axis=-1, keepdims=True)
    z = jnp.exp(t2 - t1)
    w1 = gw / (1.0 + z)
    w2 = gw * z / (1.0 + z)
    e1 = (i1 - N_GROUPS).astype(F32)
    e2 = (i2 - N_GROUPS).astype(F32)
    route = jnp.where(lane == ROUTE_E1, e1,
                      jnp.where(lane == ROUTE_E2, e2,
                                jnp.where(lane == ROUTE_W1, w1,
                                          jnp.where(lane == ROUTE_W2, w2, 0.0))))
    route_ref[rows, :] = route
    hits = (jnp.where(lane == i1 - N_GROUPS, 1.0, 0.0) + jnp.where(lane == i2 - N_GROUPS, 1.0, 0.0))
    return jnp.sum(hits, axis=0, keepdims=True)


def _merge(x2, od, ow, gates, pd, pw, wo, g, wr, br):
    n = x2.shape[0]
    tm = ROW_TILE
    est = (2 * tm * D_MODEL * 4 * 3 + 2 * tm * (DIFF_WIDTH + WIN_WIDTH + C_GATE) * 2
           + 2 * (DIFF_WIDTH + WIN_WIDTH + D_MODEL) * D_MODEL * 2 + D_MODEL * LANES * 4 * 2
           + 6 * tm * D_MODEL * 4)
    row = lambda c: pl.BlockSpec((tm, c), lambda i: (i, 0))
    full = lambda r, c: pl.BlockSpec((r, c), lambda i: (0, 0))
    return pl.pallas_call(
        _merge_kernel,
        grid=(n // tm,),
        in_specs=[row(D_MODEL), row(DIFF_WIDTH), row(WIN_WIDTH), row(C_GATE),
                  full(DIFF_WIDTH, D_MODEL), full(WIN_WIDTH, D_MODEL), full(D_MODEL, D_MODEL),
                  full(1, D_MODEL), full(D_MODEL, LANES), full(1, LANES)],
        out_specs=[row(D_MODEL), row(D_MODEL), row(LANES), full(1, LANES)],
        out_shape=[jax.ShapeDtypeStruct((n, D_MODEL), F32), jax.ShapeDtypeStruct((n, D_MODEL), F32),
                   jax.ShapeDtypeStruct((n, LANES), F32), jax.ShapeDtypeStruct((1, LANES), F32)],
        compiler_params=pltpu.CompilerParams(
            dimension_semantics=("arbitrary",), vmem_limit_bytes=_vmem_limit(est)),
        name="merge_route",
    )(x2, od, ow, gates, pd, pw, wo, g, wr, br)


def _positions_kernel(route_ref, pstart_ref, dest_ref, carry_ref):
    i = pl.program_id(0)

    @pl.when(i == 0)
    def _():
        carry_ref[...] = jnp.zeros_like(carry_ref)

    r = route_ref[...]
    tb = r.shape[0]
    lane = lax.broadcasted_iota(jnp.int32, r.shape, 1)
    lane_f = lane.astype(F32)
    oh1 = lane_f == r[:, ROUTE_E1:ROUTE_E1 + 1]
    oh2 = lane_f == r[:, ROUTE_E2:ROUTE_E2 + 1]
    cnt = jnp.where(oh1, 1.0, 0.0) + jnp.where(oh2, 1.0, 0.0)
    rows = lax.broadcasted_iota(jnp.int32, (tb, tb), 0)
    cols = lax.broadcasted_iota(jnp.int32, (tb, tb), 1)
    tri = jnp.where(rows >= cols, 1.0, 0.0).astype(BF16)
    incl = jnp.dot(tri, cnt.astype(BF16), preferred_element_type=F32)
    row_of = incl - cnt + carry_ref[...] + pstart_ref[...]
    dest1 = jnp.sum(jnp.where(oh1, row_of, 0.0), axis=-1, keepdims=True)
    dest2 = jnp.sum(jnp.where(oh2, row_of, 0.0), axis=-1, keepdims=True)
    dest_ref[...] = jnp.where(lane == ROUTE_E1, dest1, jnp.where(lane == ROUTE_E2, dest2, 0.0))
    carry_ref[...] = carry_ref[...] + incl[tb - 1:tb, :]


def _positions(route, pad_start):
    n = route.shape[0]
    tb = POS_TILE
    return pl.pallas_call(
        _positions_kernel,
        grid=(n // tb,),
        in_specs=[pl.BlockSpec((tb, LANES), lambda i: (i, 0)),
                  pl.BlockSpec((1, LANES), lambda i: (0, 0))],
        out_specs=pl.BlockSpec((tb, LANES), lambda i: (i, 0)),
        out_shape=jax.ShapeDtypeStruct((n, LANES), F32),
        scratch_shapes=[pltpu.VMEM((1, LANES), F32)],
        compiler_params=pltpu.CompilerParams(dimension_semantics=("arbitrary",)),
        name="positions",
    )(route, pad_start)


def _dispatch_kernel(d1_ref, d2_ref, hn_ref, xs_init_hbm, xs_hbm, sem):
    del xs_init_hbm
    ts = hn_ref.shape[0]
    for t in range(ts):
        pltpu.make_async_copy(hn_ref.at[pl.ds(t, 1)], xs_hbm.at[pl.ds(d1_ref[t], 1)],
                              sem).start(priority=0)
        pltpu.make_async_copy(hn_ref.at[pl.ds(t, 1)], xs_hbm.at[pl.ds(d2_ref[t], 1)],
                              sem).start(priority=1)
    for _ in range(2):
        pltpu.make_async_copy(hn_ref, xs_hbm.at[pl.ds(0, ts)], sem).wait()


def _dispatch(dest1, dest2, hn, xs_init):
    n = hn.shape[0]
    ts = MOVE_TILE
    smem_blk = pl.BlockSpec((ts,), lambda i: (i,), memory_space=pltpu.SMEM)
    return pl.pallas_call(
        _dispatch_kernel,
        grid=(n // ts,),
        in_specs=[smem_blk, smem_blk,
                  pl.BlockSpec((ts, D_MODEL), lambda i: (i, 0)), pl.BlockSpec(memory_space=pl.ANY)],
        out_specs=pl.BlockSpec(memory_space=pl.ANY),
        out_shape=jax.ShapeDtypeStruct(xs_init.shape, F32),
        scratch_shapes=[pltpu.SemaphoreType.DMA(())],
        input_output_aliases={3: 0},
        compiler_params=pltpu.CompilerParams(dimension_semantics=("arbitrary",)),
        name="dispatch",
    )(dest1, dest2, hn, xs_init)


def _experts_kernel(be_ref, na_ref, xs_ref, wg_ref, wu_ref, wd_ref, ys_ref, wg_bf, wu_bf, wd_bf):
    j = pl.program_id(0)
    active = j < na_ref[0]
    new_expert = jnp.logical_or(j == 0, be_ref[j] != be_ref[jnp.maximum(j - 1, 0)])

    @pl.when(jnp.logical_and(active, new_expert))
    def _():
        wg_bf[...] = wg_ref[0, 0].astype(BF16)
        wu_bf[...] = wu_ref[0, 0].astype(BF16)
        wd_bf[...] = wd_ref[0, 0].astype(BF16)

    @pl.when(active)
    def _():
        x = xs_ref[...].astype(BF16)
        g = jnp.dot(x, wg_bf[...], preferred_element_type=F32)
        u = jnp.dot(x, wu_bf[...], preferred_element_type=F32)
        hid = (g * jax.nn.sigmoid(g) * u).astype(BF16)
        ys_ref[...] = jnp.dot(hid, wd_bf[...], preferred_element_type=F32)

    @pl.when(j >= na_ref[0])
    def _():
        ys_ref[...] = jnp.zeros_like(ys_ref)


def _experts(block_e, n_active, xs, wg, wu, wd, layer):
    n_rows = xs.shape[0]
    tmb = MOE_TILE
    nblk = n_rows // tmb
    rows = pl.BlockSpec((tmb, D_MODEL), lambda j, be, na: (j, 0))
    est = (4 * tmb * D_MODEL * 4 + 3 * D_MODEL * EXPERT_HIDDEN * (2 * 4 + 2)
           + 4 * tmb * EXPERT_HIDDEN * 4 + tmb * D_MODEL * 4)
    return pl.pallas_call(
        _experts_kernel,
        grid_spec=pltpu.PrefetchScalarGridSpec(
            num_scalar_prefetch=2,
            grid=(nblk,),
            in_specs=[
                rows,
                pl.BlockSpec((1, 1, D_MODEL, EXPERT_HIDDEN), lambda j, be, na: (layer, be[j], 0, 0)),
                pl.BlockSpec((1, 1, D_MODEL, EXPERT_HIDDEN), lambda j, be, na: (layer, be[j], 0, 0)),
                pl.BlockSpec((1, 1, EXPERT_HIDDEN, D_MODEL), lambda j, be, na: (layer, be[j], 0, 0)),
            ],
            out_specs=rows,
            scratch_shapes=[pltpu.VMEM((D_MODEL, EXPERT_HIDDEN), BF16),
                            pltpu.VMEM((D_MODEL, EXPERT_HIDDEN), BF16),
                            pltpu.VMEM((EXPERT_HIDDEN, D_MODEL), BF16)],
        ),
        out_shape=jax.ShapeDtypeStruct((n_rows, D_MODEL), F32),
        compiler_params=pltpu.CompilerParams(
            dimension_semantics=("arbitrary",), vmem_limit_bytes=_vmem_limit(est)),
        name="experts",
    )(block_e, n_active, xs, wg, wu, wd)


def _plan_blocks(counts, n_rows):
    counts = counts[0, :N_EXPERTS].astype(jnp.int32)
    padded = (counts + MOE_TILE - 1) // MOE_TILE * MOE_TILE
    pad_end = jnp.cumsum(padded)
    pad_start = pad_end - padded
    blk_start = jnp.arange(n_rows // MOE_TILE, dtype=jnp.int32) * MOE_TILE
    block_e = jnp.minimum(jnp.sum((pad_end[None, :] <= blk_start[:, None]).astype(jnp.int32), axis=1),
                          N_EXPERTS - 1)
    n_active = (pad_end[-1:] // MOE_TILE).astype(jnp.int32)
    pad_start_lanes = jnp.zeros((1, LANES), F32).at[0, :N_EXPERTS].set(pad_start.astype(F32))
    return pad_start_lanes, block_e, n_active


def _forward(x, w_in, diff_lambda, diff_subln, win_sink, w_branch_diff, w_branch_win, w_out,
             rel_bias, norm_mix, norm_ffn, w_router_group, b_router_group, w_router_expert,
             b_router_expert, w_exp_gate, w_exp_up, w_exp_down, norm_final):
    batch, seq, _ = x.shape
    n = batch * seq
    depth = w_in.shape[0]
    assert seq % DIFF_KEY_CHUNK == 0 and seq % DIFF_TQ == 0 and seq >= WIN_BAND
    assert n % ROW_TILE == 0 and n % POS_TILE == 0 and n % MOVE_TILE == 0

    diff_strip = _diff_bias_strip(rel_bias[:, :DIFF_HEADS], DIFF_TQ)
    win_strip = _win_bias_strip(rel_bias[:, DIFF_HEADS:])
    n_rows = -(-(2 * n + N_EXPERTS * (MOE_TILE - 1)) // MOE_TILE) * MOE_TILE
    xs = jnp.zeros((n_rows, D_MODEL), F32)

    x2 = x.reshape(n, D_MODEL)
    moe = None
    for l in range(depth):
        splits = (512, 1024, 1536, 2048, 2176, 2304)
        wdq, wdk, wdv, wwq, wwk, wwv, wgt = jnp.split(w_in[l], splits, axis=-1)
        dup = lambda w: jnp.concatenate(
            [w[:, :WIN_HEAD_DIM], w[:, :WIN_HEAD_DIM], w[:, WIN_HEAD_DIM:], w[:, WIN_HEAD_DIM:]],
            axis=-1)
        w_all = jnp.concatenate([wdq, wdk, wdv, wwq, dup(wwk), dup(wwv), wgt], axis=-1).astype(BF16)

        if moe is None:
            dq, dk, dv, wq, wk, wv, gates = _inproj(x2, norm_mix[l][None], w_all)
        else:
            x2, dq, dk, dv, wq, wk, wv, gates = _combine_inproj(*moe, norm_mix[l][None], w_all)

        lambda_init = 0.8 - 0.6 * math.exp(-0.3 * l)
        o_diff = _diff_attention(dq, dk, dv, diff_lambda[l], diff_strip, diff_subln[l][None],
                                 batch, seq, lambda_init)
        o_win = _win_attention(wq, wk, wv, win_sink[l], win_strip, batch, seq)

        wr = jnp.zeros((D_MODEL, LANES // 2), F32)
        wr = wr.at[:, :N_GROUPS].set(w_router_group[l])
        wr = wr.at[:, N_GROUPS:N_GROUPS + N_EXPERTS].set(w_router_expert[l])
        wr_hi = wr.astype(BF16)
        wr_lo = (wr - wr_hi.astype(F32)).astype(BF16)
        wr = jnp.concatenate([wr_hi, wr_lo], axis=1)
        br = jnp.zeros((1, LANES), F32)
        br = br.at[0, :N_GROUPS].set(b_router_group[l])
        br = br.at[0, N_GROUPS:N_GROUPS + N_EXPERTS].set(b_router_expert[l])
        x1, hn, route, counts = _merge(x2, o_diff, o_win, gates,
                                       w_branch_diff[l].astype(BF16), w_branch_win[l].astype(BF16),
                                       w_out[l].astype(BF16), norm_ffn[l][None], wr, br)

        pad_start, block_e, n_active = _plan_blocks(counts, n_rows)
        dest = _positions(route, pad_start)
        dest1 = dest[:, ROUTE_E1].astype(jnp.int32)
        dest2 = dest[:, ROUTE_E2].astype(jnp.int32)
        xs = _dispatch(dest1, dest2, hn, xs)
        ys = _experts(block_e, n_active, xs, w_exp_gate, w_exp_up, w_exp_down, l)
        moe = (dest1, dest2, route, x1, ys)
    dest1, dest2, route, x1, ys = moe
    out = _combine_final(dest1, dest2, route, x1, ys, norm_final[None])
    return out.reshape(batch, seq, D_MODEL)


def kernel(x, w_in, diff_lambda, diff_subln, win_sink, w_branch_diff, w_branch_win, w_out, rel_bias, norm_mix, norm_ffn, w_router_group, b_router_group, w_router_expert, b_router_expert, w_exp_gate, w_exp_up, w_exp_down, norm_final):
    return _forward(x, w_in, diff_lambda, diff_subln, win_sink, w_branch_diff, w_branch_win, w_out,
                    rel_bias, norm_mix, norm_ffn, w_router_group, b_router_group, w_router_expert,
                    b_router_expert, w_exp_gate, w_exp_up, w_exp_down, norm_final)
```

```python
import functools
import math

import jax
import jax.numpy as jnp
from jax import lax
from jax.experimental import pallas as pl
from jax.experimental.pallas import tpu as pltpu

D_MODEL = 1024
DIFF_HEADS = 4
DIFF_HEAD_DIM = 64
DIFF_V_DIM = 2 * DIFF_HEAD_DIM
DIFF_WIDTH = DIFF_HEADS * DIFF_V_DIM
WIN_HEADS = 8
WIN_KV_HEADS = 2
WIN_GROUP = WIN_HEADS // WIN_KV_HEADS
WIN_HEAD_DIM = 64
WIN_WIDTH = WIN_HEADS * WIN_HEAD_DIM
WINDOW = 128
WIN_BLOCK = 128
REL_BUCKETS = 32
REL_MAX_DIST = 128
N_GROUPS = 4
EXPERTS_PER_GROUP = 8
N_EXPERTS = N_GROUPS * EXPERTS_PER_GROUP
EXPERT_HIDDEN = 512
RMS_EPS = 1e-6
NEG_INF = -1e30

LANES = 128
SUBLANES = 8
V7X_VMEM_BYTES = 64 * 1024 * 1024

ROW_TILE = 512
MERGE_SUBTILES = 2
DIFF_TQ = 256
DIFF_TILES_PER_STEP = 8
DIFF_KEY_BLOCK = 128
DIFF_KEY_CHUNK = 512
DIFF_VALUE_CHUNK = 256
DIFF_NEAR_BLOCKS = 3
DIFF_NEAR_BLOCKS_NEG = 2
LOG2E = math.log2(math.e)
WIN_BLOCKS_PER_STEP = 8
POS_TILE = 512
MOE_TILE = 512
MOVE_TILE = 512

F32 = jnp.float32
BF16 = jnp.bfloat16


def _vmem_limit(nbytes):
    return int(min(max(2 * nbytes, 16 * 1024 * 1024), V7X_VMEM_BYTES - 8 * 1024 * 1024))


def _rel_bucket(rel):
    half = REL_BUCKETS // 2
    max_exact = half // 2
    n = jnp.abs(rel)
    nf = jnp.maximum(n, max_exact).astype(jnp.float32)
    large = max_exact + (jnp.log(nf / max_exact) / math.log(REL_MAX_DIST / max_exact)
                         * (half - max_exact)).astype(jnp.int32)
    large = jnp.minimum(large, half - 1)
    return jnp.where(rel > 0, half, 0) + jnp.where(n < max_exact, n, large)


C_DQ, C_DK, C_DV, C_WQ = 512, 512, 512, 512
C_WKD, C_WVD = 2 * LANES, 2 * LANES
C_GATE = 2 * D_MODEL
IN_COLS = C_DQ + C_DK + C_DV + C_WQ + C_WKD + C_WVD + C_GATE


def _project(x, g_ref, w_ref, dq_ref, dk_ref, dv_ref, wq_ref, wk_ref, wv_ref, gate_ref):
    ms = jnp.mean(x * x, axis=-1, keepdims=True)
    h = (x * lax.rsqrt(ms + RMS_EPS) * g_ref[...]).astype(BF16)

    col = 0

    def proj(width):
        nonlocal col
        out = jnp.dot(h, w_ref[:, col:col + width], preferred_element_type=F32)
        col += width
        return out

    dq_ref[...] = (proj(C_DQ) * (DIFF_HEAD_DIM ** -0.5 * LOG2E)).astype(BF16)
    dk_ref[...] = proj(C_DK).astype(BF16)
    dv_ref[...] = proj(C_DV).astype(BF16)
    wq_ref[...] = (proj(C_WQ) * (WIN_HEAD_DIM ** -0.5 * LOG2E)).astype(BF16)
    wk_ref[...] = proj(C_WKD).astype(BF16)
    wv_ref[...] = proj(C_WVD).astype(BF16)
    gate_ref[...] = jax.nn.sigmoid(proj(C_GATE)).astype(BF16)


def _inproj_kernel(x_ref, g_ref, w_ref, *out_refs):
    _project(x_ref[...], g_ref, w_ref, *out_refs)


PROJ_WIDTHS = (C_DQ, C_DK, C_DV, C_WQ, C_WKD, C_WVD, C_GATE)


def _inproj(x2, g, w):
    n = x2.shape[0]
    tm = ROW_TILE
    est = 2 * tm * D_MODEL * 4 + D_MODEL * IN_COLS * 2 + 2 * tm * IN_COLS * 2 + tm * IN_COLS * 4
    return pl.pallas_call(
        _inproj_kernel,
        grid=(n // tm,),
        in_specs=[
            pl.BlockSpec((tm, D_MODEL), lambda i: (i, 0)),
            pl.BlockSpec((1, D_MODEL), lambda i: (0, 0)),
            pl.BlockSpec((D_MODEL, IN_COLS), lambda i: (0, 0), pipeline_mode=pl.Buffered(1)),
        ],
        out_specs=[pl.BlockSpec((tm, c), lambda i: (i, 0)) for c in PROJ_WIDTHS],
        out_shape=[jax.ShapeDtypeStruct((n, c), BF16) for c in PROJ_WIDTHS],
        compiler_params=pltpu.CompilerParams(
            dimension_semantics=("parallel",), vmem_limit_bytes=_vmem_limit(est)),
        name="inproj",
    )(x2, g, w)


def _combined_rows(d1_next_ref, d2_next_ref, d1_first_ref, d2_first_ref, route_ref, x1_ref, ys_hbm,
                   buf, sem):
    i = pl.program_id(0)
    tm = x1_ref.shape[0]
    slot = i % 2

    def gather(i1_ref, i2_ref, dst):
        for t in range(tm):
            pltpu.make_async_copy(ys_hbm.at[pl.ds(i1_ref[t], 1)], buf.at[dst, pl.ds(t, 1)],
                                  sem.at[dst]).start(priority=0)
            pltpu.make_async_copy(ys_hbm.at[pl.ds(i2_ref[t], 1)], buf.at[dst, pl.ds(tm + t, 1)],
                                  sem.at[dst]).start(priority=1)

    @pl.when(i == 0)
    def _():
        gather(d1_first_ref, d2_first_ref, 0)

    @pl.when(i + 1 < pl.num_programs(0))
    def _():
        gather(d1_next_ref, d2_next_ref, 1 - slot)

    pltpu.make_async_copy(ys_hbm.at[pl.ds(0, 2 * tm)], buf.at[slot], sem.at[slot]).wait()
    r = route_ref[...]
    w1 = r[:, ROUTE_W1:ROUTE_W1 + 1]
    w2 = r[:, ROUTE_W2:ROUTE_W2 + 1]
    return x1_ref[...] + w1 * buf[slot, 0:tm, :] + w2 * buf[slot, tm:2 * tm, :]


def _combine_inproj_kernel(d1_next_ref, d2_next_ref, d1_first_ref, d2_first_ref, route_ref, x1_ref,
                           g_ref, w_ref, ys_hbm, x2_ref, *rest):
    out_refs, (buf, sem) = rest[:len(PROJ_WIDTHS)], rest[len(PROJ_WIDTHS):]
    x = _combined_rows(d1_next_ref, d2_next_ref, d1_first_ref, d2_first_ref, route_ref, x1_ref,
                       ys_hbm, buf, sem)
    x2_ref[...] = x
    _project(x, g_ref, w_ref, *out_refs)


def _combine_final_kernel(d1_next_ref, d2_next_ref, d1_first_ref, d2_first_ref, route_ref, x1_ref,
                          g_ref, ys_hbm, o_ref, buf, sem):
    x = _combined_rows(d1_next_ref, d2_next_ref, d1_first_ref, d2_first_ref, route_ref, x1_ref,
                       ys_hbm, buf, sem)
    ms = jnp.mean(x * x, axis=-1, keepdims=True)
    o_ref[...] = x * lax.rsqrt(ms + RMS_EPS) * g_ref[...]


def _combine_specs(n):
    tm = ROW_TILE
    nt = n // tm
    nxt = pl.BlockSpec((tm,), lambda i: (jnp.minimum(i + 1, nt - 1),), memory_space=pltpu.SMEM)
    first = pl.BlockSpec((tm,), lambda i: (0,), memory_space=pltpu.SMEM)
    specs = [nxt, nxt, first, first,
             pl.BlockSpec((tm, LANES), lambda i: (i, 0)),
             pl.BlockSpec((tm, D_MODEL), lambda i: (i, 0))]
    scratch = [pltpu.VMEM((2, 2 * tm, D_MODEL), F32), pltpu.SemaphoreType.DMA((2,))]
    return tm, nt, specs, scratch


def _combine_final(dest1, dest2, route, x1, ys, g):
    n = x1.shape[0]
    tm, nt, specs, scratch = _combine_specs(n)
    est = 4 * tm * D_MODEL * 4 + 2 * 2 * tm * D_MODEL * 4 + 4 * tm * D_MODEL * 4
    return pl.pallas_call(
        _combine_final_kernel,
        grid=(nt,),
        in_specs=specs + [pl.BlockSpec((1, D_MODEL), lambda i: (0, 0)),
                          pl.BlockSpec(memory_space=pl.ANY)],
        out_specs=pl.BlockSpec((tm, D_MODEL), lambda i: (i, 0)),
        out_shape=jax.ShapeDtypeStruct((n, D_MODEL), F32),
        scratch_shapes=scratch,
        compiler_params=pltpu.CompilerParams(
            dimension_semantics=("arbitrary",), vmem_limit_bytes=_vmem_limit(est)),
        name="combine_final",
    )(dest1, dest2, dest1, dest2, route, x1, g, ys)


def _combine_inproj(dest1, dest2, route, x1, ys, g, w):
    n = x1.shape[0]
    tm, nt, specs, scratch = _combine_specs(n)
    est = (4 * tm * D_MODEL * 4 + D_MODEL * IN_COLS * 2 + 2 * tm * IN_COLS * 2 + tm * IN_COLS * 4
           + 2 * 2 * tm * D_MODEL * 4)
    return pl.pallas_call(
        _combine_inproj_kernel,
        grid=(nt,),
        in_specs=specs + [
            pl.BlockSpec((1, D_MODEL), lambda i: (0, 0)),
            pl.BlockSpec((D_MODEL, IN_COLS), lambda i: (0, 0), pipeline_mode=pl.Buffered(1)),
            pl.BlockSpec(memory_space=pl.ANY),
        ],
        out_specs=[pl.BlockSpec((tm, D_MODEL), lambda i: (i, 0))]
        + [pl.BlockSpec((tm, c), lambda i: (i, 0)) for c in PROJ_WIDTHS],
        out_shape=[jax.ShapeDtypeStruct((n, D_MODEL), F32)]
        + [jax.ShapeDtypeStruct((n, c), BF16) for c in PROJ_WIDTHS],
        scratch_shapes=scratch,
        compiler_params=pltpu.CompilerParams(
            dimension_semantics=("arbitrary",), vmem_limit_bytes=_vmem_limit(est)),
        name="combine_inproj",
    )(dest1, dest2, dest1, dest2, route, x1, g, w, ys)


def _diff_attn_kernel(lam_ref, q_ref, k_ref, v_ref, bias_ref, g_ref, o_ref, vt_ref, s_ref, *,
                      seq, lambda_init):
    qi = pl.program_id(2)

    @pl.when(qi == 0)
    def _():
        vt_ref[...] = v_ref[...].astype(F32).T.astype(BF16)

    lp = lam_ref[...]
    lam = (jnp.exp(jnp.sum(lp[0:1] * lp[1:2], axis=-1, keepdims=True))
           - jnp.exp(jnp.sum(lp[2:3] * lp[3:4], axis=-1, keepdims=True)) + lambda_init)

    tq = DIFF_TQ
    tiles = q_ref.shape[0] // tq
    kb = DIFF_KEY_BLOCK
    groups = kb // SUBLANES
    nkb = seq // kb
    per_chunk = DIFF_KEY_CHUNK // kb
    width = bias_ref.shape[-1]
    row = lax.broadcasted_iota(jnp.int32, (LANES, 1), 0)

    def scores(t):
        qt = q_ref[t * tq:(t + 1) * tq, :].astype(F32).T
        qcat = jnp.concatenate([jnp.where(row < DIFF_HEAD_DIM, qt, 0.0),
                                jnp.where(row >= DIFF_HEAD_DIM, qt, 0.0)], axis=1).astype(BF16)
        q0 = (qi * tiles + t) * tq
        mx = jnp.full((SUBLANES, 2 * tq), NEG_INF, F32)
        for c in range(seq // DIFF_KEY_CHUNK):
            s = jnp.dot(k_ref[c * DIFF_KEY_CHUNK:(c + 1) * DIFF_KEY_CHUNK, :], qcat,
                        preferred_element_type=F32)
            for rr in range(per_chunk):
                r = c * per_chunk + rr
                start = pl.multiple_of(
                    jnp.clip(DIFF_NEAR_BLOCKS * kb - r * kb + q0, 0, width - tq), LANES)
                b = bias_ref[0, :, pl.ds(start, tq)]
                blk = s[rr * kb:(rr + 1) * kb, :]
                blk = jnp.concatenate([blk[:, :tq] + b, blk[:, tq:] + b], axis=1)
                mx = jnp.maximum(mx, jnp.max(blk.reshape(groups, SUBLANES, 2 * tq), axis=0))
                s_ref[t, r * kb:(r + 1) * kb, :] = blk
        return jnp.max(mx, axis=0, keepdims=True)

    def probs(t, m):
        sm = jnp.zeros((SUBLANES, 2 * tq), F32)
        acc = jnp.zeros((DIFF_V_DIM, 2 * tq), F32)
        kc = DIFF_VALUE_CHUNK
        for c in range(seq // kc):
            p = jnp.exp2(s_ref[t, c * kc:(c + 1) * kc, :] - m)
            sm = sm + jnp.sum(p.reshape(kc // SUBLANES, SUBLANES, 2 * tq), axis=0)
            acc = acc + jnp.dot(vt_ref[:, c * kc:(c + 1) * kc], p.astype(BF16),
                                preferred_element_type=F32)
        return jnp.sum(sm, axis=0, keepdims=True), acc

    def values(t, l_acc):
        l, acc = l_acc
        o = (acc[:, :tq] / l[:, :tq] - lam * (acc[:, tq:] / l[:, tq:])).T
        ms = jnp.mean(o * o, axis=-1, keepdims=True)
        o = o * lax.rsqrt(ms + RMS_EPS) * g_ref[...] * (1.0 - lambda_init)
        o_ref[t * tq:(t + 1) * tq, :] = o.astype(o_ref.dtype)

    m_next = scores(0)
    for t in range(tiles):
        m = m_next
        if t + 1 < tiles:
            m_next = scores(t + 1)
        values(t, probs(t, m))


def _toeplitz(t, rows, width):
    h, length = t.shape
    assert length == rows + width - 1
    flat = jnp.tile(jnp.pad(t, ((0, 0), (0, 1))), (1, rows))[:, :rows * length]
    return flat.reshape(h, rows, length)[:, :, rows - 1:]


def _diff_bias_strip(rel_diff, tq):
    kb = DIFF_KEY_BLOCK
    assert DIFF_NEAR_BLOCKS * kb - (tq - 1) > REL_MAX_DIST
    assert -DIFF_NEAR_BLOCKS_NEG * kb + kb - 1 < -REL_MAX_DIST
    width = tq + (DIFF_NEAR_BLOCKS + DIFF_NEAR_BLOCKS_NEG) * kb
    rel = jnp.arange(kb + width - 1) - (width - 1) + DIFF_NEAR_BLOCKS * kb
    t = rel_diff[_rel_bucket(rel)].T.astype(F32) * LOG2E
    return _toeplitz(t, kb, width)[:, ::-1, ::-1]


def _diff_attention(dq, dk, dv, lam_p, strip, subln_g, batch, seq, lambda_init):
    tq = DIFF_TQ
    tiles = min(DIFF_TILES_PER_STEP, seq // tq)
    ts = tiles * tq
    nq = seq // ts
    width = strip.shape[-1]
    est = (2 * ts * LANES * 2 * 2 + 2 * 2 * seq * LANES * 2 + 2 * DIFF_KEY_BLOCK * width * 4
           + seq * LANES * 2 + tiles * seq * 2 * tq * 4 + 2 * DIFF_KEY_CHUNK * 2 * tq * 4)
    return pl.pallas_call(
        functools.partial(_diff_attn_kernel, seq=seq, lambda_init=lambda_init),
        grid=(batch, DIFF_HEADS, nq),
        in_specs=[
            pl.BlockSpec((4, DIFF_HEAD_DIM), lambda b, h, i: (0, 0)),
            pl.BlockSpec((ts, LANES), lambda b, h, i: (b * nq + i, h)),
            pl.BlockSpec((seq, LANES), lambda b, h, i: (b, h)),
            pl.BlockSpec((seq, LANES), lambda b, h, i: (b, h)),
            pl.BlockSpec((1, DIFF_KEY_BLOCK, width), lambda b, h, i: (h, 0, 0)),
            pl.BlockSpec((1, DIFF_V_DIM), lambda b, h, i: (0, 0)),
        ],
        out_specs=pl.BlockSpec((ts, LANES), lambda b, h, i: (b * nq + i, h)),
        out_shape=jax.ShapeDtypeStruct((batch * seq, DIFF_WIDTH), BF16),
        scratch_shapes=[pltpu.VMEM((DIFF_V_DIM, seq), BF16),
                        pltpu.VMEM((tiles, seq, 2 * tq), F32)],
        compiler_params=pltpu.CompilerParams(
            dimension_semantics=("parallel", "parallel", "arbitrary"),
            vmem_limit_bytes=_vmem_limit(est)),
        name="diff_attn",
    )(lam_p, dq, dk, dv, strip, subln_g)


WIN_BAND = 3 * WIN_BLOCK
WIN_STRIP = 5 * WIN_BLOCK


def _win_attn_kernel(q_ref, k_ref, v_ref, strip_ref, sink_ref, o_ref, *, seq):
    row = lax.broadcasted_iota(jnp.int32, (LANES, 1), 0)
    lane = lax.broadcasted_iota(jnp.int32, (1, LANES), 1)
    blocks = q_ref.shape[0] // WIN_BLOCK
    for sub in range(blocks):
        _win_block(pl.program_id(1) * blocks + sub, sub, row, lane, q_ref, k_ref, v_ref, strip_ref,
                   sink_ref, o_ref, seq)


def _win_block(n, sub, row, lane, q_ref, k_ref, v_ref, strip_ref, sink_ref, o_ref, seq):
    rows = slice(sub * WIN_BLOCK, (sub + 1) * WIN_BLOCK)
    start = pl.multiple_of(jnp.clip(n * WIN_BLOCK - WIN_BLOCK, 0, seq - WIN_BAND), WIN_BLOCK)
    ustart = pl.multiple_of(start - n * WIN_BLOCK + 2 * WIN_BLOCK, WIN_BLOCK)
    for kv in range(WIN_KV_HEADS):
        kb = k_ref[pl.ds(start, WIN_BAND), kv * LANES:(kv + 1) * LANES]
        vb = v_ref[pl.ds(start, WIN_BAND), kv * LANES:(kv + 1) * LANES]
        cols = []
        for pair in range(WIN_GROUP // 2):
            c0 = kv * WIN_GROUP * WIN_HEAD_DIM + pair * LANES
            qt = q_ref[rows, c0:c0 + LANES].astype(F32).T
            cols.append(jnp.where(row < WIN_HEAD_DIM, qt, 0.0))
            cols.append(jnp.where(row >= WIN_HEAD_DIM, qt, 0.0))
        qcat = jnp.concatenate(cols, axis=1).astype(BF16)
        s = jnp.dot(kb, qcat, preferred_element_type=F32)
        s = s + strip_ref[kv, pl.ds(ustart, WIN_BAND), :]
        sink = sink_ref[kv]
        m = jnp.maximum(jnp.max(s, axis=0, keepdims=True), sink)
        p = jnp.exp2(s - m)
        den = jnp.sum(p, axis=0, keepdims=True) + jnp.exp2(sink - m)
        vt = vb.astype(F32).T.astype(BF16)
        o = (jnp.dot(vt, p.astype(BF16), preferred_element_type=F32) / den).T
        for pair in range(WIN_GROUP // 2):
            c0 = kv * WIN_GROUP * WIN_HEAD_DIM + pair * LANES
            even = o[(2 * pair) * WIN_BLOCK:(2 * pair + 1) * WIN_BLOCK, :]
            odd = o[(2 * pair + 1) * WIN_BLOCK:(2 * pair + 2) * WIN_BLOCK, :]
            o_ref[rows, c0:c0 + LANES] = jnp.where(lane < WIN_HEAD_DIM, even, odd).astype(o_ref.dtype)


def _win_bias_strip(rel_win):
    rel = jnp.arange(WIN_BLOCK + WIN_STRIP - 1) - (WIN_BLOCK - 1) - 2 * WIN_BLOCK
    t = jnp.where((jnp.abs(rel) <= WINDOW)[None],
                  rel_win[_rel_bucket(rel)].T.astype(F32) * LOG2E, NEG_INF)
    strip = _toeplitz(t, WIN_BLOCK, WIN_STRIP)
    strip = strip.reshape(WIN_KV_HEADS, WIN_GROUP, WIN_BLOCK, WIN_STRIP).transpose(0, 3, 1, 2)
    return strip.reshape(WIN_KV_HEADS, WIN_STRIP, WIN_GROUP * WIN_BLOCK)


def _win_attention(wq, wk, wv, sink, strip, batch, seq):
    heads_lanes = WIN_GROUP * WIN_BLOCK
    sink_lanes = jnp.broadcast_to((sink.astype(F32) * LOG2E).reshape(WIN_KV_HEADS, WIN_GROUP, 1),
                                  (WIN_KV_HEADS, WIN_GROUP, WIN_BLOCK)).reshape(
                                      WIN_KV_HEADS, 1, heads_lanes)
    rows = min(WIN_BLOCKS_PER_STEP * WIN_BLOCK, seq)
    assert seq % rows == 0
    nb = seq // rows
    est = (2 * rows * WIN_WIDTH * 2 * 2 + 2 * 2 * seq * 2 * LANES * 2
           + 2 * WIN_KV_HEADS * WIN_STRIP * heads_lanes * 4 + 8 * WIN_BAND * heads_lanes * 4)
    return pl.pallas_call(
        functools.partial(_win_attn_kernel, seq=seq),
        grid=(batch, nb),
        in_specs=[
            pl.BlockSpec((rows, WIN_WIDTH), lambda b, n: (b * nb + n, 0)),
            pl.BlockSpec((seq, 2 * LANES), lambda b, n: (b, 0)),
            pl.BlockSpec((seq, 2 * LANES), lambda b, n: (b, 0)),
            pl.BlockSpec((WIN_KV_HEADS, WIN_STRIP, heads_lanes), lambda b, n: (0, 0, 0)),
            pl.BlockSpec((WIN_KV_HEADS, 1, heads_lanes), lambda b, n: (0, 0, 0)),
        ],
        out_specs=pl.BlockSpec((rows, WIN_WIDTH), lambda b, n: (b * nb + n, 0)),
        out_shape=jax.ShapeDtypeStruct((batch * seq, WIN_WIDTH), BF16),
        compiler_params=pltpu.CompilerParams(
            dimension_semantics=("parallel", "parallel"), vmem_limit_bytes=_vmem_limit(est)),
        name="win_attn",
    )(wq, wk, wv, strip, sink_lanes)


ROUTE_E1, ROUTE_E2, ROUTE_W1, ROUTE_W2 = 0, 1, 2, 3


def _merge_kernel(x_ref, od_ref, ow_ref, gate_ref, pd_ref, pw_ref, wo_ref, g_ref, wr_ref, br_ref,
                  x1_ref, hn_ref, route_ref, counts_ref):
    @pl.when(pl.program_id(0) == 0)
    def _():
        counts_ref[...] = jnp.zeros_like(counts_ref)

    sub_rows = x_ref.shape[0] // MERGE_SUBTILES
    hits = [_merge_rows(slice(i * sub_rows, (i + 1) * sub_rows), x_ref, od_ref, ow_ref, gate_ref,
                        pd_ref, pw_ref, wo_ref, g_ref, wr_ref, br_ref, x1_ref, hn_ref, route_ref)
            for i in range(MERGE_SUBTILES)]
    counts_ref[...] += sum(hits[1:], hits[0])


def _merge_rows(rows, x_ref, od_ref, ow_ref, gate_ref, pd_ref, pw_ref, wo_ref, g_ref, wr_ref, br_ref,
                x1_ref, hn_ref, route_ref):
    md = jnp.dot(od_ref[rows, :], pd_ref[...], preferred_element_type=F32)
    mw = jnp.dot(ow_ref[rows, :], pw_ref[...], preferred_element_type=F32)
    merged = (gate_ref[rows, :D_MODEL].astype(F32) * md + gate_ref[rows, D_MODEL:].astype(F32) * mw)
    x1 = x_ref[rows, :] + jnp.dot(merged.astype(BF16), wo_ref[...], preferred_element_type=F32)
    x1_ref[rows, :] = x1
    ms = jnp.mean(x1 * x1, axis=-1, keepdims=True)
    hn = x1 * lax.rsqrt(ms + RMS_EPS) * g_ref[...]
    hn_ref[rows, :] = hn

    hn_hi = hn.astype(BF16)
    hn_lo = (hn - hn_hi.astype(F32)).astype(BF16)
    parts = (jnp.dot(hn_hi, wr_ref[...], preferred_element_type=F32)
             + jnp.dot(hn_lo, wr_ref[...], preferred_element_type=F32))
    logits = parts + pltpu.roll(parts, LANES // 2, axis=1) + br_ref[...]
    lane = lax.broadcasted_iota(jnp.int32, logits.shape, 1)
    big = jnp.int32(LANES)
    is_group = lane < N_GROUPS
    gl = jnp.where(is_group, logits, NEG_INF)
    gmax = jnp.max(gl, axis=-1, keepdims=True)
    gsel = jnp.min(jnp.where(gl == gmax, lane, big), axis=-1, keepdims=True)
    gden = jnp.sum(jnp.where(is_group, jnp.exp(gl - gmax), 0.0), axis=-1, keepdims=True)
    gw = 1.0 / gden
    lo = N_GROUPS + EXPERTS_PER_GROUP * gsel
    in_group = jnp.logical_and(lane >= lo, lane < lo + EXPERTS_PER_GROUP)
    el = jnp.where(in_group, logits, NEG_INF)
    t1 = jnp.max(el, axis=-1, keepdims=True)
    i1 = jnp.min(jnp.where(el == t1, lane, big), axis=-1, keepdims=True)
    el2 = jnp.where(lane == i1, NEG_INF, el)
    t2 = jnp.max(el2, axis=-1, keepdims=True)
    i2 = jnp.min(jnp.where(el2 == t2, lane, big), axis=-1, keepdims=True)
    z = jnp.exp(t2 - t1)
    w1 = gw / (1.0 + z)
    w2 = gw * z / (1.0 + z)
    e1 = (i1 - N_GROUPS).astype(F32)
    e2 = (i2 - N_GROUPS).astype(F32)
    route = jnp.where(lane == ROUTE_E1, e1,
                      jnp.where(lane == ROUTE_E2, e2,
                                jnp.where(lane == ROUTE_W1, w1,
                                          jnp.where(lane == ROUTE_W2, w2, 0.0))))
    route_ref[rows, :] = route
    hits = (jnp.where(lane == i1 - N_GROUPS, 1.0, 0.0) + jnp.where(lane == i2 - N_GROUPS, 1.0, 0.0))
    return jnp.sum(hits, axis=0, keepdims=True)


def _merge(x2, od, ow, gates, pd, pw, wo, g, wr, br):
    n = x2.shape[0]
    tm = ROW_TILE
    est = (2 * tm * D_MODEL * 4 * 3 + 2 * tm * (DIFF_WIDTH + WIN_WIDTH + C_GATE) * 2
           + 2 * (DIFF_WIDTH + WIN_WIDTH + D_MODEL) * D_MODEL * 2 + D_MODEL * LANES * 4 * 2
           + 6 * tm * D_MODEL * 4)
    row = lambda c: pl.BlockSpec((tm, c), lambda i: (i, 0))
    full = lambda r, c: pl.BlockSpec((r, c), lambda i: (0, 0))
    return pl.pallas_call(
        _merge_kernel,
        grid=(n // tm,),
        in_specs=[row(D_MODEL), row(DIFF_WIDTH), row(WIN_WIDTH), row(C_GATE),
                  full(DIFF_WIDTH, D_MODEL), full(WIN_WIDTH, D_MODEL), full(D_MODEL, D_MODEL),
                  full(1, D_MODEL), full(D_MODEL, LANES), full(1, LANES)],
        out_specs=[row(D_MODEL), row(D_MODEL), row(LANES), full(1, LANES)],
        out_shape=[jax.ShapeDtypeStruct((n, D_MODEL), F32), jax.ShapeDtypeStruct((n, D_MODEL), F32),
                   jax.ShapeDtypeStruct((n, LANES), F32), jax.ShapeDtypeStruct((1, LANES), F32)],
        compiler_params=pltpu.CompilerParams(
            dimension_semantics=("arbitrary",), vmem_limit_bytes=_vmem_limit(est)),
        name="merge_route",
    )(x2, od, ow, gates, pd, pw, wo, g, wr, br)


def _positions_kernel(route_ref, pstart_ref, dest_ref, carry_ref):
    i = pl.program_id(0)

    @pl.when(i == 0)
    def _():
        carry_ref[...] = jnp.zeros_like(carry_ref)

    r = route_ref[...]
    tb = r.shape[0]
    lane = lax.broadcasted_iota(jnp.int32, r.shape, 1)
    lane_f = lane.astype(F32)
    oh1 = lane_f == r[:, ROUTE_E1:ROUTE_E1 + 1]
    oh2 = lane_f == r[:, ROUTE_E2:ROUTE_E2 + 1]
    cnt = jnp.where(oh1, 1.0, 0.0) + jnp.where(oh2, 1.0, 0.0)
    rows = lax.broadcasted_iota(jnp.int32, (tb, tb), 0)
    cols = lax.broadcasted_iota(jnp.int32, (tb, tb), 1)
    tri = jnp.where(rows >= cols, 1.0, 0.0).astype(BF16)
    incl = jnp.dot(tri, cnt.astype(BF16), preferred_element_type=F32)
    row_of = incl - cnt + carry_ref[...] + pstart_ref[...]
    dest1 = jnp.sum(jnp.where(oh1, row_of, 0.0), axis=-1, keepdims=True)
    dest2 = jnp.sum(jnp.where(oh2, row_of, 0.0), axis=-1, keepdims=True)
    dest_ref[...] = jnp.where(lane == ROUTE_E1, dest1, jnp.where(lane == ROUTE_E2, dest2, 0.0))
    carry_ref[...] = carry_ref[...] + incl[tb - 1:tb, :]


def _positions(route, pad_start):
    n = route.shape[0]
    tb = POS_TILE
    return pl.pallas_call(
        _positions_kernel,
        grid=(n // tb,),
        in_specs=[pl.BlockSpec((tb, LANES), lambda i: (i, 0)),
                  pl.BlockSpec((1, LANES), lambda i: (0, 0))],
        out_specs=pl.BlockSpec((tb, LANES), lambda i: (i, 0)),
        out_shape=jax.ShapeDtypeStruct((n, LANES), F32),
        scratch_shapes=[pltpu.VMEM((1, LANES), F32)],
        compiler_params=pltpu.CompilerParams(dimension_semantics=("arbitrary",)),
        name="positions",
    )(route, pad_start)


def _dispatch_kernel(zoff_ref, na_ref, d1_ref, d2_ref, hn_ref, xs_hbm, zbuf, sem, zsem):
    ts = hn_ref.shape[0]
    zb = zbuf.shape[0]
    nblk = xs_hbm.shape[0] // zb

    @pl.when(pl.program_id(0) == 0)
    def _():
        zbuf[...] = jnp.zeros_like(zbuf)

        def fill(row0):
            return pltpu.make_async_copy(zbuf, xs_hbm.at[pl.ds(row0, zb)], zsem)

        for e in range(N_EXPERTS):
            @pl.when(zoff_ref[e] >= 0)
            def _():
                fill(pl.multiple_of(zoff_ref[e], zb)).start()

        def start_tail(j, carry):
            fill(pl.multiple_of(j * zb, zb)).start()
            return carry

        lax.fori_loop(na_ref[0], nblk, start_tail, 0)
        for e in range(N_EXPERTS):
            @pl.when(zoff_ref[e] >= 0)
            def _():
                fill(0).wait()

        def wait_tail(j, carry):
            fill(0).wait()
            return carry

        lax.fori_loop(na_ref[0], nblk, wait_tail, 0)

    for t in range(ts):
        pltpu.make_async_copy(hn_ref.at[pl.ds(t, 1)], xs_hbm.at[pl.ds(d1_ref[t], 1)],
                              sem).start(priority=0)
        pltpu.make_async_copy(hn_ref.at[pl.ds(t, 1)], xs_hbm.at[pl.ds(d2_ref[t], 1)],
                              sem).start(priority=1)
    for _ in range(2):
        pltpu.make_async_copy(hn_ref, xs_hbm.at[pl.ds(0, ts)], sem).wait()


def _dispatch(zero_off, n_active, dest1, dest2, hn, n_rows):
    n = hn.shape[0]
    ts = MOVE_TILE
    smem_blk = pl.BlockSpec((ts,), lambda i, zo, na: (i,), memory_space=pltpu.SMEM)
    est = 2 * ts * D_MODEL * 4 + MOE_TILE * D_MODEL * 4
    return pl.pallas_call(
        _dispatch_kernel,
        grid_spec=pltpu.PrefetchScalarGridSpec(
            num_scalar_prefetch=2,
            grid=(n // ts,),
            in_specs=[smem_blk, smem_blk, pl.BlockSpec((ts, D_MODEL), lambda i, zo, na: (i, 0))],
            out_specs=pl.BlockSpec(memory_space=pl.ANY),
            scratch_shapes=[pltpu.VMEM((MOE_TILE, D_MODEL), F32), pltpu.SemaphoreType.DMA(()),
                            pltpu.SemaphoreType.DMA(())],
        ),
        out_shape=jax.ShapeDtypeStruct((n_rows, D_MODEL), F32),
        compiler_params=pltpu.CompilerParams(
            dimension_semantics=("arbitrary",), vmem_limit_bytes=_vmem_limit(est)),
        name="dispatch",
    )(zero_off, n_active, dest1, dest2, hn)


def _experts_kernel(be_ref, na_ref, xs_ref, wg_ref, wu_ref, wd_ref, ys_ref, wg_bf, wu_bf, wd_bf):
    j = pl.program_id(0)
    active = j < na_ref[0]
    new_expert = jnp.logical_or(j == 0, be_ref[j] != be_ref[jnp.maximum(j - 1, 0)])

    @pl.when(jnp.logical_and(active, new_expert))
    def _():
        wg_bf[...] = wg_ref[0, 0].astype(BF16)
        wu_bf[...] = wu_ref[0, 0].astype(BF16)
        wd_bf[...] = wd_ref[0, 0].astype(BF16)

    @pl.when(active)
    def _():
        x = xs_ref[...].astype(BF16)
        g = jnp.dot(x, wg_bf[...], preferred_element_type=F32)
        u = jnp.dot(x, wu_bf[...], preferred_element_type=F32)
        hid = (g * jax.nn.sigmoid(g) * u).astype(BF16)
        ys_ref[...] = jnp.dot(hid, wd_bf[...], preferred_element_type=F32)

    @pl.when(j >= na_ref[0])
    def _():
        ys_ref[...] = jnp.zeros_like(ys_ref)


def _experts(block_e, n_active, xs, wg, wu, wd, layer):
    n_rows = xs.shape[0]
    tmb = MOE_TILE
    nblk = n_rows // tmb
    rows = pl.BlockSpec((tmb, D_MODEL), lambda j, be, na: (j, 0))
    est = (4 * tmb * D_MODEL * 4 + 3 * D_MODEL * EXPERT_HIDDEN * (2 * 4 + 2)
           + 4 * tmb * EXPERT_HIDDEN * 4 + tmb * D_MODEL * 4)
    return pl.pallas_call(
        _experts_kernel,
        grid_spec=pltpu.PrefetchScalarGridSpec(
            num_scalar_prefetch=2,
            grid=(nblk,),
            in_specs=[
                rows,
                pl.BlockSpec((1, 1, D_MODEL, EXPERT_HIDDEN), lambda j, be, na: (layer, be[j], 0, 0)),
                pl.BlockSpec((1, 1, D_MODEL, EXPERT_HIDDEN), lambda j, be, na: (layer, be[j], 0, 0)),
                pl.BlockSpec((1, 1, EXPERT_HIDDEN, D_MODEL), lambda j, be, na: (layer, be[j], 0, 0)),
            ],
            out_specs=rows,
            scratch_shapes=[pltpu.VMEM((D_MODEL, EXPERT_HIDDEN), BF16),
                            pltpu.VMEM((D_MODEL, EXPERT_HIDDEN), BF16),
                            pltpu.VMEM((EXPERT_HIDDEN, D_MODEL), BF16)],
        ),
        out_shape=jax.ShapeDtypeStruct((n_rows, D_MODEL), F32),
        compiler_params=pltpu.CompilerParams(
            dimension_semantics=("arbitrary",), vmem_limit_bytes=_vmem_limit(est)),
        name="experts",
    )(block_e, n_active, xs, wg, wu, wd)


def _plan_blocks(counts, n_rows):
    counts = counts[0, :N_EXPERTS].astype(jnp.int32)
    padded = (counts + MOE_TILE - 1) // MOE_TILE * MOE_TILE
    pad_end = jnp.cumsum(padded)
    pad_start = pad_end - padded
    blk_start = jnp.arange(n_rows // MOE_TILE, dtype=jnp.int32) * MOE_TILE
    block_e = jnp.minimum(jnp.sum((pad_end[None, :] <= blk_start[:, None]).astype(jnp.int32), axis=1),
                          N_EXPERTS - 1)
    n_active = (pad_end[-1:] // MOE_TILE).astype(jnp.int32)
    pad_start_lanes = jnp.zeros((1, LANES), F32).at[0, :N_EXPERTS].set(pad_start.astype(F32))
    zero_off = jnp.where(counts > 0, pad_end - MOE_TILE, -1).astype(jnp.int32)
    return pad_start_lanes, block_e, n_active, zero_off


def _forward(x, w_in, diff_lambda, diff_subln, win_sink, w_branch_diff, w_branch_win, w_out,
             rel_bias, norm_mix, norm_ffn, w_router_group, b_router_group, w_router_expert,
             b_router_expert, w_exp_gate, w_exp_up, w_exp_down, norm_final):
    batch, seq, _ = x.shape
    n = batch * seq
    depth = w_in.shape[0]
    assert seq % DIFF_KEY_CHUNK == 0 and seq % DIFF_TQ == 0 and seq >= WIN_BAND
    assert n % ROW_TILE == 0 and n % POS_TILE == 0 and n % MOVE_TILE == 0

    diff_strip = _diff_bias_strip(rel_bias[:, :DIFF_HEADS], DIFF_TQ)
    win_strip = _win_bias_strip(rel_bias[:, DIFF_HEADS:])
    n_rows = -(-(2 * n + N_EXPERTS * (MOE_TILE - 1)) // MOE_TILE) * MOE_TILE

    x2 = x.reshape(n, D_MODEL)
    moe = None
    for l in range(depth):
        splits = (512, 1024, 1536, 2048, 2176, 2304)
        wdq, wdk, wdv, wwq, wwk, wwv, wgt = jnp.split(w_in[l], splits, axis=-1)
        dup = lambda w: jnp.concatenate(
            [w[:, :WIN_HEAD_DIM], w[:, :WIN_HEAD_DIM], w[:, WIN_HEAD_DIM:], w[:, WIN_HEAD_DIM:]],
            axis=-1)
        w_all = jnp.concatenate([wdq, wdk, wdv, wwq, dup(wwk), dup(wwv), wgt], axis=-1).astype(BF16)

        if moe is None:
            dq, dk, dv, wq, wk, wv, gates = _inproj(x2, norm_mix[l][None], w_all)
        else:
            x2, dq, dk, dv, wq, wk, wv, gates = _combine_inproj(*moe, norm_mix[l][None], w_all)

        lambda_init = 0.8 - 0.6 * math.exp(-0.3 * l)
        o_diff = _diff_attention(dq, dk, dv, diff_lambda[l], diff_strip, diff_subln[l][None],
                                 batch, seq, lambda_init)
        o_win = _win_attention(wq, wk, wv, win_sink[l], win_strip, batch, seq)

        wr = jnp.zeros((D_MODEL, LANES // 2), F32)
        wr = wr.at[:, :N_GROUPS].set(w_router_group[l])
        wr = wr.at[:, N_GROUPS:N_GROUPS + N_EXPERTS].set(w_router_expert[l])
        wr_hi = wr.astype(BF16)
        wr_lo = (wr - wr_hi.astype(F32)).astype(BF16)
        wr = jnp.concatenate([wr_hi, wr_lo], axis=1)
        br = jnp.zeros((1, LANES), F32)
        br = br.at[0, :N_GROUPS].set(b_router_group[l])
        br = br.at[0, N_GROUPS:N_GROUPS + N_EXPERTS].set(b_router_expert[l])
        x1, hn, route, counts = _merge(x2, o_diff, o_win, gates,
                                       w_branch_diff[l].astype(BF16), w_branch_win[l].astype(BF16),
                                       w_out[l].astype(BF16), norm_ffn[l][None], wr, br)

        pad_start, block_e, n_active, zero_off = _plan_blocks(counts, n_rows)
        dest = _positions(route, pad_start)
        dest1 = dest[:, ROUTE_E1].astype(jnp.int32)
        dest2 = dest[:, ROUTE_E2].astype(jnp.int32)
        xs = _dispatch(zero_off, n_active, dest1, dest2, hn, n_rows)
        ys = _experts(block_e, n_active, xs, w_exp_gate, w_exp_up, w_exp_down, l)
        moe = (dest1, dest2, route, x1, ys)
    dest1, dest2, route, x1, ys = moe
    out = _combine_final(dest1, dest2, route, x1, ys, norm_final[None])
    return out.reshape(batch, seq, D_MODEL)


def kernel(x, w_in, diff_lambda, diff_subln, win_sink, w_branch_diff, w_branch_win, w_out, rel_bias, norm_mix, norm_ffn, w_router_group, b_router_group, w_router_expert, b_router_expert, w_exp_gate, w_exp_up, w_exp_down, norm_final):
    return _forward(x, w_in, diff_lambda, diff_subln, win_sink, w_branch_diff, w_branch_win, w_out,
                    rel_bias, norm_mix, norm_ffn, w_router_group, b_router_group, w_router_expert,
                    b_router_expert, w_exp_gate, w_exp_up, w_exp_down, norm_final)
```

```python
import functools
import math

import jax
import jax.numpy as jnp
from jax import lax
from jax.experimental import pallas as pl
from jax.experimental.pallas import tpu as pltpu

D_MODEL = 1024
DIFF_HEADS = 4
DIFF_HEAD_DIM = 64
DIFF_V_DIM = 2 * DIFF_HEAD_DIM
DIFF_WIDTH = DIFF_HEADS * DIFF_V_DIM
WIN_HEADS = 8
WIN_KV_HEADS = 2
WIN_GROUP = WIN_HEADS // WIN_KV_HEADS
WIN_HEAD_DIM = 64
WIN_WIDTH = WIN_HEADS * WIN_HEAD_DIM
WINDOW = 128
WIN_BLOCK = 128
REL_BUCKETS = 32
REL_MAX_DIST = 128
N_GROUPS = 4
EXPERTS_PER_GROUP = 8
N_EXPERTS = N_GROUPS * EXPERTS_PER_GROUP
EXPERT_HIDDEN = 512
RMS_EPS = 1e-6
NEG_INF = -1e30

LANES = 128
SUBLANES = 8
V7X_VMEM_BYTES = 64 * 1024 * 1024

ROW_TILE = 512
MERGE_SUBTILES = 2
DIFF_TQ = 256
DIFF_TILES_PER_STEP = 8
DIFF_KEY_BLOCK = 128
DIFF_KEY_CHUNK = 512
DIFF_VALUE_CHUNK = 256
DIFF_SUM_ROWS = 16
DIFF_NEAR_BLOCKS = 3
DIFF_NEAR_BLOCKS_NEG = 2
LOG2E = math.log2(math.e)
WIN_BLOCKS_PER_STEP = 8
POS_TILE = 512
MOE_TILE = 512
MOVE_TILE = 512

F32 = jnp.float32
BF16 = jnp.bfloat16


def _vmem_limit(nbytes):
    return int(min(max(2 * nbytes, 16 * 1024 * 1024), V7X_VMEM_BYTES - 8 * 1024 * 1024))


def _rel_bucket(rel):
    half = REL_BUCKETS // 2
    max_exact = half // 2
    n = jnp.abs(rel)
    nf = jnp.maximum(n, max_exact).astype(jnp.float32)
    large = max_exact + (jnp.log(nf / max_exact) / math.log(REL_MAX_DIST / max_exact)
                         * (half - max_exact)).astype(jnp.int32)
    large = jnp.minimum(large, half - 1)
    return jnp.where(rel > 0, half, 0) + jnp.where(n < max_exact, n, large)


C_DQ, C_DK, C_DV, C_WQ = 512, 512, 512, 512
C_WKD, C_WVD = 2 * LANES, 2 * LANES
C_GATE = 2 * D_MODEL
IN_COLS = C_DQ + C_DK + C_DV + C_WQ + C_WKD + C_WVD + C_GATE


def _project(x, g_ref, w_ref, dq_ref, dk_ref, dv_ref, wq_ref, wk_ref, wv_ref, gate_ref):
    ms = jnp.mean(x * x, axis=-1, keepdims=True)
    h = (x * lax.rsqrt(ms + RMS_EPS) * g_ref[...]).astype(BF16)

    col = 0

    def proj(width):
        nonlocal col
        out = jnp.dot(h, w_ref[:, col:col + width], preferred_element_type=F32)
        col += width
        return out

    dq_ref[...] = (proj(C_DQ) * (DIFF_HEAD_DIM ** -0.5 * LOG2E)).astype(BF16)
    dk_ref[...] = proj(C_DK).astype(BF16)
    dv_ref[...] = proj(C_DV).astype(BF16)
    wq_ref[...] = (proj(C_WQ) * (WIN_HEAD_DIM ** -0.5 * LOG2E)).astype(BF16)
    wk_ref[...] = proj(C_WKD).astype(BF16)
    wv_ref[...] = proj(C_WVD).astype(BF16)
    gate_ref[...] = jax.nn.sigmoid(proj(C_GATE)).astype(BF16)


def _inproj_kernel(x_ref, g_ref, w_ref, *out_refs):
    _project(x_ref[...], g_ref, w_ref, *out_refs)


PROJ_WIDTHS = (C_DQ, C_DK, C_DV, C_WQ, C_WKD, C_WVD, C_GATE)


def _inproj(x2, g, w):
    n = x2.shape[0]
    tm = ROW_TILE
    est = 2 * tm * D_MODEL * 4 + D_MODEL * IN_COLS * 2 + 2 * tm * IN_COLS * 2 + tm * IN_COLS * 4
    return pl.pallas_call(
        _inproj_kernel,
        grid=(n // tm,),
        in_specs=[
            pl.BlockSpec((tm, D_MODEL), lambda i: (i, 0)),
            pl.BlockSpec((1, D_MODEL), lambda i: (0, 0)),
            pl.BlockSpec((D_MODEL, IN_COLS), lambda i: (0, 0), pipeline_mode=pl.Buffered(1)),
        ],
        out_specs=[pl.BlockSpec((tm, c), lambda i: (i, 0)) for c in PROJ_WIDTHS],
        out_shape=[jax.ShapeDtypeStruct((n, c), BF16) for c in PROJ_WIDTHS],
        compiler_params=pltpu.CompilerParams(
            dimension_semantics=("parallel",), vmem_limit_bytes=_vmem_limit(est)),
        name="inproj",
    )(x2, g, w)


def _combined_rows(d1_next_ref, d2_next_ref, d1_first_ref, d2_first_ref, route_ref, x1_ref, ys_hbm,
                   buf, sem):
    i = pl.program_id(0)
    tm = x1_ref.shape[0]
    slot = i % 2

    def gather(i1_ref, i2_ref, dst):
        for t in range(tm):
            pltpu.make_async_copy(ys_hbm.at[pl.ds(i1_ref[t], 1)], buf.at[dst, pl.ds(t, 1)],
                                  sem.at[dst]).start(priority=0)
            pltpu.make_async_copy(ys_hbm.at[pl.ds(i2_ref[t], 1)], buf.at[dst, pl.ds(tm + t, 1)],
                                  sem.at[dst]).start(priority=1)

    @pl.when(i == 0)
    def _():
        gather(d1_first_ref, d2_first_ref, 0)

    @pl.when(i + 1 < pl.num_programs(0))
    def _():
        gather(d1_next_ref, d2_next_ref, 1 - slot)

    pltpu.make_async_copy(ys_hbm.at[pl.ds(0, 2 * tm)], buf.at[slot], sem.at[slot]).wait()
    r = route_ref[...]
    w1 = r[:, ROUTE_W1:ROUTE_W1 + 1]
    w2 = r[:, ROUTE_W2:ROUTE_W2 + 1]
    return x1_ref[...] + w1 * buf[slot, 0:tm, :] + w2 * buf[slot, tm:2 * tm, :]


def _combine_inproj_kernel(d1_next_ref, d2_next_ref, d1_first_ref, d2_first_ref, route_ref, x1_ref,
                           g_ref, w_ref, ys_hbm, x2_ref, *rest):
    out_refs, (buf, sem) = rest[:len(PROJ_WIDTHS)], rest[len(PROJ_WIDTHS):]
    x = _combined_rows(d1_next_ref, d2_next_ref, d1_first_ref, d2_first_ref, route_ref, x1_ref,
                       ys_hbm, buf, sem)
    x2_ref[...] = x
    _project(x, g_ref, w_ref, *out_refs)


def _combine_final_kernel(d1_next_ref, d2_next_ref, d1_first_ref, d2_first_ref, route_ref, x1_ref,
                          g_ref, ys_hbm, o_ref, buf, sem):
    x = _combined_rows(d1_next_ref, d2_next_ref, d1_first_ref, d2_first_ref, route_ref, x1_ref,
                       ys_hbm, buf, sem)
    ms = jnp.mean(x * x, axis=-1, keepdims=True)
    o_ref[...] = x * lax.rsqrt(ms + RMS_EPS) * g_ref[...]


def _combine_specs(n):
    tm = ROW_TILE
    nt = n // tm
    nxt = pl.BlockSpec((tm,), lambda i: (jnp.minimum(i + 1, nt - 1),), memory_space=pltpu.SMEM)
    first = pl.BlockSpec((tm,), lambda i: (0,), memory_space=pltpu.SMEM)
    specs = [nxt, nxt, first, first,
             pl.BlockSpec((tm, LANES), lambda i: (i, 0)),
             pl.BlockSpec((tm, D_MODEL), lambda i: (i, 0))]
    scratch = [pltpu.VMEM((2, 2 * tm, D_MODEL), F32), pltpu.SemaphoreType.DMA((2,))]
    return tm, nt, specs, scratch


def _combine_final(dest1, dest2, route, x1, ys, g):
    n = x1.shape[0]
    tm, nt, specs, scratch = _combine_specs(n)
    est = 4 * tm * D_MODEL * 4 + 2 * 2 * tm * D_MODEL * 4 + 4 * tm * D_MODEL * 4
    return pl.pallas_call(
        _combine_final_kernel,
        grid=(nt,),
        in_specs=specs + [pl.BlockSpec((1, D_MODEL), lambda i: (0, 0)),
                          pl.BlockSpec(memory_space=pl.ANY)],
        out_specs=pl.BlockSpec((tm, D_MODEL), lambda i: (i, 0)),
        out_shape=jax.ShapeDtypeStruct((n, D_MODEL), F32),
        scratch_shapes=scratch,
        compiler_params=pltpu.CompilerParams(
            dimension_semantics=("arbitrary",), vmem_limit_bytes=_vmem_limit(est)),
        name="combine_final",
    )(dest1, dest2, dest1, dest2, route, x1, g, ys)


def _combine_inproj(dest1, dest2, route, x1, ys, g, w):
    n = x1.shape[0]
    tm, nt, specs, scratch = _combine_specs(n)
    est = (4 * tm * D_MODEL * 4 + D_MODEL * IN_COLS * 2 + 2 * tm * IN_COLS * 2 + tm * IN_COLS * 4
           + 2 * 2 * tm * D_MODEL * 4)
    return pl.pallas_call(
        _combine_inproj_kernel,
        grid=(nt,),
        in_specs=specs + [
            pl.BlockSpec((1, D_MODEL), lambda i: (0, 0)),
            pl.BlockSpec((D_MODEL, IN_COLS), lambda i: (0, 0), pipeline_mode=pl.Buffered(1)),
            pl.BlockSpec(memory_space=pl.ANY),
        ],
        out_specs=[pl.BlockSpec((tm, D_MODEL), lambda i: (i, 0))]
        + [pl.BlockSpec((tm, c), lambda i: (i, 0)) for c in PROJ_WIDTHS],
        out_shape=[jax.ShapeDtypeStruct((n, D_MODEL), F32)]
        + [jax.ShapeDtypeStruct((n, c), BF16) for c in PROJ_WIDTHS],
        scratch_shapes=scratch,
        compiler_params=pltpu.CompilerParams(
            dimension_semantics=("arbitrary",), vmem_limit_bytes=_vmem_limit(est)),
        name="combine_inproj",
    )(dest1, dest2, dest1, dest2, route, x1, g, w, ys)


def _diff_attn_kernel(lam_ref, q_ref, k_ref, v_ref, bias_ref, g_ref, o_ref, vt_ref, s_ref, *,
                      seq, lambda_init):
    qi = pl.program_id(2)

    @pl.when(qi == 0)
    def _():
        vt_ref[0:DIFF_V_DIM, :] = v_ref[...].astype(F32).T.astype(BF16)
        ones_row = lax.broadcasted_iota(jnp.int32, (DIFF_SUM_ROWS, seq), 0) == 0
        vt_ref[DIFF_V_DIM:, :] = jnp.where(ones_row, 1.0, 0.0).astype(BF16)

    lp = lam_ref[...]
    lam = (jnp.exp(jnp.sum(lp[0:1] * lp[1:2], axis=-1, keepdims=True))
           - jnp.exp(jnp.sum(lp[2:3] * lp[3:4], axis=-1, keepdims=True)) + lambda_init)

    tq = DIFF_TQ
    tiles = q_ref.shape[0] // tq
    kb = DIFF_KEY_BLOCK
    groups = kb // SUBLANES
    nkb = seq // kb
    per_chunk = DIFF_KEY_CHUNK // kb
    width = bias_ref.shape[-1]
    row = lax.broadcasted_iota(jnp.int32, (LANES, 1), 0)

    def scores(t):
        qt = q_ref[t * tq:(t + 1) * tq, :].astype(F32).T
        qcat = jnp.concatenate([jnp.where(row < DIFF_HEAD_DIM, qt, 0.0),
                                jnp.where(row >= DIFF_HEAD_DIM, qt, 0.0)], axis=1).astype(BF16)
        q0 = (qi * tiles + t) * tq
        mx = jnp.full((SUBLANES, 2 * tq), NEG_INF, F32)
        for c in range(seq // DIFF_KEY_CHUNK):
            s = jnp.dot(k_ref[c * DIFF_KEY_CHUNK:(c + 1) * DIFF_KEY_CHUNK, :], qcat,
                        preferred_element_type=F32)
            for rr in range(per_chunk):
                r = c * per_chunk + rr
                start = pl.multiple_of(
                    jnp.clip(DIFF_NEAR_BLOCKS * kb - r * kb + q0, 0, width - tq), LANES)
                b = bias_ref[0, :, pl.ds(start, tq)]
                blk = s[rr * kb:(rr + 1) * kb, :]
                blk = jnp.concatenate([blk[:, :tq] + b, blk[:, tq:] + b], axis=1)
                mx = jnp.maximum(mx, jnp.max(blk.reshape(groups, SUBLANES, 2 * tq), axis=0))
                s_ref[t, r * kb:(r + 1) * kb, :] = blk
        return jnp.max(mx, axis=0, keepdims=True)

    def probs(t, m):
        acc = jnp.zeros((vt_ref.shape[0], 2 * tq), F32)
        kc = DIFF_VALUE_CHUNK
        for c in range(seq // kc):
            p = jnp.exp2(s_ref[t, c * kc:(c + 1) * kc, :] - m)
            acc = acc + jnp.dot(vt_ref[:, c * kc:(c + 1) * kc], p.astype(BF16),
                                preferred_element_type=F32)
        return acc[DIFF_V_DIM:DIFF_V_DIM + 1, :], acc[:DIFF_V_DIM, :]

    def values(t, l_acc):
        l, acc = l_acc
        o = (acc[:, :tq] / l[:, :tq] - lam * (acc[:, tq:] / l[:, tq:])).T
        ms = jnp.mean(o * o, axis=-1, keepdims=True)
        o = o * lax.rsqrt(ms + RMS_EPS) * g_ref[...] * (1.0 - lambda_init)
        o_ref[t * tq:(t + 1) * tq, :] = o.astype(o_ref.dtype)

    m_next = scores(0)
    for t in range(tiles):
        m = m_next
        if t + 1 < tiles:
            m_next = scores(t + 1)
        values(t, probs(t, m))


def _toeplitz(t, rows, width):
    h, length = t.shape
    assert length == rows + width - 1
    flat = jnp.tile(jnp.pad(t, ((0, 0), (0, 1))), (1, rows))[:, :rows * length]
    return flat.reshape(h, rows, length)[:, :, rows - 1:]


def _diff_bias_strip(rel_diff, tq):
    kb = DIFF_KEY_BLOCK
    assert DIFF_NEAR_BLOCKS * kb - (tq - 1) > REL_MAX_DIST
    assert -DIFF_NEAR_BLOCKS_NEG * kb + kb - 1 < -REL_MAX_DIST
    width = tq + (DIFF_NEAR_BLOCKS + DIFF_NEAR_BLOCKS_NEG) * kb
    rel = jnp.arange(kb + width - 1) - (width - 1) + DIFF_NEAR_BLOCKS * kb
    t = rel_diff[_rel_bucket(rel)].T.astype(F32) * LOG2E
    return _toeplitz(t, kb, width)[:, ::-1, ::-1]


def _diff_attention(dq, dk, dv, lam_p, strip, subln_g, batch, seq, lambda_init):
    tq = DIFF_TQ
    tiles = min(DIFF_TILES_PER_STEP, seq // tq)
    ts = tiles * tq
    nq = seq // ts
    width = strip.shape[-1]
    est = (2 * ts * LANES * 2 * 2 + 2 * 2 * seq * LANES * 2 + 2 * DIFF_KEY_BLOCK * width * 4
           + seq * LANES * 2 + tiles * seq * 2 * tq * 4 + 2 * DIFF_KEY_CHUNK * 2 * tq * 4)
    return pl.pallas_call(
        functools.partial(_diff_attn_kernel, seq=seq, lambda_init=lambda_init),
        grid=(batch, DIFF_HEADS, nq),
        in_specs=[
            pl.BlockSpec((4, DIFF_HEAD_DIM), lambda b, h, i: (0, 0)),
            pl.BlockSpec((ts, LANES), lambda b, h, i: (b * nq + i, h)),
            pl.BlockSpec((seq, LANES), lambda b, h, i: (b, h)),
            pl.BlockSpec((seq, LANES), lambda b, h, i: (b, h)),
            pl.BlockSpec((1, DIFF_KEY_BLOCK, width), lambda b, h, i: (h, 0, 0)),
            pl.BlockSpec((1, DIFF_V_DIM), lambda b, h, i: (0, 0)),
        ],
        out_specs=pl.BlockSpec((ts, LANES), lambda b, h, i: (b * nq + i, h)),
        out_shape=jax.ShapeDtypeStruct((batch * seq, DIFF_WIDTH), BF16),
        scratch_shapes=[pltpu.VMEM((DIFF_V_DIM + DIFF_SUM_ROWS, seq), BF16),
                        pltpu.VMEM((tiles, seq, 2 * tq), F32)],
        compiler_params=pltpu.CompilerParams(
            dimension_semantics=("parallel", "parallel", "arbitrary"),
            vmem_limit_bytes=_vmem_limit(est)),
        name="diff_attn",
    )(lam_p, dq, dk, dv, strip, subln_g)


WIN_BAND = 3 * WIN_BLOCK
WIN_STRIP = 5 * WIN_BLOCK


def _win_attn_kernel(q_ref, k_ref, v_ref, strip_ref, sink_ref, o_ref, *, seq):
    row = lax.broadcasted_iota(jnp.int32, (LANES, 1), 0)
    lane = lax.broadcasted_iota(jnp.int32, (1, LANES), 1)
    blocks = q_ref.shape[0] // WIN_BLOCK
    for sub in range(blocks):
        _win_block(pl.program_id(1) * blocks + sub, sub, row, lane, q_ref, k_ref, v_ref, strip_ref,
                   sink_ref, o_ref, seq)


def _win_block(n, sub, row, lane, q_ref, k_ref, v_ref, strip_ref, sink_ref, o_ref, seq):
    rows = slice(sub * WIN_BLOCK, (sub + 1) * WIN_BLOCK)
    start = pl.multiple_of(jnp.clip(n * WIN_BLOCK - WIN_BLOCK, 0, seq - WIN_BAND), WIN_BLOCK)
    ustart = pl.multiple_of(start - n * WIN_BLOCK + 2 * WIN_BLOCK, WIN_BLOCK)
    for kv in range(WIN_KV_HEADS):
        kb = k_ref[pl.ds(start, WIN_BAND), kv * LANES:(kv + 1) * LANES]
        vb = v_ref[pl.ds(start, WIN_BAND), kv * LANES:(kv + 1) * LANES]
        cols = []
        for pair in range(WIN_GROUP // 2):
            c0 = kv * WIN_GROUP * WIN_HEAD_DIM + pair * LANES
            qt = q_ref[rows, c0:c0 + LANES].astype(F32).T
            cols.append(jnp.where(row < WIN_HEAD_DIM, qt, 0.0))
            cols.append(jnp.where(row >= WIN_HEAD_DIM, qt, 0.0))
        qcat = jnp.concatenate(cols, axis=1).astype(BF16)
        s = jnp.dot(kb, qcat, preferred_element_type=F32)
        s = s + strip_ref[kv, pl.ds(ustart, WIN_BAND), :]
        sink = sink_ref[kv]
        m = jnp.maximum(jnp.max(s, axis=0, keepdims=True), sink)
        p = jnp.exp2(s - m)
        den = jnp.sum(p, axis=0, keepdims=True) + jnp.exp2(sink - m)
        vt = vb.astype(F32).T.astype(BF16)
        o = (jnp.dot(vt, p.astype(BF16), preferred_element_type=F32) / den).T
        for pair in range(WIN_GROUP // 2):
            c0 = kv * WIN_GROUP * WIN_HEAD_DIM + pair * LANES
            even = o[(2 * pair) * WIN_BLOCK:(2 * pair + 1) * WIN_BLOCK, :]
            odd = o[(2 * pair + 1) * WIN_BLOCK:(2 * pair + 2) * WIN_BLOCK, :]
            o_ref[rows, c0:c0 + LANES] = jnp.where(lane < WIN_HEAD_DIM, even, odd).astype(o_ref.dtype)


def _win_bias_strip(rel_win):
    rel = jnp.arange(WIN_BLOCK + WIN_STRIP - 1) - (WIN_BLOCK - 1) - 2 * WIN_BLOCK
    t = jnp.where((jnp.abs(rel) <= WINDOW)[None],
                  rel_win[_rel_bucket(rel)].T.astype(F32) * LOG2E, NEG_INF)
    strip = _toeplitz(t, WIN_BLOCK, WIN_STRIP)
    strip = strip.reshape(WIN_KV_HEADS, WIN_GROUP, WIN_BLOCK, WIN_STRIP).transpose(0, 3, 1, 2)
    return strip.reshape(WIN_KV_HEADS, WIN_STRIP, WIN_GROUP * WIN_BLOCK)


def _win_attention(wq, wk, wv, sink, strip, batch, seq):
    heads_lanes = WIN_GROUP * WIN_BLOCK
    sink_lanes = jnp.broadcast_to((sink.astype(F32) * LOG2E).reshape(WIN_KV_HEADS, WIN_GROUP, 1),
                                  (WIN_KV_HEADS, WIN_GROUP, WIN_BLOCK)).reshape(
                                      WIN_KV_HEADS, 1, heads_lanes)
    rows = min(WIN_BLOCKS_PER_STEP * WIN_BLOCK, seq)
    assert seq % rows == 0
    nb = seq // rows
    est = (2 * rows * WIN_WIDTH * 2 * 2 + 2 * 2 * seq * 2 * LANES * 2
           + 2 * WIN_KV_HEADS * WIN_STRIP * heads_lanes * 4 + 8 * WIN_BAND * heads_lanes * 4)
    return pl.pallas_call(
        functools.partial(_win_attn_kernel, seq=seq),
        grid=(batch, nb),
        in_specs=[
            pl.BlockSpec((rows, WIN_WIDTH), lambda b, n: (b * nb + n, 0)),
            pl.BlockSpec((seq, 2 * LANES), lambda b, n: (b, 0)),
            pl.BlockSpec((seq, 2 * LANES), lambda b, n: (b, 0)),
            pl.BlockSpec((WIN_KV_HEADS, WIN_STRIP, heads_lanes), lambda b, n: (0, 0, 0)),
            pl.BlockSpec((WIN_KV_HEADS, 1, heads_lanes), lambda b, n: (0, 0, 0)),
        ],
        out_specs=pl.BlockSpec((rows, WIN_WIDTH), lambda b, n: (b * nb + n, 0)),
        out_shape=jax.ShapeDtypeStruct((batch * seq, WIN_WIDTH), BF16),
        compiler_params=pltpu.CompilerParams(
            dimension_semantics=("parallel", "parallel"), vmem_limit_bytes=_vmem_limit(est)),
        name="win_attn",
    )(wq, wk, wv, strip, sink_lanes)


ROUTE_E1, ROUTE_E2, ROUTE_W1, ROUTE_W2 = 0, 1, 2, 3


def _merge_kernel(x_ref, od_ref, ow_ref, gate_ref, pd_ref, pw_ref, wo_ref, g_ref, wr_ref, br_ref,
                  x1_ref, hn_ref, route_ref, counts_ref):
    @pl.when(pl.program_id(0) == 0)
    def _():
        counts_ref[...] = jnp.zeros_like(counts_ref)

    sub_rows = x_ref.shape[0] // MERGE_SUBTILES
    hits = [_merge_rows(slice(i * sub_rows, (i + 1) * sub_rows), x_ref, od_ref, ow_ref, gate_ref,
                        pd_ref, pw_ref, wo_ref, g_ref, wr_ref, br_ref, x1_ref, hn_ref, route_ref)
            for i in range(MERGE_SUBTILES)]
    counts_ref[...] += sum(hits[1:], hits[0])


def _merge_rows(rows, x_ref, od_ref, ow_ref, gate_ref, pd_ref, pw_ref, wo_ref, g_ref, wr_ref, br_ref,
                x1_ref, hn_ref, route_ref):
    md = jnp.dot(od_ref[rows, :], pd_ref[...], preferred_element_type=F32)
    mw = jnp.dot(ow_ref[rows, :], pw_ref[...], preferred_element_type=F32)
    merged = (gate_ref[rows, :D_MODEL].astype(F32) * md + gate_ref[rows, D_MODEL:].astype(F32) * mw)
    x1 = x_ref[rows, :] + jnp.dot(merged.astype(BF16), wo_ref[...], preferred_element_type=F32)
    x1_ref[rows, :] = x1
    ms = jnp.mean(x1 * x1, axis=-1, keepdims=True)
    hn = x1 * lax.rsqrt(ms + RMS_EPS) * g_ref[...]
    hn_ref[rows, :] = hn

    hn_hi = hn.astype(BF16)
    hn_lo = (hn - hn_hi.astype(F32)).astype(BF16)
    parts = (jnp.dot(hn_hi, wr_ref[...], preferred_element_type=F32)
             + jnp.dot(hn_lo, wr_ref[...], preferred_element_type=F32))
    logits = parts + pltpu.roll(parts, LANES // 2, axis=1) + br_ref[...]
    lane = lax.broadcasted_iota(jnp.int32, logits.shape, 1)
    big = jnp.int32(LANES)
    is_group = lane < N_GROUPS
    gl = jnp.where(is_group, logits, NEG_INF)
    gmax = jnp.max(gl, axis=-1, keepdims=True)
    gsel = jnp.min(jnp.where(gl == gmax, lane, big), axis=-1, keepdims=True)
    gden = jnp.sum(jnp.where(is_group, jnp.exp(gl - gmax), 0.0), axis=-1, keepdims=True)
    gw = 1.0 / gden
    lo = N_GROUPS + EXPERTS_PER_GROUP * gsel
    in_group = jnp.logical_and(lane >= lo, lane < lo + EXPERTS_PER_GROUP)
    el = jnp.where(in_group, logits, NEG_INF)
    t1 = jnp.max(el, axis=-1, keepdims=True)
    i1 = jnp.min(jnp.where(el == t1, lane, big), axis=-1, keepdims=True)
    el2 = jnp.where(lane == i1, NEG_INF, el)
    t2 = jnp.max(el2, axis=-1, keepdims=True)
    i2 = jnp.min(jnp.where(el2 == t2, lane, big), axis=-1, keepdims=True)
    z = jnp.exp(t2 - t1)
    w1 = gw / (1.0 + z)
    w2 = gw * z / (1.0 + z)
    e1 = (i1 - N_GROUPS).astype(F32)
    e2 = (i2 - N_GROUPS).astype(F32)
    route = jnp.where(lane == ROUTE_E1, e1,
                      jnp.where(lane == ROUTE_E2, e2,
                                jnp.where(lane == ROUTE_W1, w1,
                                          jnp.where(lane == ROUTE_W2, w2, 0.0))))
    route_ref[rows, :] = route
    hits = (jnp.where(lane == i1 - N_GROUPS, 1.0, 0.0) + jnp.where(lane == i2 - N_GROUPS, 1.0, 0.0))
    return jnp.sum(hits, axis=0, keepdims=True)


def _merge(x2, od, ow, gates, pd, pw, wo, g, wr, br):
    n = x2.shape[0]
    tm = ROW_TILE
    est = (2 * tm * D_MODEL * 4 * 3 + 2 * tm * (DIFF_WIDTH + WIN_WIDTH + C_GATE) * 2
           + 2 * (DIFF_WIDTH + WIN_WIDTH + D_MODEL) * D_MODEL * 2 + D_MODEL * LANES * 4 * 2
           + 6 * tm * D_MODEL * 4)
    row = lambda c: pl.BlockSpec((tm, c), lambda i: (i, 0))
    full = lambda r, c: pl.BlockSpec((r, c), lambda i: (0, 0))
    return pl.pallas_call(
        _merge_kernel,
        grid=(n // tm,),
        in_specs=[row(D_MODEL), row(DIFF_WIDTH), row(WIN_WIDTH), row(C_GATE),
                  full(DIFF_WIDTH, D_MODEL), full(WIN_WIDTH, D_MODEL), full(D_MODEL, D_MODEL),
                  full(1, D_MODEL), full(D_MODEL, LANES), full(1, LANES)],
        out_specs=[row(D_MODEL), row(D_MODEL), row(LANES), full(1, LANES)],
        out_shape=[jax.ShapeDtypeStruct((n, D_MODEL), F32), jax.ShapeDtypeStruct((n, D_MODEL), F32),
                   jax.ShapeDtypeStruct((n, LANES), F32), jax.ShapeDtypeStruct((1, LANES), F32)],
        compiler_params=pltpu.CompilerParams(
            dimension_semantics=("arbitrary",), vmem_limit_bytes=_vmem_limit(est)),
        name="merge_route",
    )(x2, od, ow, gates, pd, pw, wo, g, wr, br)


def _positions_kernel(route_ref, pstart_ref, dest_ref, carry_ref):
    i = pl.program_id(0)

    @pl.when(i == 0)
    def _():
        carry_ref[...] = jnp.zeros_like(carry_ref)

    r = route_ref[...]
    tb = r.shape[0]
    lane = lax.broadcasted_iota(jnp.int32, r.shape, 1)
    lane_f = lane.astype(F32)
    oh1 = lane_f == r[:, ROUTE_E1:ROUTE_E1 + 1]
    oh2 = lane_f == r[:, ROUTE_E2:ROUTE_E2 + 1]
    cnt = jnp.where(oh1, 1.0, 0.0) + jnp.where(oh2, 1.0, 0.0)
    rows = lax.broadcasted_iota(jnp.int32, (tb, tb), 0)
    cols = lax.broadcasted_iota(jnp.int32, (tb, tb), 1)
    tri = jnp.where(rows >= cols, 1.0, 0.0).astype(BF16)
    incl = jnp.dot(tri, cnt.astype(BF16), preferred_element_type=F32)
    row_of = incl - cnt + carry_ref[...] + pstart_ref[...]
    dest1 = jnp.sum(jnp.where(oh1, row_of, 0.0), axis=-1, keepdims=True)
    dest2 = jnp.sum(jnp.where(oh2, row_of, 0.0), axis=-1, keepdims=True)
    dest_ref[...] = jnp.where(lane == ROUTE_E1, dest1, jnp.where(lane == ROUTE_E2, dest2, 0.0))
    carry_ref[...] = carry_ref[...] + incl[tb - 1:tb, :]


def _positions(route, pad_start):
    n = route.shape[0]
    tb = POS_TILE
    return pl.pallas_call(
        _positions_kernel,
        grid=(n // tb,),
        in_specs=[pl.BlockSpec((tb, LANES), lambda i: (i, 0)),
                  pl.BlockSpec((1, LANES), lambda i: (0, 0))],
        out_specs=pl.BlockSpec((tb, LANES), lambda i: (i, 0)),
        out_shape=jax.ShapeDtypeStruct((n, LANES), F32),
        scratch_shapes=[pltpu.VMEM((1, LANES), F32)],
        compiler_params=pltpu.CompilerParams(dimension_semantics=("arbitrary",)),
        name="positions",
    )(route, pad_start)


def _dispatch_kernel(zoff_ref, na_ref, d1_ref, d2_ref, hn_ref, xs_hbm, zbuf, sem, zsem):
    ts = hn_ref.shape[0]
    zb = zbuf.shape[0]
    nblk = xs_hbm.shape[0] // zb

    @pl.when(pl.program_id(0) == 0)
    def _():
        zbuf[...] = jnp.zeros_like(zbuf)

        def fill(row0):
            return pltpu.make_async_copy(zbuf, xs_hbm.at[pl.ds(row0, zb)], zsem)

        for e in range(N_EXPERTS):
            @pl.when(zoff_ref[e] >= 0)
            def _():
                fill(pl.multiple_of(zoff_ref[e], zb)).start()

        def start_tail(j, carry):
            fill(pl.multiple_of(j * zb, zb)).start()
            return carry

        lax.fori_loop(na_ref[0], nblk, start_tail, 0)
        for e in range(N_EXPERTS):
            @pl.when(zoff_ref[e] >= 0)
            def _():
                fill(0).wait()

        def wait_tail(j, carry):
            fill(0).wait()
            return carry

        lax.fori_loop(na_ref[0], nblk, wait_tail, 0)

    for t in range(ts):
        pltpu.make_async_copy(hn_ref.at[pl.ds(t, 1)], xs_hbm.at[pl.ds(d1_ref[t], 1)],
                              sem).start(priority=0)
        pltpu.make_async_copy(hn_ref.at[pl.ds(t, 1)], xs_hbm.at[pl.ds(d2_ref[t], 1)],
                              sem).start(priority=1)
    for _ in range(2):
        pltpu.make_async_copy(hn_ref, xs_hbm.at[pl.ds(0, ts)], sem).wait()


def _dispatch(zero_off, n_active, dest1, dest2, hn, n_rows):
    n = hn.shape[0]
    ts = MOVE_TILE
    smem_blk = pl.BlockSpec((ts,), lambda i, zo, na: (i,), memory_space=pltpu.SMEM)
    est = 2 * ts * D_MODEL * 4 + MOE_TILE * D_MODEL * 4
    return pl.pallas_call(
        _dispatch_kernel,
        grid_spec=pltpu.PrefetchScalarGridSpec(
            num_scalar_prefetch=2,
            grid=(n // ts,),
            in_specs=[smem_blk, smem_blk, pl.BlockSpec((ts, D_MODEL), lambda i, zo, na: (i, 0))],
            out_specs=pl.BlockSpec(memory_space=pl.ANY),
            scratch_shapes=[pltpu.VMEM((MOE_TILE, D_MODEL), F32), pltpu.SemaphoreType.DMA(()),
                            pltpu.SemaphoreType.DMA(())],
        ),
        out_shape=jax.ShapeDtypeStruct((n_rows, D_MODEL), F32),
        compiler_params=pltpu.CompilerParams(
            dimension_semantics=("arbitrary",), vmem_limit_bytes=_vmem_limit(est)),
        name="dispatch",
    )(zero_off, n_active, dest1, dest2, hn)


def _experts_kernel(be_ref, na_ref, xs_ref, wg_ref, wu_ref, wd_ref, ys_ref, wg_bf, wu_bf, wd_bf):
    j = pl.program_id(0)
    active = j < na_ref[0]
    new_expert = jnp.logical_or(j == 0, be_ref[j] != be_ref[jnp.maximum(j - 1, 0)])

    @pl.when(jnp.logical_and(active, new_expert))
    def _():
        wg_bf[...] = wg_ref[0, 0].astype(BF16)
        wu_bf[...] = wu_ref[0, 0].astype(BF16)
        wd_bf[...] = wd_ref[0, 0].astype(BF16)

    @pl.when(active)
    def _():
        x = xs_ref[...].astype(BF16)
        g = jnp.dot(x, wg_bf[...], preferred_element_type=F32)
        u = jnp.dot(x, wu_bf[...], preferred_element_type=F32)
        hid = (g * jax.nn.sigmoid(g) * u).astype(BF16)
        ys_ref[...] = jnp.dot(hid, wd_bf[...], preferred_element_type=F32)

    @pl.when(j >= na_ref[0])
    def _():
        ys_ref[...] = jnp.zeros_like(ys_ref)


def _experts(block_e, n_active, xs, wg, wu, wd, layer):
    n_rows = xs.shape[0]
    tmb = MOE_TILE
    nblk = n_rows // tmb
    rows = pl.BlockSpec((tmb, D_MODEL), lambda j, be, na: (j, 0))
    est = (4 * tmb * D_MODEL * 4 + 3 * D_MODEL * EXPERT_HIDDEN * (2 * 4 + 2)
           + 4 * tmb * EXPERT_HIDDEN * 4 + tmb * D_MODEL * 4)
    return pl.pallas_call(
        _experts_kernel,
        grid_spec=pltpu.PrefetchScalarGridSpec(
            num_scalar_prefetch=2,
            grid=(nblk,),
            in_specs=[
                rows,
                pl.BlockSpec((1, 1, D_MODEL, EXPERT_HIDDEN), lambda j, be, na: (layer, be[j], 0, 0)),
                pl.BlockSpec((1, 1, D_MODEL, EXPERT_HIDDEN), lambda j, be, na: (layer, be[j], 0, 0)),
                pl.BlockSpec((1, 1, EXPERT_HIDDEN, D_MODEL), lambda j, be, na: (layer, be[j], 0, 0)),
            ],
            out_specs=rows,
            scratch_shapes=[pltpu.VMEM((D_MODEL, EXPERT_HIDDEN), BF16),
                            pltpu.VMEM((D_MODEL, EXPERT_HIDDEN), BF16),
                            pltpu.VMEM((EXPERT_HIDDEN, D_MODEL), BF16)],
        ),
        out_shape=jax.ShapeDtypeStruct((n_rows, D_MODEL), F32),
        compiler_params=pltpu.CompilerParams(
            dimension_semantics=("arbitrary",), vmem_limit_bytes=_vmem_limit(est)),
        name="experts",
    )(block_e, n_active, xs, wg, wu, wd)


def _plan_blocks(counts, n_rows):
    counts = counts[0, :N_EXPERTS].astype(jnp.int32)
    padded = (counts + MOE_TILE - 1) // MOE_TILE * MOE_TILE
    pad_end = jnp.cumsum(padded)
    pad_start = pad_end - padded
    blk_start = jnp.arange(n_rows // MOE_TILE, dtype=jnp.int32) * MOE_TILE
    block_e = jnp.minimum(jnp.sum((pad_end[None, :] <= blk_start[:, None]).astype(jnp.int32), axis=1),
                          N_EXPERTS - 1)
    n_active = (pad_end[-1:] // MOE_TILE).astype(jnp.int32)
    pad_start_lanes = jnp.zeros((1, LANES), F32).at[0, :N_EXPERTS].set(pad_start.astype(F32))
    zero_off = jnp.where(counts > 0, pad_end - MOE_TILE, -1).astype(jnp.int32)
    return pad_start_lanes, block_e, n_active, zero_off


def _forward(x, w_in, diff_lambda, diff_subln, win_sink, w_branch_diff, w_branch_win, w_out,
             rel_bias, norm_mix, norm_ffn, w_router_group, b_router_group, w_router_expert,
             b_router_expert, w_exp_gate, w_exp_up, w_exp_down, norm_final):
    batch, seq, _ = x.shape
    n = batch * seq
    depth = w_in.shape[0]
    assert seq % DIFF_KEY_CHUNK == 0 and seq % DIFF_TQ == 0 and seq >= WIN_BAND
    assert n % ROW_TILE == 0 and n % POS_TILE == 0 and n % MOVE_TILE == 0

    diff_strip = _diff_bias_strip(rel_bias[:, :DIFF_HEADS], DIFF_TQ)
    win_strip = _win_bias_strip(rel_bias[:, DIFF_HEADS:])
    n_rows = -(-(2 * n + N_EXPERTS * (MOE_TILE - 1)) // MOE_TILE) * MOE_TILE

    x2 = x.reshape(n, D_MODEL)
    moe = None
    for l in range(depth):
        splits = (512, 1024, 1536, 2048, 2176, 2304)
        wdq, wdk, wdv, wwq, wwk, wwv, wgt = jnp.split(w_in[l], splits, axis=-1)
        dup = lambda w: jnp.concatenate(
            [w[:, :WIN_HEAD_DIM], w[:, :WIN_HEAD_DIM], w[:, WIN_HEAD_DIM:], w[:, WIN_HEAD_DIM:]],
            axis=-1)
        w_all = jnp.concatenate([wdq, wdk, wdv, wwq, dup(wwk), dup(wwv), wgt], axis=-1).astype(BF16)

        if moe is None:
            dq, dk, dv, wq, wk, wv, gates = _inproj(x2, norm_mix[l][None], w_all)
        else:
            x2, dq, dk, dv, wq, wk, wv, gates = _combine_inproj(*moe, norm_mix[l][None], w_all)

        lambda_init = 0.8 - 0.6 * math.exp(-0.3 * l)
        o_diff = _diff_attention(dq, dk, dv, diff_lambda[l], diff_strip, diff_subln[l][None],
                                 batch, seq, lambda_init)
        o_win = _win_attention(wq, wk, wv, win_sink[l], win_strip, batch, seq)

        wr = jnp.zeros((D_MODEL, LANES // 2), F32)
        wr = wr.at[:, :N_GROUPS].set(w_router_group[l])
        wr = wr.at[:, N_GROUPS:N_GROUPS + N_EXPERTS].set(w_router_expert[l])
        wr_hi = wr.astype(BF16)
        wr_lo = (wr - wr_hi.astype(F32)).astype(BF16)
        wr = jnp.concatenate([wr_hi, wr_lo], axis=1)
        br = jnp.zeros((1, LANES), F32)
        br = br.at[0, :N_GROUPS].set(b_router_group[l])
        br = br.at[0, N_GROUPS:N_GROUPS + N_EXPERTS].set(b_router_expert[l])
        x1, hn, route, counts = _merge(x2, o_diff, o_win, gates,
                                       w_branch_diff[l].astype(BF16), w_branch_win[l].astype(BF16),
                                       w_out[l].astype(BF16), norm_ffn[l][None], wr, br)

        pad_start, block_e, n_active, zero_off = _plan_blocks(counts, n_rows)
        dest = _positions(route, pad_start)
        dest1 = dest[:, ROUTE_E1].astype(jnp.int32)
        dest2 = dest[:, ROUTE_E2].astype(jnp.int32)
        xs = _dispatch(zero_off, n_active, dest1, dest2, hn, n_rows)
        ys = _experts(block_e, n_active, xs, w_exp_gate, w_exp_up, w_exp_down, l)
        moe = (dest1, dest2, route, x1, ys)
    dest1, dest2, route, x1, ys = moe
    out = _combine_final(dest1, dest2, route, x1, ys, norm_final[None])
    return out.reshape(batch, seq, D_MODEL)


def kernel(x, w_in, diff_lambda, diff_subln, win_sink, w_branch_diff, w_branch_win, w_out, rel_bias, norm_mix, norm_ffn, w_router_group, b_router_group, w_router_expert, b_router_expert, w_exp_gate, w_exp_up, w_exp_down, norm_final):
    return _forward(x, w_in, diff_lambda, diff_subln, win_sink, w_branch_diff, w_branch_win, w_out,
                    rel_bias, norm_mix, norm_ffn, w_router_group, b_router_group, w_router_expert,
                    b_router_expert, w_exp_gate, w_exp_up, w_exp_down, norm_final)
```
